```python
import math
import jax, jax.numpy as jnp
from jax import lax
import numpy as np

D_MODEL = 1024
BATCH = 8
SEQ = 2048
DEPTH = 4
DEC_BATCH = 128
DEC_SEQ = 8
PAST_LEN = 16384
PAGE_SIZE = 128

CONV_WIDTH = 512
CONV_K = 3
RET_HEADS = 4
RET_DK = 128
RET_DV = 256
RET_QK_WIDTH = RET_HEADS * RET_DK
RET_V_WIDTH = RET_HEADS * RET_DV
RET_CHUNK = 128
MEM_LEN = 256
MEM_HEADS = 4
MEM_HEAD_DIM = 128
MEM_WIDTH = MEM_HEADS * MEM_HEAD_DIM
N_BRANCH = 3
D_FF = 4 * D_MODEL
ROPE_BASE = 10000.0
LN_EPS = 1e-5
GN_EPS = 1e-6
DEEPNORM_ALPHA = (2 * DEPTH) ** 0.25
DEEPNORM_BETA = (8 * DEPTH) ** -0.25
SPLITS = (CONV_WIDTH, CONV_WIDTH, CONV_WIDTH, RET_QK_WIDTH, RET_QK_WIDTH,
          RET_V_WIDTH, RET_V_WIDTH, MEM_WIDTH, N_BRANCH * D_MODEL)
IN_WIDTH = sum(SPLITS)

kernel_name = "hybrid_conv_retention_memory_deepnorm_step"


def layer_norm(x, g, b):
    xf = x.astype(jnp.float32)
    mu = xf.mean(-1, keepdims=True)
    var = jnp.square(xf - mu).mean(-1, keepdims=True)
    y = (xf - mu) * lax.rsqrt(var + LN_EPS) * g.astype(jnp.float32) + b.astype(jnp.float32)
    return y.astype(x.dtype)


def rotary(x, pos):
    half = x.shape[-1] // 2
    inv = ROPE_BASE ** (-jnp.arange(half, dtype=jnp.float32) / half)
    ang = pos.astype(jnp.float32)[:, None] * inv[None, :]
    cos = jnp.cos(ang)[None, :, None, :]
    sin = jnp.sin(ang)[None, :, None, :]
    xf = x.astype(jnp.float32)
    x1, x2 = xf[..., :half], xf[..., half:]
    return jnp.concatenate([x1 * cos - x2 * sin, x1 * sin + x2 * cos], axis=-1)


def short_conv(u, buf, w):
    t = u.shape[1]
    full = jnp.concatenate([buf.astype(u.dtype), u], axis=1)
    y = full[:, 0:t] * w[0]
    for j in range(1, CONV_K):
        y = y + full[:, j:j + t] * w[j]
    return y, full[:, -(CONV_K - 1):]


def retention(q, k, v, s0):
    b, t, h, _ = q.shape
    c = RET_CHUNK if t % RET_CHUNK == 0 else t
    n = t // c
    log_g = jnp.log1p(-jnp.exp2(-5.0 - jnp.arange(h, dtype=jnp.float32)))
    idx = jnp.arange(c, dtype=jnp.float32)
    rel = idx[:, None] - idx[None, :]
    causal = rel >= 0
    dmask = jnp.where(causal[None], jnp.exp(jnp.where(causal, rel, 0.0)[None] * log_g[:, None, None]), 0.0)
    q_decay = jnp.exp((idx + 1.0)[None, :] * log_g[:, None])
    k_decay = jnp.exp((c - 1.0 - idx)[None, :] * log_g[:, None])
    chunk_decay = jnp.exp(c * log_g)

    def to_chunks(a):
        return a.reshape(b, n, c, h, a.shape[-1]).transpose(1, 0, 3, 2, 4)

    def step(s, inp):
        qi, ki, vi = inp
        scores = jnp.einsum('bhid,bhjd->bhij', qi, ki) * dmask[None]
        o = (jnp.einsum('bhij,bhje->bhie', scores, vi)
             + jnp.einsum('bhid,bhde->bhie', qi, s) * q_decay[None, :, :, None])
        s = (s * chunk_decay[None, :, None, None]
             + jnp.einsum('bhjd,bhje->bhde', ki * k_decay[None, :, :, None], vi))
        return s, o

    s_final, o = lax.scan(step, s0, (to_chunks(q), to_chunks(k), to_chunks(v)))
    o = o.transpose(1, 0, 3, 2, 4).reshape(b, t, h, v.shape[-1])
    return o, s_final


def mem_kv(mem, w_mem_kv_l):
    b, m, _ = mem.shape
    kv = mem @ w_mem_kv_l
    k, v = jnp.split(kv, 2, axis=-1)
    return (k.reshape(b, m, MEM_HEADS, MEM_HEAD_DIM), v.reshape(b, m, MEM_HEADS, MEM_HEAD_DIM))


def mem_attention(q, mk, mv):
    s = jnp.einsum('bthd,bmhd->bhtm', q, mk).astype(jnp.float32) * (MEM_HEAD_DIM ** -0.5)
    p = jax.nn.softmax(s, axis=-1).astype(mv.dtype)
    return jnp.einsum('bhtm,bmhd->bthd', p, mv)


def trunk_layer(x, pos, conv_buf, ret_state, mk, mv, w_in, b_gate, conv_w, ret_gn_g,
                w_conv_out, w_ret_out, w_mem_out, w_out, ln1_g, ln1_b, w_up, w_down, ln2_g, ln2_b):
    b, t, _ = x.shape
    split_idx = np.cumsum(np.array(SPLITS))[:-1].tolist()
    proj = x @ w_in
    cb, cc, ch, rq, rk, rv, rg, mq, gl = jnp.split(proj, split_idx, axis=-1)
    conv_y, new_buf = short_conv(cc * ch, conv_buf, conv_w)
    a_out = (cb * conv_y) @ w_conv_out
    q = rotary(rq.reshape(b, t, RET_HEADS, RET_DK), pos)
    k = rotary(rk.reshape(b, t, RET_HEADS, RET_DK), pos) * (RET_DK ** -0.5)
    v = rv.reshape(b, t, RET_HEADS, RET_DV).astype(jnp.float32)
    o, new_state = retention(q, k, v, ret_state.astype(jnp.float32))
    mu = o.mean(-1, keepdims=True)
    var = jnp.square(o - mu).mean(-1, keepdims=True)
    o = ((o - mu) * lax.rsqrt(var + GN_EPS)).reshape(b, t, RET_V_WIDTH) * ret_gn_g.astype(jnp.float32)
    b_out = (jax.nn.silu(rg.astype(jnp.float32)) * o).astype(x.dtype) @ w_ret_out
    m_out = mem_attention(mq.reshape(b, t, MEM_HEADS, MEM_HEAD_DIM), mk, mv).reshape(b, t, MEM_WIDTH) @ w_mem_out
    g = jax.nn.sigmoid((gl + b_gate).astype(jnp.float32)).astype(x.dtype).reshape(b, t, N_BRANCH, D_MODEL)
    merged = g[:, :, 0] * a_out + g[:, :, 1] * b_out + g[:, :, 2] * m_out
    x = layer_norm(DEEPNORM_ALPHA * x + merged @ w_out, ln1_g, ln1_b)
    hdn = jnp.square(jax.nn.relu(x @ w_up))
    x = layer_norm(DEEPNORM_ALPHA * x + hdn @ w_down, ln2_g, ln2_b)
    return x, new_buf, new_state.astype(ret_state.dtype)


def setup_inputs(seed: int = 0) -> dict:
    key = jax.random.key(seed)
    ks = jax.random.split(key, 24)
    f32 = jnp.float32
    nrm = lambda k, shape, scale: jax.random.normal(k, shape, f32) * scale
    return {
        "x_prompt": nrm(ks[0], (BATCH, SEQ, D_MODEL), 1.0),
        "x_sample": nrm(ks[1], (DEC_BATCH, DEC_SEQ, D_MODEL), 1.0),
        "cache_mem_k": nrm(ks[2], (DEPTH, DEC_BATCH, MEM_LEN, MEM_HEADS, MEM_HEAD_DIM), 1.0),
        "cache_mem_v": nrm(ks[3], (DEPTH, DEC_BATCH, MEM_LEN, MEM_HEADS, MEM_HEAD_DIM), 1.0),
        "state_conv": nrm(ks[4], (DEPTH, DEC_BATCH, CONV_K - 1, CONV_WIDTH), 1.0),
        "state_ret": nrm(ks[5], (DEPTH, DEC_BATCH, RET_HEADS, RET_DK, RET_DV), RET_DK ** -0.5),
        "mem_prompt": nrm(ks[6], (BATCH, MEM_LEN, D_MODEL), 1.0),
        "w_in": nrm(ks[7], (DEPTH, D_MODEL, IN_WIDTH), D_MODEL ** -0.5),
        "b_gate": nrm(ks[8], (DEPTH, N_BRANCH * D_MODEL), 0.02),
        "conv_w": nrm(ks[9], (DEPTH, CONV_K, CONV_WIDTH), CONV_K ** -0.5),
        "ret_gn_g": 1.0 + nrm(ks[10], (DEPTH, RET_V_WIDTH), 0.02),
        "w_conv_out": nrm(ks[11], (DEPTH, CONV_WIDTH, D_MODEL), CONV_WIDTH ** -0.5),
        "w_ret_out": nrm(ks[12], (DEPTH, RET_V_WIDTH, D_MODEL), RET_V_WIDTH ** -0.5),
        "w_mem_out": nrm(ks[13], (DEPTH, MEM_WIDTH, D_MODEL), MEM_WIDTH ** -0.5),
        "w_out": nrm(ks[14], (DEPTH, D_MODEL, D_MODEL), DEEPNORM_BETA * D_MODEL ** -0.5),
        "w_mem_kv": nrm(ks[15], (DEPTH, D_MODEL, 2 * MEM_WIDTH), D_MODEL ** -0.5),
        "ln1_g": 1.0 + nrm(ks[16], (DEPTH, D_MODEL), 0.02),
        "ln1_b": nrm(ks[17], (DEPTH, D_MODEL), 0.02),
        "w_up": nrm(ks[18], (DEPTH, D_MODEL, D_FF), D_MODEL ** -0.5),
        "w_down": nrm(ks[19], (DEPTH, D_FF, D_MODEL), DEEPNORM_BETA * D_FF ** -0.5),
        "ln2_g": 1.0 + nrm(ks[20], (DEPTH, D_MODEL), 0.02),
        "ln2_b": nrm(ks[21], (DEPTH, D_MODEL), 0.02),
    }


def reference(x_prompt, x_sample, cache_mem_k, cache_mem_v, state_conv, state_ret, mem_prompt,
              w_in, b_gate, conv_w, ret_gn_g, w_conv_out, w_ret_out, w_mem_out, w_out, w_mem_kv,
              ln1_g, ln1_b, w_up, w_down, ln2_g, ln2_b):
    pos_p = jnp.arange(x_prompt.shape[1], dtype=jnp.int32)
    pos_s = PAST_LEN + jnp.arange(x_sample.shape[1], dtype=jnp.int32)
    bp = x_prompt.shape[0]
    xp, xs = x_prompt, x_sample
    mk_list, mv_list, cp_list, rp_list, cs_list, rs_list = [], [], [], [], [], []
    for l in range(DEPTH):
        lw = (w_in[l], b_gate[l], conv_w[l], ret_gn_g[l], w_conv_out[l], w_ret_out[l], w_mem_out[l],
              w_out[l], ln1_g[l], ln1_b[l], w_up[l], w_down[l], ln2_g[l], ln2_b[l])
        mk_p, mv_p = mem_kv(mem_prompt, w_mem_kv[l])
        conv0 = jnp.zeros((bp, CONV_K - 1, CONV_WIDTH), xp.dtype)
        ret0 = jnp.zeros((bp, RET_HEADS, RET_DK, RET_DV), state_ret.dtype)
        xp, cbuf_p, rst_p = trunk_layer(xp, pos_p, conv0, ret0, mk_p, mv_p, *lw)
        xs, cbuf_s, rst_s = trunk_layer(xs, pos_s, state_conv[l], state_ret[l],
                                        cache_mem_k[l], cache_mem_v[l], *lw)
        mk_list.append(mk_p)
        mv_list.append(mv_p)
        cp_list.append(cbuf_p)
        rp_list.append(rst_p)
        cs_list.append(cbuf_s)
        rs_list.append(rst_s)
    return (xp, xs, jnp.stack(mk_list), jnp.stack(mv_list), jnp.stack(cp_list),
            jnp.stack(rp_list), jnp.stack(cs_list), jnp.stack(rs_list))
```

```python
import functools

import jax
import jax.numpy as jnp
from jax import lax
from jax.experimental import pallas as pl
from jax.experimental.pallas import tpu as pltpu

D_MODEL = 1024
DEPTH = 4
CONV_WIDTH = 512
CONV_K = 3
RET_HEADS = 4
RET_DK = 128
RET_DV = 256
RET_QK_WIDTH = RET_HEADS * RET_DK
RET_V_WIDTH = RET_HEADS * RET_DV
RET_CHUNK = 128
MEM_LEN = 256
MEM_HEADS = 4
MEM_HEAD_DIM = 128
MEM_WIDTH = MEM_HEADS * MEM_HEAD_DIM
N_BRANCH = 3
D_FF = 4 * D_MODEL
ROPE_BASE = 10000.0
LN_EPS = 1e-5
GN_EPS = 1e-6
PAST_LEN = 16384
DEEPNORM_ALPHA = (2 * DEPTH) ** 0.25

OFF_CB = 0
OFF_CC = OFF_CB + CONV_WIDTH
OFF_CH = OFF_CC + CONV_WIDTH
OFF_RQ = OFF_CH + CONV_WIDTH
OFF_RK = OFF_RQ + RET_QK_WIDTH
OFF_RV = OFF_RK + RET_QK_WIDTH
OFF_RG = OFF_RV + RET_V_WIDTH
OFF_MQ = OFF_RG + RET_V_WIDTH
OFF_GL = OFF_MQ + MEM_WIDTH
IN_WIDTH = OFF_GL + N_BRANCH * D_MODEL

V7X_VMEM_BYTES = 64 * 1024 * 1024
VMEM_LIMIT_BYTES = V7X_VMEM_BYTES - 6 * 1024 * 1024

PROMPT_TILE = 256
FFN_TILE = 512
FFN_CHUNK = 1024
SAMPLE_SEQS = 8
SAMPLE_POST_TILE = 256

BF16 = jnp.bfloat16
F32 = jnp.float32


def _dot(a, b):
    return jnp.dot(a, b, preferred_element_type=F32)


def _dot_nt(a, b):
    return lax.dot_general(a, b, (((1,), (1,)), ((), ())), preferred_element_type=F32)


def _dot_tn(a, b):
    return lax.dot_general(a, b, (((0,), (0,)), ((), ())), preferred_element_type=F32)


def _layer_norm(z, g, b):
    mu = jnp.mean(z, axis=-1, keepdims=True)
    zc = z - mu
    var = jnp.mean(zc * zc, axis=-1, keepdims=True)
    return zc * lax.rsqrt(var + LN_EPS) * g + b


def _group_norm(o):
    mu = jnp.mean(o, axis=-1, keepdims=True)
    oc = o - mu
    var = jnp.mean(oc * oc, axis=-1, keepdims=True)
    return oc * lax.rsqrt(var + GN_EPS)


def _rotary(xh, cos, sin_signed):
    return xh * cos + pltpu.roll(xh, RET_DK // 2, axis=1) * sin_signed


def _softmax_rows(s):
    m = jnp.max(s, axis=-1, keepdims=True)
    e = jnp.exp(s - m)
    return e / jnp.sum(e, axis=-1, keepdims=True)


def _merge_out_ln(x, a_pre, b_pre, m_pre, gl, bg_ref, wco_ref, wro_ref, wmo_ref, wo_ref, g_ref, b_ref):
    a_out = _dot(a_pre, wco_ref[...])
    b_out = _dot(b_pre, wro_ref[...])
    m_out = _dot(m_pre, wmo_ref[...])
    g0 = jax.nn.sigmoid(gl[0] + bg_ref[:, 0:D_MODEL])
    g1 = jax.nn.sigmoid(gl[1] + bg_ref[:, D_MODEL:2 * D_MODEL])
    g2 = jax.nn.sigmoid(gl[2] + bg_ref[:, 2 * D_MODEL:3 * D_MODEL])
    merged = g0 * a_out + g1 * b_out + g2 * m_out
    z = DEEPNORM_ALPHA * x + _dot(merged.astype(BF16), wo_ref[...])
    return _layer_norm(z, g_ref[...], b_ref[...])


def _ffn_ln(x, wup_ref, wdn_ref, g_ref, b_ref):
    xb = x.astype(BF16)
    acc = None
    for c in range(D_FF // FFN_CHUNK):
        cols = slice(c * FFN_CHUNK, (c + 1) * FFN_CHUNK)
        h = jnp.maximum(_dot(xb, wup_ref[:, cols]), 0.0)
        part = _dot((h * h).astype(BF16), wdn_ref[cols, :])
        acc = part if acc is None else acc + part
    return _layer_norm(DEEPNORM_ALPHA * x + acc, g_ref[...], b_ref[...])


def _mem_kv_kernel(mem_ref, w_ref, k_ref, v_ref, kb_ref, vb_ref):
    kv = _dot(mem_ref[...].astype(BF16), w_ref[...])
    k = kv[:, :MEM_WIDTH]
    v = kv[:, MEM_WIDTH:]
    k_ref[...] = k
    v_ref[...] = v
    kb_ref[...] = k.astype(BF16)
    vb_ref[...] = v.astype(BF16)


def _mem_kv(mem_prompt, w_mem_kv_b):
    bp = mem_prompt.shape[0]
    out_f = jax.ShapeDtypeStruct((DEPTH, bp, MEM_LEN, MEM_WIDTH), F32)
    out_b = jax.ShapeDtypeStruct((DEPTH, bp, MEM_LEN, MEM_WIDTH), BF16)
    o_spec = pl.BlockSpec((None, None, MEM_LEN, MEM_WIDTH), lambda l, b: (l, b, 0, 0))
    return pl.pallas_call(
        _mem_kv_kernel,
        grid=(DEPTH, bp),
        in_specs=[
            pl.BlockSpec((None, MEM_LEN, D_MODEL), lambda l, b: (b, 0, 0)),
            pl.BlockSpec((None, D_MODEL, 2 * MEM_WIDTH), lambda l, b: (l, 0, 0)),
        ],
        out_specs=[o_spec, o_spec, o_spec, o_spec],
        out_shape=[out_f, out_f, out_b, out_b],
        name="mem_kv",
    )(mem_prompt, w_mem_kv_b)


def _prompt_mixer_kernel(cd_ref, x_ref, cos_ref, sin_ref, dmask_ref, qdec_ref, kdec_ref, mk_ref, mv_ref,
                         win_ref, bg_ref, cw_ref, gn_ref, wco_ref, wro_ref, wmo_ref, wo_ref, lg_ref, lb_ref,
                         y_ref, ctail_ref, ret_ref, bbuf_ref, mbuf_ref):
    t = pl.program_id(1)
    tq = x_ref.shape[0]

    @pl.when(t == 0)
    def _():
        ret_ref[...] = jnp.zeros_like(ret_ref)
        ctail_ref[...] = jnp.zeros_like(ctail_ref)

    x = x_ref[...]
    xb = x.astype(BF16)

    def proj(lo, width):
        return _dot(xb, win_ref[:, lo:lo + width])

    cb = proj(OFF_CB, CONV_WIDTH)
    u = proj(OFF_CC, CONV_WIDTH) * proj(OFF_CH, CONV_WIDTH)
    row = lax.broadcasted_iota(jnp.int32, u.shape, 0)
    prev1 = ctail_ref[7:8, :]
    prev2 = ctail_ref[6:7, :]
    u1 = jnp.where(row == 0, prev1, pltpu.roll(u, 1, axis=0))
    u2 = jnp.where(row == 0, prev2, jnp.where(row == 1, prev1, pltpu.roll(u, 2, axis=0)))
    conv_y = u2 * cw_ref[0:1, :] + u1 * cw_ref[1:2, :] + u * cw_ref[2:3, :]
    ctail_ref[...] = u[tq - 8:, :]
    a_pre = (cb * conv_y).astype(BF16)

    cos = cos_ref[...]
    sin = sin_ref[...]
    pq = proj(OFF_RQ, RET_QK_WIDTH)
    pk = proj(OFF_RK, RET_QK_WIDTH)
    pv = proj(OFF_RV, RET_V_WIDTH)
    pg = proj(OFF_RG, RET_V_WIDTH)
    for h in range(RET_HEADS):
        kcols = slice(h * RET_DK, (h + 1) * RET_DK)
        vcols = slice(h * RET_DV, (h + 1) * RET_DV)
        qh = _rotary(pq[:, kcols], cos, sin)
        kh = _rotary(pk[:, kcols], cos, sin) * (RET_DK ** -0.5)
        for j in range(tq // RET_CHUNK):
            rows = slice(j * RET_CHUNK, (j + 1) * RET_CHUNK)
            qb = qh[rows].astype(BF16)
            kb = kh[rows].astype(BF16)
            kdb = (kh[rows] * kdec_ref[h]).astype(BF16)
            vb = pv[rows, vcols].astype(BF16)
            s_prev = ret_ref[h]
            scores = _dot_nt(qb, kb) * dmask_ref[h]
            o = _dot(scores.astype(BF16), vb) + _dot(qb, s_prev.astype(BF16)) * qdec_ref[h]
            ret_ref[h] = s_prev * cd_ref[h] + _dot_tn(kdb, vb)
            gated = jax.nn.silu(pg[rows, vcols]) * (_group_norm(o) * gn_ref[:, vcols])
            bbuf_ref[rows, vcols] = gated.astype(BF16)

    pm = proj(OFF_MQ, MEM_WIDTH)
    for h in range(MEM_HEADS):
        cols = slice(h * MEM_HEAD_DIM, (h + 1) * MEM_HEAD_DIM)
        s = _dot_nt(pm[:, cols].astype(BF16), mk_ref[:, cols]) * (MEM_HEAD_DIM ** -0.5)
        p = _softmax_rows(s)
        mbuf_ref[:, cols] = _dot(p.astype(BF16), mv_ref[:, cols]).astype(BF16)

    gl = [proj(OFF_GL + i * D_MODEL, D_MODEL) for i in range(N_BRANCH)]
    y_ref[...] = _merge_out_ln(x, a_pre, bbuf_ref[...], mbuf_ref[...], gl, bg_ref,
                               wco_ref, wro_ref, wmo_ref, wo_ref, lg_ref, lb_ref)


def _resident(shape, layer):
    nd = len(shape)
    return pl.BlockSpec((None,) + tuple(shape), lambda *_: (layer,) + (0,) * nd,
                        pipeline_mode=pl.Buffered(1))


def _const(shape):
    nd = len(shape)
    return pl.BlockSpec(tuple(shape), lambda *_: (0,) * nd, pipeline_mode=pl.Buffered(1))


def _prompt_mixer(layer, x, tabs, mk_b, mv_b, w):
    bp, seq, _ = x.shape
    tq = PROMPT_TILE
    grid = (bp, seq // tq)
    in_specs = [
        pl.BlockSpec(memory_space=pltpu.SMEM),
        pl.BlockSpec((None, tq, D_MODEL), lambda b, t: (b, t, 0)),
        pl.BlockSpec((tq, RET_DK), lambda b, t: (t, 0)),
        pl.BlockSpec((tq, RET_DK), lambda b, t: (t, 0)),
        _const((RET_HEADS, RET_CHUNK, RET_CHUNK)),
        _const((RET_HEADS, RET_CHUNK, RET_DV)),
        _const((RET_HEADS, RET_CHUNK, RET_DK)),
        pl.BlockSpec((None, None, MEM_LEN, MEM_WIDTH), lambda b, t: (layer, b, 0, 0)),
        pl.BlockSpec((None, None, MEM_LEN, MEM_WIDTH), lambda b, t: (layer, b, 0, 0)),
        _resident((D_MODEL, IN_WIDTH), layer),
        _resident((1, N_BRANCH * D_MODEL), layer),
        _resident((CONV_K, CONV_WIDTH), layer),
        _resident((1, RET_V_WIDTH), layer),
        _resident((CONV_WIDTH, D_MODEL), layer),
        _resident((RET_V_WIDTH, D_MODEL), layer),
        _resident((MEM_WIDTH, D_MODEL), layer),
        _resident((D_MODEL, D_MODEL), layer),
        _resident((1, D_MODEL), layer),
        _resident((1, D_MODEL), layer),
    ]
    out_specs = [
        pl.BlockSpec((None, tq, D_MODEL), lambda b, t: (b, t, 0)),
        pl.BlockSpec((None, 8, CONV_WIDTH), lambda b, t: (b, 0, 0)),
        pl.BlockSpec((None, RET_HEADS, RET_DK, RET_DV), lambda b, t: (b, 0, 0, 0)),
    ]
    out_shape = [
        jax.ShapeDtypeStruct((bp, seq, D_MODEL), F32),
        jax.ShapeDtypeStruct((bp, 8, CONV_WIDTH), F32),
        jax.ShapeDtypeStruct((bp, RET_HEADS, RET_DK, RET_DV), F32),
    ]
    return pl.pallas_call(
        _prompt_mixer_kernel,
        grid=grid,
        in_specs=in_specs,
        out_specs=out_specs,
        out_shape=out_shape,
        scratch_shapes=[pltpu.VMEM((tq, RET_V_WIDTH), BF16), pltpu.VMEM((tq, MEM_WIDTH), BF16)],
        compiler_params=pltpu.CompilerParams(
            dimension_semantics=("arbitrary", "arbitrary"), vmem_limit_bytes=VMEM_LIMIT_BYTES),
        name="prompt_mixer",
    )(tabs["cd"], x, tabs["cos"], tabs["sin"], tabs["dmask"], tabs["qdec"], tabs["kdec"], mk_b, mv_b,
      w["w_in"], w["b_gate"], w["conv_w"], w["ret_gn_g"], w["w_conv_out"], w["w_ret_out"], w["w_mem_out"],
      w["w_out"], w["ln1_g"], w["ln1_b"])


def _ffn_kernel(x_ref, wup_ref, wdn_ref, g_ref, b_ref, y_ref):
    y_ref[...] = _ffn_ln(x_ref[...], wup_ref, wdn_ref, g_ref, b_ref)


def _ffn(layer, x, w):
    n = x.shape[0]
    tm = FFN_TILE
    return pl.pallas_call(
        _ffn_kernel,
        grid=(n // tm,),
        in_specs=[
            pl.BlockSpec((tm, D_MODEL), lambda i: (i, 0)),
            _resident((D_MODEL, D_FF), layer),
            _resident((D_FF, D_MODEL), layer),
            _resident((1, D_MODEL), layer),
            _resident((1, D_MODEL), layer),
        ],
        out_specs=pl.BlockSpec((tm, D_MODEL), lambda i: (i, 0)),
        out_shape=jax.ShapeDtypeStruct((n, D_MODEL), F32),
        compiler_params=pltpu.CompilerParams(
            dimension_semantics=("arbitrary",), vmem_limit_bytes=VMEM_LIMIT_BYTES),
        name="ffn",
    )(x, w["w_up"], w["w_down"], w["ln2_g"], w["ln2_b"])


def _sample_proj_kernel(x_ref, w_ref, o_ref):
    o_ref[...] = _dot(x_ref[...].astype(BF16), w_ref[...])


def _sample_proj(layer, x, w_in_b):
    n = x.shape[0]
    tn = 1024
    return pl.pallas_call(
        _sample_proj_kernel,
        grid=(IN_WIDTH // tn,),
        in_specs=[
            pl.BlockSpec((n, D_MODEL), lambda j: (0, 0)),
            pl.BlockSpec((None, D_MODEL, tn), lambda j: (layer, 0, j)),
        ],
        out_specs=pl.BlockSpec((n, tn), lambda j: (0, j)),
        out_shape=jax.ShapeDtypeStruct((n, IN_WIDTH), F32),
        compiler_params=pltpu.CompilerParams(
            dimension_semantics=("arbitrary",), vmem_limit_bytes=VMEM_LIMIT_BYTES),
        name="sample_proj",
    )(x, w_in_b)


def _sample_state_kernel(cd_ref, proj_ref, cst_ref, cos_ref, sin_ref, dmask_ref, qdec_ref, kdec_ref,
                         sret_ref, kc_ref, vc_ref, cw_ref, gn_ref,
                         a_ref, b_ref, m_ref, u_ref, nret_ref,
                         q_scr, k_scr, v_scr, mq_scr, o_scr):
    ns = sret_ref.shape[0]
    t_len = proj_ref.shape[0] // ns

    cb = proj_ref[:, OFF_CB:OFF_CB + CONV_WIDTH]
    u = proj_ref[:, OFF_CC:OFF_CC + CONV_WIDTH] * proj_ref[:, OFF_CH:OFF_CH + CONV_WIDTH]
    tok = lax.broadcasted_iota(jnp.int32, u.shape, 0) & (t_len - 1)
    st2 = cst_ref[...]
    st1 = pltpu.roll(st2, st2.shape[0] - 1, axis=0)
    u1 = jnp.where(tok == 0, st1, pltpu.roll(u, 1, axis=0))
    u2 = jnp.where(tok < 2, st2, pltpu.roll(u, 2, axis=0))
    conv_y = u2 * cw_ref[0:1, :] + u1 * cw_ref[1:2, :] + u * cw_ref[2:3, :]
    u_ref[...] = u
    a_ref[...] = cb * conv_y

    cos = cos_ref[...]
    sin = sin_ref[...]
    for h in range(RET_HEADS):
        cols = slice(h * RET_DK, (h + 1) * RET_DK)
        q_scr[:, cols] = _rotary(proj_ref[:, OFF_RQ + h * RET_DK:OFF_RQ + (h + 1) * RET_DK], cos, sin)
        k_scr[:, cols] = _rotary(proj_ref[:, OFF_RK + h * RET_DK:OFF_RK + (h + 1) * RET_DK], cos, sin) * (RET_DK ** -0.5)
    v_scr[...] = proj_ref[:, OFF_RV:OFF_RV + RET_V_WIDTH]
    mq_scr[...] = proj_ref[:, OFF_MQ:OFF_MQ + MEM_WIDTH]

    def per_seq(s, carry):
        rows = pl.ds(pl.multiple_of(s * t_len, t_len), t_len)
        for h in range(RET_HEADS):
            kcols = slice(h * RET_DK, (h + 1) * RET_DK)
            vcols = slice(h * RET_DV, (h + 1) * RET_DV)
            qb = q_scr[rows, kcols].astype(BF16)
            kf = k_scr[rows, kcols]
            kb = kf.astype(BF16)
            kdb = (kf * kdec_ref[h]).astype(BF16)
            vb = v_scr[rows, vcols].astype(BF16)
            s_prev = sret_ref[s, h]
            scores = _dot_nt(qb, kb) * dmask_ref[h]
            o = _dot(scores.astype(BF16), vb) + _dot(qb, s_prev.astype(BF16)) * qdec_ref[h]
            nret_ref[s, h] = s_prev * cd_ref[h] + _dot_tn(kdb, vb)
            o_scr[rows, vcols] = o
        for h in range(MEM_HEADS):
            cols = slice(h * MEM_HEAD_DIM, (h + 1) * MEM_HEAD_DIM)
            sc = _dot_nt(mq_scr[rows, cols].astype(BF16), kc_ref[s, :, cols].astype(BF16)) * (MEM_HEAD_DIM ** -0.5)
            p = _softmax_rows(sc)
            m_ref[rows, cols] = _dot(p.astype(BF16), vc_ref[s, :, cols].astype(BF16))
        return carry

    lax.fori_loop(0, ns, per_seq, 0)

    for h in range(RET_HEADS):
        vcols = slice(h * RET_DV, (h + 1) * RET_DV)
        rg = proj_ref[:, OFF_RG + h * RET_DV:OFF_RG + (h + 1) * RET_DV]
        b_ref[:, vcols] = jax.nn.silu(rg) * (_group_norm(o_scr[:, vcols]) * gn_ref[:, vcols])


def _sample_state(layer, proj, cst, tabs, state_ret, kc, vc, w):
    n = proj.shape[0]
    nseq = state_ret.shape[1]
    t_len = n // nseq
    ns = SAMPLE_SEQS
    rows = ns * t_len
    grid = (nseq // ns,)
    tok_spec = lambda width: pl.BlockSpec((rows, width), lambda i: (i, 0))
    in_specs = [
        pl.BlockSpec(memory_space=pltpu.SMEM),
        tok_spec(OFF_GL),
        tok_spec(CONV_WIDTH),
        _const((rows, RET_DK)),
        _const((rows, RET_DK)),
        _const((RET_HEADS, t_len, t_len)),
        _const((RET_HEADS, t_len, RET_DV)),
        _const((RET_HEADS, t_len, RET_DK)),
        pl.BlockSpec((None, ns, RET_HEADS, RET_DK, RET_DV), lambda i: (layer, i, 0, 0, 0)),
        pl.BlockSpec((None, ns, MEM_LEN, MEM_WIDTH), lambda i: (layer, i, 0, 0)),
        pl.BlockSpec((None, ns, MEM_LEN, MEM_WIDTH), lambda i: (layer, i, 0, 0)),
        _resident((CONV_K, CONV_WIDTH), layer),
        _resident((1, RET_V_WIDTH), layer),
    ]
    out_specs = [
        tok_spec(CONV_WIDTH),
        tok_spec(RET_V_WIDTH),
        tok_spec(MEM_WIDTH),
        tok_spec(CONV_WIDTH),
        pl.BlockSpec((ns, RET_HEADS, RET_DK, RET_DV), lambda i: (i, 0, 0, 0)),
    ]
    out_shape = [
        jax.ShapeDtypeStruct((n, CONV_WIDTH), F32),
        jax.ShapeDtypeStruct((n, RET_V_WIDTH), F32),
        jax.ShapeDtypeStruct((n, MEM_WIDTH), F32),
        jax.ShapeDtypeStruct((n, CONV_WIDTH), F32),
        jax.ShapeDtypeStruct((nseq, RET_HEADS, RET_DK, RET_DV), F32),
    ]
    scratch = [
        pltpu.VMEM((rows, RET_QK_WIDTH), F32),
        pltpu.VMEM((rows, RET_QK_WIDTH), F32),
        pltpu.VMEM((rows, RET_V_WIDTH), F32),
        pltpu.VMEM((rows, MEM_WIDTH), F32),
        pltpu.VMEM((rows, RET_V_WIDTH), F32),
    ]
    return pl.pallas_call(
        _sample_state_kernel,
        grid=grid,
        in_specs=in_specs,
        out_specs=out_specs,
        out_shape=out_shape,
        scratch_shapes=scratch,
        compiler_params=pltpu.CompilerParams(
            dimension_semantics=("arbitrary",), vmem_limit_bytes=VMEM_LIMIT_BYTES),
        name="sample_state",
    )(tabs["cd"], proj, cst, tabs["cos"], tabs["sin"], tabs["dmask"], tabs["qdec"], tabs["kdec"],
      state_ret, kc, vc, w["conv_w"], w["ret_gn_g"])


def _sample_post_kernel(x_ref, a_ref, b_ref, m_ref, g0_ref, g1_ref, g2_ref,
                        bg_ref, wco_ref, wro_ref, wmo_ref, wo_ref, l1g_ref, l1b_ref,
                        wup_ref, wdn_ref, l2g_ref, l2b_ref, y_ref):
    gl = [g0_ref[...], g1_ref[...], g2_ref[...]]
    x1 = _merge_out_ln(x_ref[...], a_ref[...].astype(BF16), b_ref[...].astype(BF16), m_ref[...].astype(BF16),
                       gl, bg_ref, wco_ref, wro_ref, wmo_ref, wo_ref, l1g_ref, l1b_ref)
    y_ref[...] = _ffn_ln(x1, wup_ref, wdn_ref, l2g_ref, l2b_ref)


def _sample_post(layer, x, a_pre, b_pre, m_pre, proj, w):
    n = x.shape[0]
    tm = SAMPLE_POST_TILE
    tok = lambda width: pl.BlockSpec((tm, width), lambda i: (i, 0))
    gate = lambda k: pl.BlockSpec((tm, D_MODEL), lambda i: (i, OFF_GL // D_MODEL + k))
    return pl.pallas_call(
        _sample_post_kernel,
        grid=(n // tm,),
        in_specs=[
            tok(D_MODEL), tok(CONV_WIDTH), tok(RET_V_WIDTH), tok(MEM_WIDTH), gate(0), gate(1), gate(2),
            _resident((1, N_BRANCH * D_MODEL), layer),
            _resident((CONV_WIDTH, D_MODEL), layer),
            _resident((RET_V_WIDTH, D_MODEL), layer),
            _resident((MEM_WIDTH, D_MODEL), layer),
            _resident((D_MODEL, D_MODEL), layer),
            _resident((1, D_MODEL), layer),
            _resident((1, D_MODEL), layer),
            _resident((D_MODEL, D_FF), layer),
            _resident((D_FF, D_MODEL), layer),
            _resident((1, D_MODEL), layer),
            _resident((1, D_MODEL), layer),
        ],
        out_specs=tok(D_MODEL),
        out_shape=jax.ShapeDtypeStruct((n, D_MODEL), F32),
        compiler_params=pltpu.CompilerParams(
            dimension_semantics=("arbitrary",), vmem_limit_bytes=VMEM_LIMIT_BYTES),
        name="sample_post",
    )(x, a_pre, b_pre, m_pre, proj, proj, proj,
      w["b_gate"], w["w_conv_out"], w["w_ret_out"], w["w_mem_out"], w["w_out"], w["ln1_g"], w["ln1_b"],
      w["w_up"], w["w_down"], w["ln2_g"], w["ln2_b"])


def _rotary_tables(pos, reps):
    half = RET_DK // 2
    inv = ROPE_BASE ** (-jnp.arange(half, dtype=F32) / half)
    ang = pos.astype(F32)[:, None] * inv[None, :]
    cos = jnp.cos(ang)
    sin = jnp.sin(ang)
    cos_full = jnp.concatenate([cos, cos], axis=-1)
    sin_signed = jnp.concatenate([-sin, sin], axis=-1)
    return jnp.tile(cos_full, (reps, 1)), jnp.tile(sin_signed, (reps, 1))


def _decay_tables(c):
    h = RET_HEADS
    log_g = jnp.log1p(-jnp.exp2(-5.0 - jnp.arange(h, dtype=F32)))
    idx = jnp.arange(c, dtype=F32)
    rel = idx[:, None] - idx[None, :]
    causal = rel >= 0
    dmask = jnp.where(causal[None], jnp.exp(jnp.where(causal, rel, 0.0)[None] * log_g[:, None, None]), 0.0)
    q_decay = jnp.exp((idx + 1.0)[None, :] * log_g[:, None])
    k_decay = jnp.exp((c - 1.0 - idx)[None, :] * log_g[:, None])
    chunk_decay = jnp.exp(c * log_g)
    return {
        "dmask": dmask,
        "qdec": jnp.broadcast_to(q_decay[:, :, None], (h, c, RET_DV)),
        "kdec": jnp.broadcast_to(k_decay[:, :, None], (h, c, RET_DK)),
        "cd": chunk_decay,
    }


def kernel(x_prompt, x_sample, cache_mem_k, cache_mem_v, state_conv, state_ret, mem_prompt,
           w_in, b_gate, conv_w, ret_gn_g, w_conv_out, w_ret_out, w_mem_out, w_out, w_mem_kv,
           ln1_g, ln1_b, w_up, w_down, ln2_g, ln2_b):
    bp, seq, _ = x_prompt.shape
    nseq, t_len, _ = x_sample.shape
    assert seq % PROMPT_TILE == 0 and PROMPT_TILE % RET_CHUNK == 0
    assert t_len == 8 and nseq % SAMPLE_SEQS == 0 and t_len % RET_CHUNK != 0

    row = lambda a: a.reshape(DEPTH, 1, a.shape[-1])
    w = {
        "w_in": w_in.astype(BF16), "b_gate": row(b_gate), "conv_w": conv_w, "ret_gn_g": row(ret_gn_g),
        "w_conv_out": w_conv_out.astype(BF16), "w_ret_out": w_ret_out.astype(BF16),
        "w_mem_out": w_mem_out.astype(BF16), "w_out": w_out.astype(BF16),
        "ln1_g": row(ln1_g), "ln1_b": row(ln1_b),
        "w_up": w_up.astype(BF16), "w_down": w_down.astype(BF16),
        "ln2_g": row(ln2_g), "ln2_b": row(ln2_b),
    }

    tabs_p = _decay_tables(RET_CHUNK)
    tabs_p["cos"], tabs_p["sin"] = _rotary_tables(jnp.arange(seq, dtype=jnp.int32), 1)
    tabs_s = _decay_tables(t_len)
    tabs_s["cos"], tabs_s["sin"] = _rotary_tables(PAST_LEN + jnp.arange(t_len, dtype=jnp.int32), SAMPLE_SEQS)

    mk_f, mv_f, mk_b, mv_b = _mem_kv(mem_prompt, w_mem_kv.astype(BF16))

    kc = cache_mem_k.reshape(DEPTH, nseq, MEM_LEN, MEM_WIDTH)
    vc = cache_mem_v.reshape(DEPTH, nseq, MEM_LEN, MEM_WIDTH)
    cst = jnp.concatenate(
        [state_conv, jnp.zeros((DEPTH, nseq, t_len - (CONV_K - 1), CONV_WIDTH), state_conv.dtype)], axis=2
    ).reshape(DEPTH, nseq * t_len, CONV_WIDTH)

    xp = x_prompt
    xs = x_sample.reshape(nseq * t_len, D_MODEL)
    cp_list, rp_list, cs_list, rs_list = [], [], [], []
    for l in range(DEPTH):
        xp1, ctail_p, ret_p = _prompt_mixer(l, xp, tabs_p, mk_b, mv_b, w)
        xp = _ffn(l, xp1.reshape(bp * seq, D_MODEL), w).reshape(bp, seq, D_MODEL)
        cp_list.append(ctail_p[:, 8 - (CONV_K - 1):, :])
        rp_list.append(ret_p)

        proj = _sample_proj(l, xs, w["w_in"])
        a_pre, b_pre, m_pre, u_s, ret_s = _sample_state(l, proj, cst[l], tabs_s, state_ret, kc, vc, w)
        xs = _sample_post(l, xs, a_pre, b_pre, m_pre, proj, w)
        cs_list.append(u_s.reshape(nseq, t_len, CONV_WIDTH)[:, t_len - (CONV_K - 1):, :])
        rs_list.append(ret_s)

    mem_shape = (DEPTH, bp, MEM_LEN, MEM_HEADS, MEM_HEAD_DIM)
    return (xp, xs.reshape(nseq, t_len, D_MODEL), mk_f.reshape(mem_shape), mv_f.reshape(mem_shape),
            jnp.stack(cp_list), jnp.stack(rp_list), jnp.stack(cs_list), jnp.stack(rs_list))
```

```python
import functools

import jax
import jax.numpy as jnp
from jax import lax
from jax.experimental import pallas as pl
from jax.experimental.pallas import tpu as pltpu

D_MODEL = 1024
DEPTH = 4
CONV_WIDTH = 512
CONV_K = 3
RET_HEADS = 4
RET_DK = 128
RET_DV = 256
RET_QK_WIDTH = RET_HEADS * RET_DK
RET_V_WIDTH = RET_HEADS * RET_DV
RET_CHUNK = 128
MEM_LEN = 256
MEM_HEADS = 4
MEM_HEAD_DIM = 128
MEM_WIDTH = MEM_HEADS * MEM_HEAD_DIM
N_BRANCH = 3
D_FF = 4 * D_MODEL
ROPE_BASE = 10000.0
LN_EPS = 1e-5
GN_EPS = 1e-6
PAST_LEN = 16384
DEEPNORM_ALPHA = (2 * DEPTH) ** 0.25

OFF_CB = 0
OFF_CC = OFF_CB + CONV_WIDTH
OFF_CH = OFF_CC + CONV_WIDTH
OFF_RQ = OFF_CH + CONV_WIDTH
OFF_RK = OFF_RQ + RET_QK_WIDTH
OFF_RV = OFF_RK + RET_QK_WIDTH
OFF_RG = OFF_RV + RET_V_WIDTH
OFF_MQ = OFF_RG + RET_V_WIDTH
OFF_GL = OFF_MQ + MEM_WIDTH
IN_WIDTH = OFF_GL + N_BRANCH * D_MODEL

V7X_VMEM_BYTES = 64 * 1024 * 1024
VMEM_LIMIT_BYTES = V7X_VMEM_BYTES - 6 * 1024 * 1024

PROMPT_TILE = 512
FFN_TILE = 512
FFN_CHUNK = 1024
SAMPLE_SEQS = 8
SAMPLE_POST_TILE = 256

BF16 = jnp.bfloat16
F32 = jnp.float32


def _dot(a, b):
    return jnp.dot(a, b, preferred_element_type=F32)


def _dot_nt(a, b):
    return lax.dot_general(a, b, (((1,), (1,)), ((), ())), preferred_element_type=F32)


def _dot_tn(a, b):
    return lax.dot_general(a, b, (((0,), (0,)), ((), ())), preferred_element_type=F32)


def _layer_norm(z, g, b):
    mu = jnp.mean(z, axis=-1, keepdims=True)
    zc = z - mu
    var = jnp.mean(zc * zc, axis=-1, keepdims=True)
    return zc * lax.rsqrt(var + LN_EPS) * g + b


def _group_norm(o):
    mu = jnp.mean(o, axis=-1, keepdims=True)
    oc = o - mu
    var = jnp.mean(oc * oc, axis=-1, keepdims=True)
    return oc * lax.rsqrt(var + GN_EPS)


def _rotary(xh, cos, sin_signed):
    return xh * cos + pltpu.roll(xh, RET_DK // 2, axis=1) * sin_signed


def _softmax_rows(s):
    m = jnp.max(s, axis=-1, keepdims=True)
    e = jnp.exp(s - m)
    return e / jnp.sum(e, axis=-1, keepdims=True)


def _merge_out_ln(x, a_pre, b_pre, m_pre, gl, bg_ref, wco_ref, wro_ref, wmo_ref, wo_ref, g_ref, b_ref):
    a_out = _dot(a_pre, wco_ref[...])
    b_out = _dot(b_pre, wro_ref[...])
    m_out = _dot(m_pre, wmo_ref[...])
    g0 = jax.nn.sigmoid(gl[0] + bg_ref[:, 0:D_MODEL])
    g1 = jax.nn.sigmoid(gl[1] + bg_ref[:, D_MODEL:2 * D_MODEL])
    g2 = jax.nn.sigmoid(gl[2] + bg_ref[:, 2 * D_MODEL:3 * D_MODEL])
    merged = g0 * a_out + g1 * b_out + g2 * m_out
    z = DEEPNORM_ALPHA * x + _dot(merged.astype(BF16), wo_ref[...])
    return _layer_norm(z, g_ref[...], b_ref[...])


def _ffn_ln(x, wup_ref, wdn_ref, g_ref, b_ref):
    xb = x.astype(BF16)
    acc = None
    for c in range(D_FF // FFN_CHUNK):
        cols = slice(c * FFN_CHUNK, (c + 1) * FFN_CHUNK)
        h = jnp.maximum(_dot(xb, wup_ref[:, cols]), 0.0)
        part = _dot((h * h).astype(BF16), wdn_ref[cols, :])
        acc = part if acc is None else acc + part
    return _layer_norm(DEEPNORM_ALPHA * x + acc, g_ref[...], b_ref[...])


def _mem_kv_kernel(mem_ref, w_ref, k_ref, v_ref, kb_ref, vb_ref):
    kv = _dot(mem_ref[...].astype(BF16), w_ref[...])
    k = kv[:, :MEM_WIDTH]
    v = kv[:, MEM_WIDTH:]
    k_ref[...] = k
    v_ref[...] = v
    kb_ref[...] = k.astype(BF16)
    vb_ref[...] = v.astype(BF16)


def _mem_kv(mem_prompt, w_mem_kv_b):
    bp = mem_prompt.shape[0]
    out_f = jax.ShapeDtypeStruct((DEPTH, bp, MEM_LEN, MEM_WIDTH), F32)
    out_b = jax.ShapeDtypeStruct((DEPTH, bp, MEM_LEN, MEM_WIDTH), BF16)
    o_spec = pl.BlockSpec((None, None, MEM_LEN, MEM_WIDTH), lambda l, b: (l, b, 0, 0))
    return pl.pallas_call(
        _mem_kv_kernel,
        grid=(DEPTH, bp),
        in_specs=[
            pl.BlockSpec((None, MEM_LEN, D_MODEL), lambda l, b: (b, 0, 0)),
            pl.BlockSpec((None, D_MODEL, 2 * MEM_WIDTH), lambda l, b: (l, 0, 0)),
        ],
        out_specs=[o_spec, o_spec, o_spec, o_spec],
        out_shape=[out_f, out_f, out_b, out_b],
        name="mem_kv",
    )(mem_prompt, w_mem_kv_b)


def _prompt_mixer_kernel(cd_ref, x_ref, cos_ref, sin_ref, dmask_ref, qdec_ref, kdec_ref, mk_ref, mv_ref,
                         win_ref, bg_ref, cw_ref, gn_ref, wco_ref, wro_ref, wmo_ref, wo_ref, lg_ref, lb_ref,
                         y_ref, ctail_ref, ret_ref, bbuf_ref, mbuf_ref):
    t = pl.program_id(1)
    tq = x_ref.shape[0]
    chunks = [slice(j * RET_CHUNK, (j + 1) * RET_CHUNK) for j in range(tq // RET_CHUNK)]
    kcols = [slice(h * RET_DK, (h + 1) * RET_DK) for h in range(RET_HEADS)]
    vcols = [slice(h * RET_DV, (h + 1) * RET_DV) for h in range(RET_HEADS)]
    mcols = [slice(h * MEM_HEAD_DIM, (h + 1) * MEM_HEAD_DIM) for h in range(MEM_HEADS)]

    @pl.when(t == 0)
    def _():
        ret_ref[...] = jnp.zeros_like(ret_ref)
        ctail_ref[...] = jnp.zeros_like(ctail_ref)

    x = x_ref[...]
    xb = x.astype(BF16)

    def proj(lo, width):
        return _dot(xb, win_ref[:, lo:lo + width])

    pq = proj(OFF_RQ, RET_QK_WIDTH)
    pk = proj(OFF_RK, RET_QK_WIDTH)
    vb = proj(OFF_RV, RET_V_WIDTH).astype(BF16)
    cos = cos_ref[...]
    sin = sin_ref[...]
    qb, kb, kdb = [], [], []
    for h in range(RET_HEADS):
        qb.append(_rotary(pq[:, kcols[h]], cos, sin).astype(BF16))
        kh = _rotary(pk[:, kcols[h]], cos, sin) * (RET_DK ** -0.5)
        kb.append(kh.astype(BF16))
        kdb.append([(kh[r] * kdec_ref[h]).astype(BF16) for r in chunks])

    pm = proj(OFF_MQ, MEM_WIDTH).astype(BF16)
    mem_s = [_dot_nt(pm[:, mcols[h]], mk_ref[:, mcols[h]]) * (MEM_HEAD_DIM ** -0.5) for h in range(MEM_HEADS)]

    scores = [[_dot_nt(qb[h][r], kb[h][r]) for r in chunks] for h in range(RET_HEADS)]

    cb = proj(OFF_CB, CONV_WIDTH)
    u = proj(OFF_CC, CONV_WIDTH) * proj(OFF_CH, CONV_WIDTH)
    row = lax.broadcasted_iota(jnp.int32, u.shape, 0)
    prev1 = ctail_ref[7:8, :]
    prev2 = ctail_ref[6:7, :]
    u1 = jnp.where(row == 0, prev1, pltpu.roll(u, 1, axis=0))
    u2 = jnp.where(row == 0, prev2, jnp.where(row == 1, prev1, pltpu.roll(u, 2, axis=0)))
    conv_y = u2 * cw_ref[0:1, :] + u1 * cw_ref[1:2, :] + u * cw_ref[2:3, :]
    ctail_ref[...] = u[tq - 8:, :]
    a_pre = (cb * conv_y).astype(BF16)

    kv = [[_dot_tn(kdb[h][j], vb[r, vcols[h]]) for j, r in enumerate(chunks)] for h in range(RET_HEADS)]

    for h in range(MEM_HEADS):
        p = _softmax_rows(mem_s[h])
        mbuf_ref[:, mcols[h]] = _dot(p.astype(BF16), mv_ref[:, mcols[h]]).astype(BF16)

    inter = []
    for h in range(RET_HEADS):
        s_cur = ret_ref[h]
        inter_h = []
        for j, r in enumerate(chunks):
            inter_h.append(_dot(qb[h][r], s_cur.astype(BF16)))
            s_cur = s_cur * cd_ref[h] + kv[h][j]
        ret_ref[h] = s_cur
        inter.append(inter_h)

    pg = proj(OFF_RG, RET_V_WIDTH)

    for h in range(RET_HEADS):
        for j, r in enumerate(chunks):
            masked = (scores[h][j] * dmask_ref[h]).astype(BF16)
            o = _dot(masked, vb[r, vcols[h]]) + inter[h][j] * qdec_ref[h]
            gated = jax.nn.silu(pg[r, vcols[h]]) * (_group_norm(o) * gn_ref[:, vcols[h]])
            bbuf_ref[r, vcols[h]] = gated.astype(BF16)

    gl =[proj(OFF_GL + i * D_MODEL, D_MODEL) for i in range(N_BRANCH)]
    y_ref[...] = _merge_out_ln(x, a_pre, bbuf_ref[...], mbuf_ref[...], gl, bg_ref,
                               wco_ref, wro_ref, wmo_ref, wo_ref, lg_ref, lb_ref)


def _resident(shape, layer):
    nd = len(shape)
    return pl.BlockSpec((None,) + tuple(shape), lambda *_: (layer,) + (0,) * nd,
                        pipeline_mode=pl.Buffered(1))


def _const(shape):
    nd = len(shape)
    return pl.BlockSpec(tuple(shape), lambda *_: (0,) * nd, pipeline_mode=pl.Buffered(1))


def _prompt_mixer(layer, x, tabs, mk_b, mv_b, w):
    bp, seq, _ = x.shape
    tq = PROMPT_TILE
    grid = (bp, seq // tq)
    in_specs = [
        pl.BlockSpec(memory_space=pltpu.SMEM),
        pl.BlockSpec((None, tq, D_MODEL), lambda b, t: (b, t, 0)),
        pl.BlockSpec((tq, RET_DK), lambda b, t: (t, 0)),
        pl.BlockSpec((tq, RET_DK), lambda b, t: (t, 0)),
        _const((RET_HEADS, RET_CHUNK, RET_CHUNK)),
        _const((RET_HEADS, RET_CHUNK, RET_DV)),
        _const((RET_HEADS, RET_CHUNK, RET_DK)),
        pl.BlockSpec((None, None, MEM_LEN, MEM_WIDTH), lambda b, t: (layer, b, 0, 0)),
        pl.BlockSpec((None, None, MEM_LEN, MEM_WIDTH), lambda b, t: (layer, b, 0, 0)),
        _resident((D_MODEL, IN_WIDTH), layer),
        _resident((1, N_BRANCH * D_MODEL), layer),
        _resident((CONV_K, CONV_WIDTH), layer),
        _resident((1, RET_V_WIDTH), layer),
        _resident((CONV_WIDTH, D_MODEL), layer),
        _resident((RET_V_WIDTH, D_MODEL), layer),
        _resident((MEM_WIDTH, D_MODEL), layer),
        _resident((D_MODEL, D_MODEL), layer),
        _resident((1, D_MODEL), layer),
        _resident((1, D_MODEL), layer),
    ]
    out_specs = [
        pl.BlockSpec((None, tq, D_MODEL), lambda b, t: (b, t, 0)),
        pl.BlockSpec((None, 8, CONV_WIDTH), lambda b, t: (b, 0, 0)),
        pl.BlockSpec((None, RET_HEADS, RET_DK, RET_DV), lambda b, t: (b, 0, 0, 0)),
    ]
    out_shape = [
        jax.ShapeDtypeStruct((bp, seq, D_MODEL), F32),
        jax.ShapeDtypeStruct((bp, 8, CONV_WIDTH), F32),
        jax.ShapeDtypeStruct((bp, RET_HEADS, RET_DK, RET_DV), F32),
    ]
    return pl.pallas_call(
        _prompt_mixer_kernel,
        grid=grid,
        in_specs=in_specs,
        out_specs=out_specs,
        out_shape=out_shape,
        scratch_shapes=[pltpu.VMEM((tq, RET_V_WIDTH), BF16), pltpu.VMEM((tq, MEM_WIDTH), BF16)],
        compiler_params=pltpu.CompilerParams(
            dimension_semantics=("arbitrary", "arbitrary"), vmem_limit_bytes=VMEM_LIMIT_BYTES),
        name="prompt_mixer",
    )(tabs["cd"], x, tabs["cos"], tabs["sin"], tabs["dmask"], tabs["qdec"], tabs["kdec"], mk_b, mv_b,
      w["w_in"], w["b_gate"], w["conv_w"], w["ret_gn_g"], w["w_conv_out"], w["w_ret_out"], w["w_mem_out"],
      w["w_out"], w["ln1_g"], w["ln1_b"])


def _ffn_kernel(x_ref, wup_ref, wdn_ref, g_ref, b_ref, y_ref):
    y_ref[...] = _ffn_ln(x_ref[...], wup_ref, wdn_ref, g_ref, b_ref)


def _ffn(layer, x, w):
    n = x.shape[0]
    tm = FFN_TILE
    return pl.pallas_call(
        _ffn_kernel,
        grid=(n // tm,),
        in_specs=[
            pl.BlockSpec((tm, D_MODEL), lambda i: (i, 0)),
            _resident((D_MODEL, D_FF), layer),
            _resident((D_FF, D_MODEL), layer),
            _resident((1, D_MODEL), layer),
            _resident((1, D_MODEL), layer),
        ],
        out_specs=pl.BlockSpec((tm, D_MODEL), lambda i: (i, 0)),
        out_shape=jax.ShapeDtypeStruct((n, D_MODEL), F32),
        compiler_params=pltpu.CompilerParams(
            dimension_semantics=("arbitrary",), vmem_limit_bytes=VMEM_LIMIT_BYTES),
        name="ffn",
    )(x, w["w_up"], w["w_down"], w["ln2_g"], w["ln2_b"])


def _sample_proj_kernel(x_ref, w_ref, o_ref):
    o_ref[...] = _dot(x_ref[...].astype(BF16), w_ref[...])


def _sample_proj(layer, x, w_in_b):
    n = x.shape[0]
    tn = 1024
    return pl.pallas_call(
        _sample_proj_kernel,
        grid=(IN_WIDTH // tn,),
        in_specs=[
            pl.BlockSpec((n, D_MODEL), lambda j: (0, 0)),
            pl.BlockSpec((None, D_MODEL, tn), lambda j: (layer, 0, j)),
        ],
        out_specs=pl.BlockSpec((n, tn), lambda j: (0, j)),
        out_shape=jax.ShapeDtypeStruct((n, IN_WIDTH), F32),
        compiler_params=pltpu.CompilerParams(
            dimension_semantics=("arbitrary",), vmem_limit_bytes=VMEM_LIMIT_BYTES),
        name="sample_proj",
    )(x, w_in_b)


def _sample_state_kernel(cd_ref, proj_ref, cst_ref, cos_ref, sin_ref, dmask_ref, qdec_ref, kdec_ref,
                         sret_ref, kc_ref, vc_ref, cw_ref, gn_ref,
                         a_ref, b_ref, m_ref, u_ref, nret_ref,
                         q_scr, k_scr, v_scr, mq_scr, o_scr):
    ns = sret_ref.shape[0]
    t_len = proj_ref.shape[0] // ns

    cb = proj_ref[:, OFF_CB:OFF_CB + CONV_WIDTH]
    u = proj_ref[:, OFF_CC:OFF_CC + CONV_WIDTH] * proj_ref[:, OFF_CH:OFF_CH + CONV_WIDTH]
    tok = lax.broadcasted_iota(jnp.int32, u.shape, 0) & (t_len - 1)
    st2 = cst_ref[...]
    st1 = pltpu.roll(st2, st2.shape[0] - 1, axis=0)
    u1 = jnp.where(tok == 0, st1, pltpu.roll(u, 1, axis=0))
    u2 = jnp.where(tok < 2, st2, pltpu.roll(u, 2, axis=0))
    conv_y = u2 * cw_ref[0:1, :] + u1 * cw_ref[1:2, :] + u * cw_ref[2:3, :]
    u_ref[...] = u
    a_ref[...] = cb * conv_y

    cos = cos_ref[...]
    sin = sin_ref[...]
    for h in range(RET_HEADS):
        cols = slice(h * RET_DK, (h + 1) * RET_DK)
        q_scr[:, cols] = _rotary(proj_ref[:, OFF_RQ + h * RET_DK:OFF_RQ + (h + 1) * RET_DK], cos, sin)
        k_scr[:, cols] = _rotary(proj_ref[:, OFF_RK + h * RET_DK:OFF_RK + (h + 1) * RET_DK], cos, sin) * (RET_DK ** -0.5)
    v_scr[...] = proj_ref[:, OFF_RV:OFF_RV + RET_V_WIDTH]
    mq_scr[...] = proj_ref[:, OFF_MQ:OFF_MQ + MEM_WIDTH]

    def per_seq(s, carry):
        rows = pl.ds(pl.multiple_of(s * t_len, t_len), t_len)
        for h in range(RET_HEADS):
            kcols = slice(h * RET_DK, (h + 1) * RET_DK)
            vcols = slice(h * RET_DV, (h + 1) * RET_DV)
            qb = q_scr[rows, kcols].astype(BF16)
            kf = k_scr[rows, kcols]
            kb = kf.astype(BF16)
            kdb = (kf * kdec_ref[h]).astype(BF16)
            vb = v_scr[rows, vcols].astype(BF16)
            s_prev = sret_ref[s, h]
            scores = _dot_nt(qb, kb) * dmask_ref[h]
            o = _dot(scores.astype(BF16), vb) + _dot(qb, s_prev.astype(BF16)) * qdec_ref[h]
            nret_ref[s, h] = s_prev * cd_ref[h] + _dot_tn(kdb, vb)
            o_scr[rows, vcols] = o
        for h in range(MEM_HEADS):
            cols = slice(h * MEM_HEAD_DIM, (h + 1) * MEM_HEAD_DIM)
            head_rows = pl.ds(h, MEM_LEN, stride=MEM_HEADS)
            sc = _dot_nt(mq_scr[rows, cols].astype(BF16), kc_ref[s, head_rows, :].astype(BF16)) * (MEM_HEAD_DIM ** -0.5)
            p = _softmax_rows(sc)
            m_ref[rows, cols] = _dot(p.astype(BF16), vc_ref[s, head_rows, :].astype(BF16))
        return carry

    lax.fori_loop(0, ns, per_seq, 0)

    for h in range(RET_HEADS):
        vcols = slice(h * RET_DV, (h + 1) * RET_DV)
        rg = proj_ref[:, OFF_RG + h * RET_DV:OFF_RG + (h + 1) * RET_DV]
        b_ref[:, vcols] = jax.nn.silu(rg) * (_group_norm(o_scr[:, vcols]) * gn_ref[:, vcols])


def _sample_state_kernel_into(stack_ref, *refs):
    del stack_ref
    _sample_state_kernel(*refs)


def _sample_state(layer, proj, cst, tabs, state_ret, kc, vc, w, ret_stack):
    n = proj.shape[0]
    nseq = state_ret.shape[1]
    t_len = n // nseq
    ns = SAMPLE_SEQS
    rows = ns * t_len
    grid = (nseq // ns,)
    tok_spec = lambda width: pl.BlockSpec((rows, width), lambda i: (i, 0))
    in_specs = [
        pl.BlockSpec(memory_space=pltpu.SMEM),
        tok_spec(OFF_GL),
        tok_spec(CONV_WIDTH),
        _const((rows, RET_DK)),
        _const((rows, RET_DK)),
        _const((RET_HEADS, t_len, t_len)),
        _const((RET_HEADS, t_len, RET_DV)),
        _const((RET_HEADS, t_len, RET_DK)),
        pl.BlockSpec((None, ns, RET_HEADS, RET_DK, RET_DV), lambda i: (layer, i, 0, 0, 0)),
        pl.BlockSpec((None, ns, MEM_LEN * MEM_HEADS, MEM_HEAD_DIM), lambda i: (layer, i, 0, 0)),
        pl.BlockSpec((None, ns, MEM_LEN * MEM_HEADS, MEM_HEAD_DIM), lambda i: (layer, i, 0, 0)),
        _resident((CONV_K, CONV_WIDTH), layer),
        _resident((1, RET_V_WIDTH), layer),
    ]
    out_specs = [
        tok_spec(CONV_WIDTH),
        tok_spec(RET_V_WIDTH),
        tok_spec(MEM_WIDTH),
        tok_spec(CONV_WIDTH),
        pl.BlockSpec((None, ns, RET_HEADS, RET_DK, RET_DV), lambda i: (layer, i, 0, 0, 0)),
    ]
    out_shape = [
        jax.ShapeDtypeStruct((n, CONV_WIDTH), F32),
        jax.ShapeDtypeStruct((n, RET_V_WIDTH), F32),
        jax.ShapeDtypeStruct((n, MEM_WIDTH), F32),
        jax.ShapeDtypeStruct((n, CONV_WIDTH), F32),
        jax.ShapeDtypeStruct((DEPTH, nseq, RET_HEADS, RET_DK, RET_DV), F32),
    ]
    args = (tabs["cd"], proj, cst, tabs["cos"], tabs["sin"], tabs["dmask"], tabs["qdec"], tabs["kdec"],
            state_ret, kc, vc, w["conv_w"], w["ret_gn_g"])
    body, aliases = _sample_state_kernel, {}
    if ret_stack is not None:
        body, aliases = _sample_state_kernel_into, {0: len(out_shape) - 1}
        in_specs = [pl.BlockSpec(memory_space=pl.ANY)] + in_specs
        args = (ret_stack,) + args
    scratch = [
        pltpu.VMEM((rows, RET_QK_WIDTH), F32),
        pltpu.VMEM((rows, RET_QK_WIDTH), F32),
        pltpu.VMEM((rows, RET_V_WIDTH), F32),
        pltpu.VMEM((rows, MEM_WIDTH), F32),
        pltpu.VMEM((rows, RET_V_WIDTH), F32),
    ]
    return pl.pallas_call(
        body,
        grid=grid,
        in_specs=in_specs,
        out_specs=out_specs,
        out_shape=out_shape,
        scratch_shapes=scratch,
        input_output_aliases=aliases,
        compiler_params=pltpu.CompilerParams(
            dimension_semantics=("arbitrary",), vmem_limit_bytes=VMEM_LIMIT_BYTES),
        name="sample_state",
    )(*args)


def _sample_post_kernel(x_ref, a_ref, b_ref, m_ref, g0_ref, g1_ref, g2_ref,
                        bg_ref, wco_ref, wro_ref, wmo_ref, wo_ref, l1g_ref, l1b_ref,
                        wup_ref, wdn_ref, l2g_ref, l2b_ref, y_ref):
    gl = [g0_ref[...], g1_ref[...], g2_ref[...]]
    x1 = _merge_out_ln(x_ref[...], a_ref[...].astype(BF16), b_ref[...].astype(BF16), m_ref[...].astype(BF16),
                       gl, bg_ref, wco_ref, wro_ref, wmo_ref, wo_ref, l1g_ref, l1b_ref)
    y_ref[...] = _ffn_ln(x1, wup_ref, wdn_ref, l2g_ref, l2b_ref)


def _sample_post(layer, x, a_pre, b_pre, m_pre, proj, w):
    n = x.shape[0]
    tm = SAMPLE_POST_TILE
    tok = lambda width: pl.BlockSpec((tm, width), lambda i: (i, 0))
    gate = lambda k: pl.BlockSpec((tm, D_MODEL), lambda i: (i, OFF_GL // D_MODEL + k))
    return pl.pallas_call(
        _sample_post_kernel,
        grid=(n // tm,),
        in_specs=[
            tok(D_MODEL), tok(CONV_WIDTH), tok(RET_V_WIDTH), tok(MEM_WIDTH), gate(0), gate(1), gate(2),
            _resident((1, N_BRANCH * D_MODEL), layer),
            _resident((CONV_WIDTH, D_MODEL), layer),
            _resident((RET_V_WIDTH, D_MODEL), layer),
            _resident((MEM_WIDTH, D_MODEL), layer),
            _resident((D_MODEL, D_MODEL), layer),
            _resident((1, D_MODEL), layer),
            _resident((1, D_MODEL), layer),
            _resident((D_MODEL, D_FF), layer),
            _resident((D_FF, D_MODEL), layer),
            _resident((1, D_MODEL), layer),
            _resident((1, D_MODEL), layer),
        ],
        out_specs=tok(D_MODEL),
        out_shape=jax.ShapeDtypeStruct((n, D_MODEL), F32),
        compiler_params=pltpu.CompilerParams(
            dimension_semantics=("arbitrary",), vmem_limit_bytes=VMEM_LIMIT_BYTES),
        name="sample_post",
    )(x, a_pre, b_pre, m_pre, proj, proj, proj,
      w["b_gate"], w["w_conv_out"], w["w_ret_out"], w["w_mem_out"], w["w_out"], w["ln1_g"], w["ln1_b"],
      w["w_up"], w["w_down"], w["ln2_g"], w["ln2_b"])


def _rotary_tables(pos, reps):
    half = RET_DK // 2
    inv = ROPE_BASE ** (-jnp.arange(half, dtype=F32) / half)
    ang = pos.astype(F32)[:, None] * inv[None, :]
    cos = jnp.cos(ang)
    sin = jnp.sin(ang)
    cos_full = jnp.concatenate([cos, cos], axis=-1)
    sin_signed = jnp.concatenate([-sin, sin], axis=-1)
    return jnp.tile(cos_full, (reps, 1)), jnp.tile(sin_signed, (reps, 1))


def _decay_tables(c):
    h = RET_HEADS
    log_g = jnp.log1p(-jnp.exp2(-5.0 - jnp.arange(h, dtype=F32)))
    idx = jnp.arange(c, dtype=F32)
    rel = idx[:, None] - idx[None, :]
    causal = rel >= 0
    dmask = jnp.where(causal[None], jnp.exp(jnp.where(causal, rel, 0.0)[None] * log_g[:, None, None]), 0.0)
    q_decay = jnp.exp((idx + 1.0)[None, :] * log_g[:, None])
    k_decay = jnp.exp((c - 1.0 - idx)[None, :] * log_g[:, None])
    chunk_decay = jnp.exp(c * log_g)
    return {
        "dmask": dmask,
        "qdec": jnp.broadcast_to(q_decay[:, :, None], (h, c, RET_DV)),
        "kdec": jnp.broadcast_to(k_decay[:, :, None], (h, c, RET_DK)),
        "cd": chunk_decay,
    }


def kernel(x_prompt, x_sample, cache_mem_k, cache_mem_v, state_conv, state_ret, mem_prompt,
           w_in, b_gate, conv_w, ret_gn_g, w_conv_out, w_ret_out, w_mem_out, w_out, w_mem_kv,
           ln1_g, ln1_b, w_up, w_down, ln2_g, ln2_b):
    bp, seq, _ = x_prompt.shape
    nseq, t_len, _ = x_sample.shape
    assert seq % PROMPT_TILE == 0 and PROMPT_TILE % RET_CHUNK == 0
    assert t_len == 8 and nseq % SAMPLE_SEQS == 0 and t_len % RET_CHUNK != 0

    row = lambda a: a.reshape(DEPTH, 1, a.shape[-1])
    w = {
        "w_in": w_in.astype(BF16), "b_gate": row(b_gate), "conv_w": conv_w, "ret_gn_g": row(ret_gn_g),
        "w_conv_out": w_conv_out.astype(BF16), "w_ret_out": w_ret_out.astype(BF16),
        "w_mem_out": w_mem_out.astype(BF16), "w_out": w_out.astype(BF16),
        "ln1_g": row(ln1_g), "ln1_b": row(ln1_b),
        "w_up": w_up.astype(BF16), "w_down": w_down.astype(BF16),
        "ln2_g": row(ln2_g), "ln2_b": row(ln2_b),
    }

    tabs_p = _decay_tables(RET_CHUNK)
    tabs_p["cos"], tabs_p["sin"] = _rotary_tables(jnp.arange(seq, dtype=jnp.int32), 1)
    tabs_s = _decay_tables(t_len)
    tabs_s["cos"], tabs_s["sin"] = _rotary_tables(PAST_LEN + jnp.arange(t_len, dtype=jnp.int32), SAMPLE_SEQS)

    mk_f, mv_f, mk_b, mv_b = _mem_kv(mem_prompt, w_mem_kv.astype(BF16))

    kc = cache_mem_k.reshape(DEPTH, nseq, MEM_LEN * MEM_HEADS, MEM_HEAD_DIM)
    vc = cache_mem_v.reshape(DEPTH, nseq, MEM_LEN * MEM_HEADS, MEM_HEAD_DIM)
    cst = jnp.concatenate(
        [state_conv, jnp.zeros((DEPTH, nseq, t_len - (CONV_K - 1), CONV_WIDTH), state_conv.dtype)], axis=2
    ).reshape(DEPTH, nseq * t_len, CONV_WIDTH)

    xp = x_prompt
    xs = x_sample.reshape(nseq * t_len, D_MODEL)
    cp_list, rp_list, cs_list = [], [], []
    ret_stack = None
    for l in range(DEPTH):
        xp1, ctail_p, ret_p = _prompt_mixer(l, xp, tabs_p, mk_b, mv_b, w)
        xp = _ffn(l, xp1.reshape(bp * seq, D_MODEL), w).reshape(bp, seq, D_MODEL)
        cp_list.append(ctail_p[:, 8 - (CONV_K - 1):, :])
        rp_list.append(ret_p)

        proj = _sample_proj(l, xs, w["w_in"])
        a_pre, b_pre, m_pre, u_s, ret_stack = _sample_state(l, proj, cst[l], tabs_s, state_ret, kc, vc, w, ret_stack)
        xs = _sample_post(l, xs, a_pre, b_pre, m_pre, proj, w)
        cs_list.append(u_s.reshape(nseq, t_len, CONV_WIDTH)[:, t_len - (CONV_K - 1):, :])

    mem_shape = (DEPTH, bp, MEM_LEN, MEM_HEADS, MEM_HEAD_DIM)
    return (xp, xs.reshape(nseq, t_len, D_MODEL), mk_f.reshape(mem_shape), mv_f.reshape(mem_shape),
            jnp.stack(cp_list), jnp.stack(rp_list), jnp.stack(cs_list), ret_stack)
```

```python
import functools

import jax
import jax.numpy as jnp
from jax import lax
from jax.experimental import pallas as pl
from jax.experimental.pallas import tpu as pltpu

D_MODEL = 1024
DEPTH = 4
CONV_WIDTH = 512
CONV_K = 3
RET_HEADS = 4
RET_DK = 128
RET_DV = 256
RET_QK_WIDTH = RET_HEADS * RET_DK
RET_V_WIDTH = RET_HEADS * RET_DV
RET_CHUNK = 128
MEM_LEN = 256
MEM_HEADS = 4
MEM_HEAD_DIM = 128
MEM_WIDTH = MEM_HEADS * MEM_HEAD_DIM
N_BRANCH = 3
D_FF = 4 * D_MODEL
ROPE_BASE = 10000.0
LN_EPS = 1e-5
GN_EPS = 1e-6
PAST_LEN = 16384
DEEPNORM_ALPHA = (2 * DEPTH) ** 0.25

OFF_CB = 0
OFF_CC = OFF_CB + CONV_WIDTH
OFF_CH = OFF_CC + CONV_WIDTH
OFF_RQ = OFF_CH + CONV_WIDTH
OFF_RK = OFF_RQ + RET_QK_WIDTH
OFF_RV = OFF_RK + RET_QK_WIDTH
OFF_RG = OFF_RV + RET_V_WIDTH
OFF_MQ = OFF_RG + RET_V_WIDTH
OFF_GL = OFF_MQ + MEM_WIDTH
IN_WIDTH = OFF_GL + N_BRANCH * D_MODEL

V7X_VMEM_BYTES = 64 * 1024 * 1024
VMEM_LIMIT_BYTES = V7X_VMEM_BYTES - 6 * 1024 * 1024

PROMPT_TILE = 512
FFN_TILE = 1024
FFN_CHUNK = 1024
SAMPLE_SEQS = 8
SAMPLE_POST_TILE = 256

BF16 = jnp.bfloat16
F32 = jnp.float32


def _dot(a, b):
    return jnp.dot(a, b, preferred_element_type=F32)


def _dot_nt(a, b):
    return lax.dot_general(a, b, (((1,), (1,)), ((), ())), preferred_element_type=F32)


def _dot_tn(a, b):
    return lax.dot_general(a, b, (((0,), (0,)), ((), ())), preferred_element_type=F32)


def _layer_norm(z, g, b):
    mu = jnp.mean(z, axis=-1, keepdims=True)
    zc = z - mu
    var = jnp.mean(zc * zc, axis=-1, keepdims=True)
    return zc * lax.rsqrt(var + LN_EPS) * g + b


def _group_norm(o):
    mu = jnp.mean(o, axis=-1, keepdims=True)
    oc = o - mu
    var = jnp.mean(oc * oc, axis=-1, keepdims=True)
    return oc * lax.rsqrt(var + GN_EPS)


def _rotary(xh, cos, sin_signed):
    return xh * cos + pltpu.roll(xh, RET_DK // 2, axis=1) * sin_signed


def _softmax_rows(s):
    m = jnp.max(s, axis=-1, keepdims=True)
    e = jnp.exp(s - m)
    return e / jnp.sum(e, axis=-1, keepdims=True)


def _merge_out_ln(x_ref, pre_refs, gate_logits, bg_ref, w_refs, wo_ref, g_ref, b_ref):
    merged = None
    for i, (pre_ref, w_ref) in enumerate(zip(pre_refs, w_refs)):
        gate = jax.nn.sigmoid(gate_logits(i) + bg_ref[:, i * D_MODEL:(i + 1) * D_MODEL])
        term = gate * _dot(pre_ref[...], w_ref[...])
        merged = term if merged is None else merged + term
    z = DEEPNORM_ALPHA * x_ref[...] + _dot(merged.astype(BF16), wo_ref[...])
    return _layer_norm(z, g_ref[...], b_ref[...])


def _ffn_ln(x, wup_ref, wdn_ref, g_ref, b_ref):
    xb = x.astype(BF16)
    acc = None
    for c in range(D_FF // FFN_CHUNK):
        cols = slice(c * FFN_CHUNK, (c + 1) * FFN_CHUNK)
        h = jnp.maximum(_dot(xb, wup_ref[:, cols]), 0.0)
        part = _dot((h * h).astype(BF16), wdn_ref[cols, :])
        acc = part if acc is None else acc + part
    return _layer_norm(DEEPNORM_ALPHA * x + acc, g_ref[...], b_ref[...])


def _mem_kv_kernel(mem_ref, w_ref, k_ref, v_ref, kb_ref, vb_ref):
    kv = _dot(mem_ref[...].astype(BF16), w_ref[...])
    k = kv[:, :MEM_WIDTH]
    v = kv[:, MEM_WIDTH:]
    k_ref[...] = k
    v_ref[...] = v
    kb_ref[...] = k.astype(BF16)
    vb_ref[...] = v.astype(BF16)


def _mem_kv(mem_prompt, w_mem_kv_b):
    bp = mem_prompt.shape[0]
    out_f = jax.ShapeDtypeStruct((DEPTH, bp, MEM_LEN, MEM_WIDTH), F32)
    out_b = jax.ShapeDtypeStruct((DEPTH, bp, MEM_LEN, MEM_WIDTH), BF16)
    o_spec = pl.BlockSpec((None, None, MEM_LEN, MEM_WIDTH), lambda l, b: (l, b, 0, 0))
    return pl.pallas_call(
        _mem_kv_kernel,
        grid=(DEPTH, bp),
        in_specs=[
            pl.BlockSpec((None, MEM_LEN, D_MODEL), lambda l, b: (b, 0, 0)),
            pl.BlockSpec((None, D_MODEL, 2 * MEM_WIDTH), lambda l, b: (l, 0, 0)),
        ],
        out_specs=[o_spec, o_spec, o_spec, o_spec],
        out_shape=[out_f, out_f, out_b, out_b],
        name="mem_kv",
    )(mem_prompt, w_mem_kv_b)


def _prompt_mixer_kernel(cd_ref, x_ref, cos_ref, sin_ref, dmask_ref, qdec_ref, kdec_ref, mk_ref, mv_ref,
                         win_ref, bg_ref, cw_ref, gn_ref, wco_ref, wro_ref, wmo_ref, wo_ref, lg_ref, lb_ref,
                         y_ref, ctail_ref, ret_ref,
                         xb_ref, q_ref, k_ref, kd_ref, vb_ref, msk_ref, kv_ref, abuf_ref, bbuf_ref, mbuf_ref):
    t = pl.program_id(1)
    tq = x_ref.shape[0]
    chunks = [slice(j * RET_CHUNK, (j + 1) * RET_CHUNK) for j in range(tq // RET_CHUNK)]
    kcols = [slice(h * RET_DK, (h + 1) * RET_DK) for h in range(RET_HEADS)]
    vcols = [slice(h * RET_DV, (h + 1) * RET_DV) for h in range(RET_HEADS)]
    mcols = [slice(h * MEM_HEAD_DIM, (h + 1) * MEM_HEAD_DIM) for h in range(MEM_HEADS)]

    @pl.when(t == 0)
    def _():
        ret_ref[...] = jnp.zeros_like(ret_ref)
        ctail_ref[...] = jnp.zeros_like(ctail_ref)

    xb_ref[...] = x_ref[...].astype(BF16)

    def proj(lo, width):
        return _dot(xb_ref[...], win_ref[:, lo:lo + width])

    pq = proj(OFF_RQ, RET_QK_WIDTH)
    pk = proj(OFF_RK, RET_QK_WIDTH)
    vb_ref[...] = proj(OFF_RV, RET_V_WIDTH).astype(BF16)
    cos = cos_ref[...]
    sin = sin_ref[...]
    for h in range(RET_HEADS):
        q_ref[:, kcols[h]] = _rotary(pq[:, kcols[h]], cos, sin).astype(BF16)
        kh = _rotary(pk[:, kcols[h]], cos, sin) * (RET_DK ** -0.5)
        k_ref[:, kcols[h]] = kh.astype(BF16)
        for r in chunks:
            kd_ref[r, kcols[h]] = (kh[r] * kdec_ref[h]).astype(BF16)

    pm = proj(OFF_MQ, MEM_WIDTH).astype(BF16)
    mem_s = [_dot_nt(pm[:, mcols[h]], mk_ref[:, mcols[h]]) * (MEM_HEAD_DIM ** -0.5) for h in range(MEM_HEADS)]

    cb = proj(OFF_CB, CONV_WIDTH)
    u = proj(OFF_CC, CONV_WIDTH) * proj(OFF_CH, CONV_WIDTH)
    row = lax.broadcasted_iota(jnp.int32, u.shape, 0)
    prev1 = ctail_ref[7:8, :]
    prev2 = ctail_ref[6:7, :]
    u1 = jnp.where(row == 0, prev1, pltpu.roll(u, 1, axis=0))
    u2 = jnp.where(row == 0, prev2, jnp.where(row == 1, prev1, pltpu.roll(u, 2, axis=0)))
    conv_y = u2 * cw_ref[0:1, :] + u1 * cw_ref[1:2, :] + u * cw_ref[2:3, :]
    ctail_ref[...] = u[tq - 8:, :]
    abuf_ref[...] = (cb * conv_y).astype(BF16)

    for h in range(MEM_HEADS):
        p = _softmax_rows(mem_s[h])
        mbuf_ref[:, mcols[h]] = _dot(p.astype(BF16), mv_ref[:, mcols[h]]).astype(BF16)

    for h in range(RET_HEADS):
        for r in chunks:
            msk_ref[r, kcols[h]] = (_dot_nt(q_ref[r, kcols[h]], k_ref[r, kcols[h]]) * dmask_ref[h]).astype(BF16)

    for h in range(RET_HEADS):
        for j, r in enumerate(chunks):
            kv_ref[h, j] = _dot_tn(kd_ref[r, kcols[h]], vb_ref[r, vcols[h]])

    for h in range(RET_HEADS):
        pg = proj(OFF_RG + h * RET_DV, RET_DV)
        s_cur = ret_ref[h]
        for j, r in enumerate(chunks):
            intra = _dot(msk_ref[r, kcols[h]], vb_ref[r, vcols[h]])
            o = intra + _dot(q_ref[r, kcols[h]], s_cur.astype(BF16)) * qdec_ref[h]
            gated = jax.nn.silu(pg[r]) * (_group_norm(o) * gn_ref[:, vcols[h]])
            bbuf_ref[r, vcols[h]] = gated.astype(BF16)
            s_cur = s_cur * cd_ref[h] + kv_ref[h, j]
        ret_ref[h] = s_cur

    y_ref[...] = _merge_out_ln(x_ref, (abuf_ref, bbuf_ref, mbuf_ref),
                               lambda i: proj(OFF_GL + i * D_MODEL, D_MODEL), bg_ref,
                               (wco_ref, wro_ref, wmo_ref), wo_ref, lg_ref, lb_ref)


def _resident(shape, layer):
    nd = len(shape)
    return pl.BlockSpec((None,) + tuple(shape), lambda *_: (layer,) + (0,) * nd,
                        pipeline_mode=pl.Buffered(1))


def _const(shape):
    nd = len(shape)
    return pl.BlockSpec(tuple(shape), lambda *_: (0,) * nd, pipeline_mode=pl.Buffered(1))


def _prompt_mixer(layer, x, tabs, mk_b, mv_b, w):
    bp, seq, _ = x.shape
    tq = PROMPT_TILE
    grid = (bp, seq // tq)
    in_specs = [
        pl.BlockSpec(memory_space=pltpu.SMEM),
        pl.BlockSpec((None, tq, D_MODEL), lambda b, t: (b, t, 0)),
        pl.BlockSpec((tq, RET_DK), lambda b, t: (t, 0)),
        pl.BlockSpec((tq, RET_DK), lambda b, t: (t, 0)),
        _const((RET_HEADS, RET_CHUNK, RET_CHUNK)),
        _const((RET_HEADS, RET_CHUNK, RET_DV)),
        _const((RET_HEADS, RET_CHUNK, RET_DK)),
        pl.BlockSpec((None, None, MEM_LEN, MEM_WIDTH), lambda b, t: (layer, b, 0, 0)),
        pl.BlockSpec((None, None, MEM_LEN, MEM_WIDTH), lambda b, t: (layer, b, 0, 0)),
        _resident((D_MODEL, IN_WIDTH), layer),
        _resident((1, N_BRANCH * D_MODEL), layer),
        _resident((CONV_K, CONV_WIDTH), layer),
        _resident((1, RET_V_WIDTH), layer),
        _resident((CONV_WIDTH, D_MODEL), layer),
        _resident((RET_V_WIDTH, D_MODEL), layer),
        _resident((MEM_WIDTH, D_MODEL), layer),
        _resident((D_MODEL, D_MODEL), layer),
        _resident((1, D_MODEL), layer),
        _resident((1, D_MODEL), layer),
    ]
    out_specs = [
        pl.BlockSpec((None, tq, D_MODEL), lambda b, t: (b, t, 0)),
        pl.BlockSpec((None, 8, CONV_WIDTH), lambda b, t: (b, 0, 0)),
        pl.BlockSpec((None, RET_HEADS, RET_DK, RET_DV), lambda b, t: (b, 0, 0, 0)),
    ]
    out_shape = [
        jax.ShapeDtypeStruct((bp, seq, D_MODEL), F32),
        jax.ShapeDtypeStruct((bp, 8, CONV_WIDTH), F32),
        jax.ShapeDtypeStruct((bp, RET_HEADS, RET_DK, RET_DV), F32),
    ]
    return pl.pallas_call(
        _prompt_mixer_kernel,
        grid=grid,
        in_specs=in_specs,
        out_specs=out_specs,
        out_shape=out_shape,
        scratch_shapes=[
            pltpu.VMEM((tq, D_MODEL), BF16),
            pltpu.VMEM((tq, RET_QK_WIDTH), BF16),
            pltpu.VMEM((tq, RET_QK_WIDTH), BF16),
            pltpu.VMEM((tq, RET_QK_WIDTH), BF16),
            pltpu.VMEM((tq, RET_V_WIDTH), BF16),
            pltpu.VMEM((tq, RET_QK_WIDTH), BF16),
            pltpu.VMEM((RET_HEADS, tq // RET_CHUNK, RET_DK, RET_DV), F32),
            pltpu.VMEM((tq, CONV_WIDTH), BF16),
            pltpu.VMEM((tq, RET_V_WIDTH), BF16),
            pltpu.VMEM((tq, MEM_WIDTH), BF16),
        ],
        compiler_params=pltpu.CompilerParams(
            dimension_semantics=("arbitrary", "arbitrary"), vmem_limit_bytes=VMEM_LIMIT_BYTES),
        name="prompt_mixer",
    )(tabs["cd"], x, tabs["cos"], tabs["sin"], tabs["dmask"], tabs["qdec"], tabs["kdec"], mk_b, mv_b,
      w["w_in"], w["b_gate"], w["conv_w"], w["ret_gn_g"], w["w_conv_out"], w["w_ret_out"], w["w_mem_out"],
      w["w_out"], w["ln1_g"], w["ln1_b"])


def _ffn_kernel(x_ref, wup_ref, wdn_ref, g_ref, b_ref, y_ref):
    y_ref[...] = _ffn_ln(x_ref[...], wup_ref, wdn_ref, g_ref, b_ref)


def _ffn(layer, x, w):
    n = x.shape[0]
    tm = FFN_TILE
    return pl.pallas_call(
        _ffn_kernel,
        grid=(n // tm,),
        in_specs=[
            pl.BlockSpec((tm, D_MODEL), lambda i: (i, 0)),
            _resident((D_MODEL, D_FF), layer),
            _resident((D_FF, D_MODEL), layer),
            _resident((1, D_MODEL), layer),
            _resident((1, D_MODEL), layer),
        ],
        out_specs=pl.BlockSpec((tm, D_MODEL), lambda i: (i, 0)),
        out_shape=jax.ShapeDtypeStruct((n, D_MODEL), F32),
        compiler_params=pltpu.CompilerParams(
            dimension_semantics=("arbitrary",), vmem_limit_bytes=VMEM_LIMIT_BYTES),
        name="ffn",
    )(x, w["w_up"], w["w_down"], w["ln2_g"], w["ln2_b"])


def _sample_proj_kernel(x_ref, w_ref, o_ref):
    o_ref[...] = _dot(x_ref[...].astype(BF16), w_ref[...])


def _sample_proj(layer, x, w_in_b):
    n = x.shape[0]
    tn = 1024
    return pl.pallas_call(
        _sample_proj_kernel,
        grid=(IN_WIDTH // tn,),
        in_specs=[
            pl.BlockSpec((n, D_MODEL), lambda j: (0, 0)),
            pl.BlockSpec((None, D_MODEL, tn), lambda j: (layer, 0, j)),
        ],
        out_specs=pl.BlockSpec((n, tn), lambda j: (0, j)),
        out_shape=jax.ShapeDtypeStruct((n, IN_WIDTH), F32),
        compiler_params=pltpu.CompilerParams(
            dimension_semantics=("arbitrary",), vmem_limit_bytes=VMEM_LIMIT_BYTES),
        name="sample_proj",
    )(x, w_in_b)


def _sample_state_kernel(cd_ref, proj_ref, cst_ref, cos_ref, sin_ref, dmask_ref, qdec_ref, kdec_ref,
                         sret_ref, kc_ref, vc_ref, cw_ref, gn_ref,
                         a_ref, b_ref, m_ref, u_ref, nret_ref,
                         o_scr, p_scr, m_scr):
    ns = sret_ref.shape[0]
    t_len = proj_ref.shape[0] // ns

    cb = proj_ref[:, OFF_CB:OFF_CB + CONV_WIDTH]
    u = proj_ref[:, OFF_CC:OFF_CC + CONV_WIDTH] * proj_ref[:, OFF_CH:OFF_CH + CONV_WIDTH]
    tok = lax.broadcasted_iota(jnp.int32, u.shape, 0) & (t_len - 1)
    st2 = cst_ref[...]
    st1 = pltpu.roll(st2, st2.shape[0] - 1, axis=0)
    u1 = jnp.where(tok == 0, st1, pltpu.roll(u, 1, axis=0))
    u2 = jnp.where(tok < 2, st2, pltpu.roll(u, 2, axis=0))
    conv_y = u2 * cw_ref[0:1, :] + u1 * cw_ref[1:2, :] + u * cw_ref[2:3, :]
    u_ref[...] = u
    a_ref[...] = (cb * conv_y).astype(BF16)

    rows = [slice(s * t_len, (s + 1) * t_len) for s in range(ns)]
    kcols = [slice(h * RET_DK, (h + 1) * RET_DK) for h in range(RET_HEADS)]
    vcols = [slice(h * RET_DV, (h + 1) * RET_DV) for h in range(RET_HEADS)]
    mcols = [slice(h * MEM_HEAD_DIM, (h + 1) * MEM_HEAD_DIM) for h in range(MEM_HEADS)]
    pcols = [slice(h * MEM_LEN, (h + 1) * MEM_LEN) for h in range(MEM_HEADS)]
    head_rows = [pl.ds(h, MEM_LEN, stride=MEM_HEADS) for h in range(MEM_HEADS)]

    cos = cos_ref[...]
    sin = sin_ref[...]
    q, kd = [], []
    for h in range(RET_HEADS):
        qh = _rotary(proj_ref[:, OFF_RQ + h * RET_DK:OFF_RQ + (h + 1) * RET_DK], cos, sin)
        kh = _rotary(proj_ref[:, OFF_RK + h * RET_DK:OFF_RK + (h + 1) * RET_DK], cos, sin) * (RET_DK ** -0.5)
        q.append(qh)
        kd.append(kh * kdec_ref[h])
        vb = proj_ref[:, OFF_RV + h * RET_DV:OFF_RV + (h + 1) * RET_DV].astype(BF16)
        masked = (_dot_nt(qh.astype(BF16), kh.astype(BF16)) * dmask_ref[h]).astype(BF16)
        o_scr[:, vcols[h]] = _dot(masked, vb)

    for s in range(ns):
        for h in range(RET_HEADS):
            s_prev = sret_ref[s, h]
            inter = _dot(q[h][rows[s]].astype(BF16), s_prev.astype(BF16))
            o_scr[rows[s], vcols[h]] += inter * qdec_ref[h]
            vb = proj_ref[rows[s], OFF_RV + h * RET_DV:OFF_RV + (h + 1) * RET_DV].astype(BF16)
            nret_ref[s, h] = s_prev * cd_ref[h] + _dot_tn(kd[h][rows[s]].astype(BF16), vb)

    for s in range(ns):
        for h in range(MEM_HEADS):
            mq = proj_ref[rows[s], OFF_MQ + h * MEM_HEAD_DIM:OFF_MQ + (h + 1) * MEM_HEAD_DIM].astype(BF16)
            p_scr[rows[s], pcols[h]] = _dot_nt(mq, kc_ref[s, head_rows[h], :].astype(BF16)) * (MEM_HEAD_DIM ** -0.5)
    for h in range(MEM_HEADS):
        p_scr[:, pcols[h]] = _softmax_rows(p_scr[:, pcols[h]])
    for s in range(ns):
        for h in range(MEM_HEADS):
            p = p_scr[rows[s], pcols[h]].astype(BF16)
            m_scr[rows[s], mcols[h]] = _dot(p, vc_ref[s, head_rows[h], :].astype(BF16))
    m_ref[...] = m_scr[...].astype(BF16)

    for h in range(RET_HEADS):
        rg = proj_ref[:, OFF_RG + h * RET_DV:OFF_RG + (h + 1) * RET_DV]
        gated = jax.nn.silu(rg) * (_group_norm(o_scr[:, vcols[h]]) * gn_ref[:, vcols[h]])
        b_ref[:, vcols[h]] = gated.astype(BF16)


def _sample_state_kernel_into(stack_ref, *refs):
    del stack_ref
    _sample_state_kernel(*refs)


def _sample_state(layer, proj, cst, tabs, state_ret, kc, vc, w, ret_stack):
    n = proj.shape[0]
    nseq = state_ret.shape[1]
    t_len = n // nseq
    ns = SAMPLE_SEQS
    rows = ns * t_len
    grid = (nseq // ns,)
    tok_spec = lambda width: pl.BlockSpec((rows, width), lambda i: (i, 0))
    in_specs = [
        pl.BlockSpec(memory_space=pltpu.SMEM),
        tok_spec(OFF_GL),
        tok_spec(CONV_WIDTH),
        _const((rows, RET_DK)),
        _const((rows, RET_DK)),
        _const((RET_HEADS, rows, rows)),
        _const((RET_HEADS, t_len, RET_DV)),
        _const((RET_HEADS, rows, RET_DK)),
        pl.BlockSpec((None, ns, RET_HEADS, RET_DK, RET_DV), lambda i: (layer, i, 0, 0, 0)),
        pl.BlockSpec((None, ns, MEM_LEN * MEM_HEADS, MEM_HEAD_DIM), lambda i: (layer, i, 0, 0)),
        pl.BlockSpec((None, ns, MEM_LEN * MEM_HEADS, MEM_HEAD_DIM), lambda i: (layer, i, 0, 0)),
        _resident((CONV_K, CONV_WIDTH), layer),
        _resident((1, RET_V_WIDTH), layer),
    ]
    out_specs = [
        tok_spec(CONV_WIDTH),
        tok_spec(RET_V_WIDTH),
        tok_spec(MEM_WIDTH),
        tok_spec(CONV_WIDTH),
        pl.BlockSpec((None, ns, RET_HEADS, RET_DK, RET_DV), lambda i: (layer, i, 0, 0, 0)),
    ]
    out_shape = [
        jax.ShapeDtypeStruct((n, CONV_WIDTH), BF16),
        jax.ShapeDtypeStruct((n, RET_V_WIDTH), BF16),
        jax.ShapeDtypeStruct((n, MEM_WIDTH), BF16),
        jax.ShapeDtypeStruct((n, CONV_WIDTH), F32),
        jax.ShapeDtypeStruct((DEPTH, nseq, RET_HEADS, RET_DK, RET_DV), F32),
    ]
    args = (tabs["cd"], proj, cst, tabs["cos"], tabs["sin"], tabs["dmask"], tabs["qdec"], tabs["kdec"],
            state_ret, kc, vc, w["conv_w"], w["ret_gn_g"])
    body, aliases = _sample_state_kernel, {}
    if ret_stack is not None:
        body, aliases = _sample_state_kernel_into, {0: len(out_shape) - 1}
        in_specs = [pl.BlockSpec(memory_space=pl.ANY)] + in_specs
        args = (ret_stack,) + args
    scratch = [
        pltpu.VMEM((rows, RET_V_WIDTH), F32),
        pltpu.VMEM((rows, MEM_HEADS * MEM_LEN), F32),
        pltpu.VMEM((rows, MEM_WIDTH), F32),
    ]
    return pl.pallas_call(
        body,
        grid=grid,
        in_specs=in_specs,
        out_specs=out_specs,
        out_shape=out_shape,
        scratch_shapes=scratch,
        input_output_aliases=aliases,
        compiler_params=pltpu.CompilerParams(
            dimension_semantics=("arbitrary",), vmem_limit_bytes=VMEM_LIMIT_BYTES),
        name="sample_state",
    )(*args)


def _sample_post_kernel(x_ref, a_ref, b_ref, m_ref, g0_ref, g1_ref, g2_ref,
                        bg_ref, wco_ref, wro_ref, wmo_ref, wo_ref, l1g_ref, l1b_ref,
                        wup_ref, wdn_ref, l2g_ref, l2b_ref, y_ref):
    gate_refs = (g0_ref, g1_ref, g2_ref)
    x1 = _merge_out_ln(x_ref, (a_ref, b_ref, m_ref), lambda i: gate_refs[i][...], bg_ref,
                       (wco_ref, wro_ref, wmo_ref), wo_ref, l1g_ref, l1b_ref)
    y_ref[...] = _ffn_ln(x1, wup_ref, wdn_ref, l2g_ref, l2b_ref)


def _sample_post(layer, x, a_pre, b_pre, m_pre, proj, w):
    n = x.shape[0]
    tm = SAMPLE_POST_TILE
    tok = lambda width: pl.BlockSpec((tm, width), lambda i: (i, 0))
    gate = lambda k: pl.BlockSpec((tm, D_MODEL), lambda i: (i, OFF_GL // D_MODEL + k))
    return pl.pallas_call(
        _sample_post_kernel,
        grid=(n // tm,),
        in_specs=[
            tok(D_MODEL), tok(CONV_WIDTH), tok(RET_V_WIDTH), tok(MEM_WIDTH), gate(0), gate(1), gate(2),
            _resident((1, N_BRANCH * D_MODEL), layer),
            _resident((CONV_WIDTH, D_MODEL), layer),
            _resident((RET_V_WIDTH, D_MODEL), layer),
            _resident((MEM_WIDTH, D_MODEL), layer),
            _resident((D_MODEL, D_MODEL), layer),
            _resident((1, D_MODEL), layer),
            _resident((1, D_MODEL), layer),
            _resident((D_MODEL, D_FF), layer),
            _resident((D_FF, D_MODEL), layer),
            _resident((1, D_MODEL), layer),
            _resident((1, D_MODEL), layer),
        ],
        out_specs=tok(D_MODEL),
        out_shape=jax.ShapeDtypeStruct((n, D_MODEL), F32),
        compiler_params=pltpu.CompilerParams(
            dimension_semantics=("arbitrary",), vmem_limit_bytes=VMEM_LIMIT_BYTES),
        name="sample_post",
    )(x, a_pre, b_pre, m_pre, proj, proj, proj,
      w["b_gate"], w["w_conv_out"], w["w_ret_out"], w["w_mem_out"], w["w_out"], w["ln1_g"], w["ln1_b"],
      w["w_up"], w["w_down"], w["ln2_g"], w["ln2_b"])


def _rotary_tables(pos, reps):
    half = RET_DK // 2
    inv = ROPE_BASE ** (-jnp.arange(half, dtype=F32) / half)
    ang = pos.astype(F32)[:, None] * inv[None, :]
    cos = jnp.cos(ang)
    sin = jnp.sin(ang)
    cos_full = jnp.concatenate([cos, cos], axis=-1)
    sin_signed = jnp.concatenate([-sin, sin], axis=-1)
    return jnp.tile(cos_full, (reps, 1)), jnp.tile(sin_signed, (reps, 1))


def _decay_tables(c):
    h = RET_HEADS
    log_g = jnp.log1p(-jnp.exp2(-5.0 - jnp.arange(h, dtype=F32)))
    idx = jnp.arange(c, dtype=F32)
    rel = idx[:, None] - idx[None, :]
    causal = rel >= 0
    dmask = jnp.where(causal[None], jnp.exp(jnp.where(causal, rel, 0.0)[None] * log_g[:, None, None]), 0.0)
    q_decay = jnp.exp((idx + 1.0)[None, :] * log_g[:, None])
    k_decay = jnp.exp((c - 1.0 - idx)[None, :] * log_g[:, None])
    chunk_decay = jnp.exp(c * log_g)
    return {
        "dmask": dmask,
        "qdec": jnp.broadcast_to(q_decay[:, :, None], (h, c, RET_DV)),
        "kdec": jnp.broadcast_to(k_decay[:, :, None], (h, c, RET_DK)),
        "cd": chunk_decay,
    }


def kernel(x_prompt, x_sample, cache_mem_k, cache_mem_v, state_conv, state_ret, mem_prompt,
           w_in, b_gate, conv_w, ret_gn_g, w_conv_out, w_ret_out, w_mem_out, w_out, w_mem_kv,
           ln1_g, ln1_b, w_up, w_down, ln2_g, ln2_b):
    bp, seq, _ = x_prompt.shape
    nseq, t_len, _ = x_sample.shape
    assert seq % PROMPT_TILE == 0 and PROMPT_TILE % RET_CHUNK == 0
    assert t_len == 8 and nseq % SAMPLE_SEQS == 0 and t_len % RET_CHUNK != 0

    row = lambda a: a.reshape(DEPTH, 1, a.shape[-1])
    w = {
        "w_in": w_in.astype(BF16), "b_gate": row(b_gate), "conv_w": conv_w, "ret_gn_g": row(ret_gn_g),
        "w_conv_out": w_conv_out.astype(BF16), "w_ret_out": w_ret_out.astype(BF16),
        "w_mem_out": w_mem_out.astype(BF16), "w_out": w_out.astype(BF16),
        "ln1_g": row(ln1_g), "ln1_b": row(ln1_b),
        "w_up": w_up.astype(BF16), "w_down": w_down.astype(BF16),
        "ln2_g": row(ln2_g), "ln2_b": row(ln2_b),
    }

    tabs_p = _decay_tables(RET_CHUNK)
    tabs_p["cos"], tabs_p["sin"] = _rotary_tables(jnp.arange(seq, dtype=jnp.int32), 1)
    tabs_s = _decay_tables(t_len)
    eye = jnp.eye(SAMPLE_SEQS, dtype=F32)
    tabs_s["dmask"] = (eye[None, :, None, :, None] * tabs_s["dmask"][:, None, :, None, :]).reshape(
        RET_HEADS, SAMPLE_SEQS * t_len, SAMPLE_SEQS * t_len)
    tabs_s["kdec"] = jnp.tile(tabs_s["kdec"], (1, SAMPLE_SEQS, 1))
    tabs_s["cos"], tabs_s["sin"] = _rotary_tables(PAST_LEN + jnp.arange(t_len, dtype=jnp.int32), SAMPLE_SEQS)

    mk_f, mv_f, mk_b, mv_b = _mem_kv(mem_prompt, w_mem_kv.astype(BF16))

    kc = cache_mem_k.reshape(DEPTH, nseq, MEM_LEN * MEM_HEADS, MEM_HEAD_DIM)
    vc = cache_mem_v.reshape(DEPTH, nseq, MEM_LEN * MEM_HEADS, MEM_HEAD_DIM)
    cst = jnp.concatenate(
        [state_conv, jnp.zeros((DEPTH, nseq, t_len - (CONV_K - 1), CONV_WIDTH), state_conv.dtype)], axis=2
    ).reshape(DEPTH, nseq * t_len, CONV_WIDTH)

    xp = x_prompt
    xs = x_sample.reshape(nseq * t_len, D_MODEL)
    cp_list, rp_list, cs_list = [], [], []
    ret_stack = None
    for l in range(DEPTH):
        xp1, ctail_p, ret_p = _prompt_mixer(l, xp, tabs_p, mk_b, mv_b, w)
        xp = _ffn(l, xp1.reshape(bp * seq, D_MODEL), w).reshape(bp, seq, D_MODEL)
        cp_list.append(ctail_p[:, 8 - (CONV_K - 1):, :])
        rp_list.append(ret_p)

        proj = _sample_proj(l, xs, w["w_in"])
        a_pre, b_pre, m_pre, u_s, ret_stack = _sample_state(l, proj, cst[l], tabs_s, state_ret, kc, vc, w, ret_stack)
        xs = _sample_post(l, xs, a_pre, b_pre, m_pre, proj, w)
        cs_list.append(u_s.reshape(nseq, t_len, CONV_WIDTH)[:, t_len - (CONV_K - 1):, :])

    mem_shape = (DEPTH, bp, MEM_LEN, MEM_HEADS, MEM_HEAD_DIM)
    return (xp, xs.reshape(nseq, t_len, D_MODEL), mk_f.reshape(mem_shape), mv_f.reshape(mem_shape),
            jnp.stack(cp_list), jnp.stack(rp_list), jnp.stack(cs_list), ret_stack)
```

```python
import functools

import jax
import jax.numpy as jnp
from jax import lax
from jax.experimental import pallas as pl
from jax.experimental.pallas import tpu as pltpu

D_MODEL = 1024
DEPTH = 4
CONV_WIDTH = 512
CONV_K = 3
RET_HEADS = 4
RET_DK = 128
RET_DV = 256
RET_QK_WIDTH = RET_HEADS * RET_DK
RET_V_WIDTH = RET_HEADS * RET_DV
RET_CHUNK = 128
MEM_LEN = 256
MEM_HEADS = 4
MEM_HEAD_DIM = 128
MEM_WIDTH = MEM_HEADS * MEM_HEAD_DIM
N_BRANCH = 3
D_FF = 4 * D_MODEL
ROPE_BASE = 10000.0
LN_EPS = 1e-5
GN_EPS = 1e-6
PAST_LEN = 16384
DEEPNORM_ALPHA = (2 * DEPTH) ** 0.25

OFF_CB = 0
OFF_CC = OFF_CB + CONV_WIDTH
OFF_CH = OFF_CC + CONV_WIDTH
OFF_RQ = OFF_CH + CONV_WIDTH
OFF_RK = OFF_RQ + RET_QK_WIDTH
OFF_RV = OFF_RK + RET_QK_WIDTH
OFF_RG = OFF_RV + RET_V_WIDTH
OFF_MQ = OFF_RG + RET_V_WIDTH
OFF_GL = OFF_MQ + MEM_WIDTH
IN_WIDTH = OFF_GL + N_BRANCH * D_MODEL

V7X_VMEM_BYTES = 64 * 1024 * 1024
VMEM_LIMIT_BYTES = V7X_VMEM_BYTES - 6 * 1024 * 1024

PROMPT_TILE = 512
FFN_TILE = 1024
FFN_CHUNK = 1024
SAMPLE_SEQS = 8
SAMPLE_POST_TILE = 512

BF16 = jnp.bfloat16
F32 = jnp.float32


def _dot(a, b):
    return jnp.dot(a, b, preferred_element_type=F32)


def _dot_nt(a, b):
    return lax.dot_general(a, b, (((1,), (1,)), ((), ())), preferred_element_type=F32)


def _dot_tn(a, b):
    return lax.dot_general(a, b, (((0,), (0,)), ((), ())), preferred_element_type=F32)


def _layer_norm(z, g, b):
    mu = jnp.mean(z, axis=-1, keepdims=True)
    zc = z - mu
    var = jnp.mean(zc * zc, axis=-1, keepdims=True)
    return zc * lax.rsqrt(var + LN_EPS) * g + b


def _group_norm(o):
    mu = jnp.mean(o, axis=-1, keepdims=True)
    oc = o - mu
    var = jnp.mean(oc * oc, axis=-1, keepdims=True)
    return oc * lax.rsqrt(var + GN_EPS)


def _rotary(xh, cos, sin_signed):
    return xh * cos + pltpu.roll(xh, RET_DK // 2, axis=1) * sin_signed


def _softmax_rows(s):
    m = jnp.max(s, axis=-1, keepdims=True)
    e = jnp.exp(s - m)
    return e / jnp.sum(e, axis=-1, keepdims=True)


def _merge_out_ln(x_ref, pre_refs, gate_logits, bg_ref, w_refs, wo_ref, g_ref, b_ref, y_ref):
    merged = None
    for i, (pre_ref, w_ref) in enumerate(zip(pre_refs, w_refs)):
        gate = jax.nn.sigmoid(gate_logits(i) + bg_ref[:, i * D_MODEL:(i + 1) * D_MODEL])
        term = gate * _dot(pre_ref[...], w_ref[...])
        merged = term if merged is None else merged + term
    merged = merged.astype(BF16)
    half = merged.shape[0] // 2
    for rows in (slice(0, half), slice(half, 2 * half)):
        z = DEEPNORM_ALPHA * x_ref[rows, :] + _dot(merged[rows], wo_ref[...])
        y_ref[rows, :] = _layer_norm(z, g_ref[...], b_ref[...])


def _ffn_ln(x, wup_ref, wdn_ref, g_ref, b_ref):
    xb = x.astype(BF16)
    acc = None
    for c in range(D_FF // FFN_CHUNK):
        cols = slice(c * FFN_CHUNK, (c + 1) * FFN_CHUNK)
        h = jnp.maximum(_dot(xb, wup_ref[:, cols]), 0.0)
        part = _dot((h * h).astype(BF16), wdn_ref[cols, :])
        acc = part if acc is None else acc + part
    return _layer_norm(DEEPNORM_ALPHA * x + acc, g_ref[...], b_ref[...])


MEM_KV_ROWS = 1024


def _mem_kv_kernel(mem_ref, w_ref, k_ref, v_ref, kb_ref, vb_ref):
    for r0 in range(0, mem_ref.shape[0], MEM_KV_ROWS):
        rows = slice(r0, r0 + MEM_KV_ROWS)
        kv = _dot(mem_ref[rows, :].astype(BF16), w_ref[...])
        kb_ref[rows, :] = kv[:, :MEM_WIDTH].astype(BF16)
        vb_ref[rows, :] = kv[:, MEM_WIDTH:].astype(BF16)
        for h in range(MEM_HEADS):
            head_rows = pl.ds(r0 * MEM_HEADS + h, MEM_KV_ROWS, stride=MEM_HEADS)
            k_ref[head_rows, :] = kv[:, h * MEM_HEAD_DIM:(h + 1) * MEM_HEAD_DIM]
            v_ref[head_rows, :] = kv[:, MEM_WIDTH + h * MEM_HEAD_DIM:MEM_WIDTH + (h + 1) * MEM_HEAD_DIM]


def _mem_kv(mem_prompt, w_mem_kv_b):
    bp = mem_prompt.shape[0]
    n = bp * MEM_LEN
    assert n % MEM_KV_ROWS == 0
    out_f = jax.ShapeDtypeStruct((DEPTH, n * MEM_HEADS, MEM_HEAD_DIM), F32)
    out_b = jax.ShapeDtypeStruct((DEPTH, n, MEM_WIDTH), BF16)
    f_spec = pl.BlockSpec((None, n * MEM_HEADS, MEM_HEAD_DIM), lambda l: (l, 0, 0))
    b_spec = pl.BlockSpec((None, n, MEM_WIDTH), lambda l: (l, 0, 0))
    return pl.pallas_call(
        _mem_kv_kernel,
        grid=(DEPTH,),
        in_specs=[
            _const((n, D_MODEL)),
            pl.BlockSpec((None, D_MODEL, 2 * MEM_WIDTH), lambda l: (l, 0, 0)),
        ],
        out_specs=[f_spec, f_spec, b_spec, b_spec],
        out_shape=[out_f, out_f, out_b, out_b],
        compiler_params=pltpu.CompilerParams(
            dimension_semantics=("arbitrary",), vmem_limit_bytes=VMEM_LIMIT_BYTES),
        name="mem_kv",
    )(mem_prompt.reshape(n, D_MODEL), w_mem_kv_b)


def _prompt_mixer_kernel(cd_ref, x_ref, cos_ref, sin_ref, dmask_ref, qdec_ref, kdec_ref, mk_ref, mv_ref,
                         win_ref, bg_ref, cw_ref, gn_ref, wco_ref, wro_ref, wmo_ref, wo_ref, lg_ref, lb_ref,
                         y_ref, ctail_ref, ret_ref,
                         xb_ref, q_ref, k_ref, kd_ref, vb_ref, msk_ref, kv_ref, abuf_ref, bbuf_ref, mbuf_ref):
    t = pl.program_id(1)
    tq = x_ref.shape[0]
    chunks = [slice(j * RET_CHUNK, (j + 1) * RET_CHUNK) for j in range(tq // RET_CHUNK)]
    kcols = [slice(h * RET_DK, (h + 1) * RET_DK) for h in range(RET_HEADS)]
    vcols = [slice(h * RET_DV, (h + 1) * RET_DV) for h in range(RET_HEADS)]
    mcols = [slice(h * MEM_HEAD_DIM, (h + 1) * MEM_HEAD_DIM) for h in range(MEM_HEADS)]

    @pl.when(t == 0)
    def _():
        ret_ref[...] = jnp.zeros_like(ret_ref)
        ctail_ref[...] = jnp.zeros_like(ctail_ref)

    xb_ref[...] = x_ref[...].astype(BF16)

    def proj(lo, width):
        return _dot(xb_ref[...], win_ref[:, lo:lo + width])

    pq = proj(OFF_RQ, RET_QK_WIDTH)
    pk = proj(OFF_RK, RET_QK_WIDTH)
    vb_ref[...] = proj(OFF_RV, RET_V_WIDTH).astype(BF16)
    cos = cos_ref[...]
    sin = sin_ref[...]
    for h in range(RET_HEADS):
        q_ref[:, kcols[h]] = _rotary(pq[:, kcols[h]], cos, sin).astype(BF16)
        kh = _rotary(pk[:, kcols[h]], cos, sin) * (RET_DK ** -0.5)
        k_ref[:, kcols[h]] = kh.astype(BF16)
        for r in chunks:
            kd_ref[r, kcols[h]] = (kh[r] * kdec_ref[h]).astype(BF16)

    pm = proj(OFF_MQ, MEM_WIDTH).astype(BF16)
    mem_s = [_dot_nt(pm[:, mcols[h]], mk_ref[:, mcols[h]]) * (MEM_HEAD_DIM ** -0.5) for h in range(MEM_HEADS)]

    cb = proj(OFF_CB, CONV_WIDTH)
    u = proj(OFF_CC, CONV_WIDTH) * proj(OFF_CH, CONV_WIDTH)
    row = lax.broadcasted_iota(jnp.int32, u.shape, 0)
    prev1 = ctail_ref[7:8, :]
    prev2 = ctail_ref[6:7, :]
    u1 = jnp.where(row == 0, prev1, pltpu.roll(u, 1, axis=0))
    u2 = jnp.where(row == 0, prev2, jnp.where(row == 1, prev1, pltpu.roll(u, 2, axis=0)))
    conv_y = u2 * cw_ref[0:1, :] + u1 * cw_ref[1:2, :] + u * cw_ref[2:3, :]
    ctail_ref[...] = u[tq - 8:, :]
    abuf_ref[...] = (cb * conv_y).astype(BF16)

    for h in range(MEM_HEADS):
        p = _softmax_rows(mem_s[h])
        mbuf_ref[:, mcols[h]] = _dot(p.astype(BF16), mv_ref[:, mcols[h]]).astype(BF16)

    for h in range(RET_HEADS):
        for r in chunks:
            msk_ref[r, kcols[h]] = (_dot_nt(q_ref[r, kcols[h]], k_ref[r, kcols[h]]) * dmask_ref[h]).astype(BF16)

    for h in range(RET_HEADS):
        for j, r in enumerate(chunks):
            kv_ref[h, j] = _dot_tn(kd_ref[r, kcols[h]], vb_ref[r, vcols[h]])

    for h in range(RET_HEADS):
        pg = proj(OFF_RG + h * RET_DV, RET_DV)
        s_cur = ret_ref[h]
        for j, r in enumerate(chunks):
            intra = _dot(msk_ref[r, kcols[h]], vb_ref[r, vcols[h]])
            o = intra + _dot(q_ref[r, kcols[h]], s_cur.astype(BF16)) * qdec_ref[h]
            gated = jax.nn.silu(pg[r]) * (_group_norm(o) * gn_ref[:, vcols[h]])
            bbuf_ref[r, vcols[h]] = gated.astype(BF16)
            s_cur = s_cur * cd_ref[h] + kv_ref[h, j]
        ret_ref[h] = s_cur

    _merge_out_ln(x_ref, (abuf_ref, bbuf_ref, mbuf_ref), lambda i: proj(OFF_GL + i * D_MODEL, D_MODEL), bg_ref,
                  (wco_ref, wro_ref, wmo_ref), wo_ref, lg_ref, lb_ref, y_ref)


def _resident(shape, layer):
    nd = len(shape)
    return pl.BlockSpec((None,) + tuple(shape), lambda *_: (layer,) + (0,) * nd,
                        pipeline_mode=pl.Buffered(1))


def _const(shape):
    nd = len(shape)
    return pl.BlockSpec(tuple(shape), lambda *_: (0,) * nd, pipeline_mode=pl.Buffered(1))


def _prompt_mixer(layer, x, tabs, mk_b, mv_b, w):
    bp, seq, _ = x.shape
    tq = PROMPT_TILE
    grid = (bp, seq // tq)
    in_specs = [
        pl.BlockSpec(memory_space=pltpu.SMEM),
        pl.BlockSpec((None, tq, D_MODEL), lambda b, t: (b, t, 0)),
        pl.BlockSpec((tq, RET_DK), lambda b, t: (t, 0)),
        pl.BlockSpec((tq, RET_DK), lambda b, t: (t, 0)),
        _const((RET_HEADS, RET_CHUNK, RET_CHUNK)),
        _const((RET_HEADS, RET_CHUNK, RET_DV)),
        _const((RET_HEADS, RET_CHUNK, RET_DK)),
        pl.BlockSpec((None, None, MEM_LEN, MEM_WIDTH), lambda b, t: (layer, b, 0, 0)),
        pl.BlockSpec((None, None, MEM_LEN, MEM_WIDTH), lambda b, t: (layer, b, 0, 0)),
        _resident((D_MODEL, IN_WIDTH), layer),
        _resident((1, N_BRANCH * D_MODEL), layer),
        _resident((CONV_K, CONV_WIDTH), layer),
        _resident((1, RET_V_WIDTH), layer),
        _resident((CONV_WIDTH, D_MODEL), layer),
        _resident((RET_V_WIDTH, D_MODEL), layer),
        _resident((MEM_WIDTH, D_MODEL), layer),
        _resident((D_MODEL, D_MODEL), layer),
        _resident((1, D_MODEL), layer),
        _resident((1, D_MODEL), layer),
    ]
    out_specs = [
        pl.BlockSpec((None, tq, D_MODEL), lambda b, t: (b, t, 0)),
        pl.BlockSpec((None, 8, CONV_WIDTH), lambda b, t: (b, 0, 0)),
        pl.BlockSpec((None, RET_HEADS, RET_DK, RET_DV), lambda b, t: (b, 0, 0, 0)),
    ]
    out_shape = [
        jax.ShapeDtypeStruct((bp, seq, D_MODEL), F32),
        jax.ShapeDtypeStruct((bp, 8, CONV_WIDTH), F32),
        jax.ShapeDtypeStruct((bp, RET_HEADS, RET_DK, RET_DV), F32),
    ]
    return pl.pallas_call(
        _prompt_mixer_kernel,
        grid=grid,
        in_specs=in_specs,
        out_specs=out_specs,
        out_shape=out_shape,
        scratch_shapes=[
            pltpu.VMEM((tq, D_MODEL), BF16),
            pltpu.VMEM((tq, RET_QK_WIDTH), BF16),
            pltpu.VMEM((tq, RET_QK_WIDTH), BF16),
            pltpu.VMEM((tq, RET_QK_WIDTH), BF16),
            pltpu.VMEM((tq, RET_V_WIDTH), BF16),
            pltpu.VMEM((tq, RET_QK_WIDTH), BF16),
            pltpu.VMEM((RET_HEADS, tq // RET_CHUNK, RET_DK, RET_DV), F32),
            pltpu.VMEM((tq, CONV_WIDTH), BF16),
            pltpu.VMEM((tq, RET_V_WIDTH), BF16),
            pltpu.VMEM((tq, MEM_WIDTH), BF16),
        ],
        compiler_params=pltpu.CompilerParams(
            dimension_semantics=("arbitrary", "arbitrary"), vmem_limit_bytes=VMEM_LIMIT_BYTES),
        name="prompt_mixer",
    )(tabs["cd"], x, tabs["cos"], tabs["sin"], tabs["dmask"], tabs["qdec"], tabs["kdec"], mk_b, mv_b,
      w["w_in"], w["b_gate"], w["conv_w"], w["ret_gn_g"], w["w_conv_out"], w["w_ret_out"], w["w_mem_out"],
      w["w_out"], w["ln1_g"], w["ln1_b"])


def _ffn_kernel(x_ref, wup_ref, wdn_ref, g_ref, b_ref, y_ref):
    half = x_ref.shape[0] // 2
    for rows in (slice(0, half), slice(half, 2 * half)):
        y_ref[rows, :] = _ffn_ln(x_ref[rows, :], wup_ref, wdn_ref, g_ref, b_ref)


def _ffn(layer, x, w):
    n = x.shape[0]
    tm = FFN_TILE
    return pl.pallas_call(
        _ffn_kernel,
        grid=(n // tm,),
        in_specs=[
            pl.BlockSpec((tm, D_MODEL), lambda i: (i, 0)),
            _resident((D_MODEL, D_FF), layer),
            _resident((D_FF, D_MODEL), layer),
            _resident((1, D_MODEL), layer),
            _resident((1, D_MODEL), layer),
        ],
        out_specs=pl.BlockSpec((tm, D_MODEL), lambda i: (i, 0)),
        out_shape=jax.ShapeDtypeStruct((n, D_MODEL), F32),
        compiler_params=pltpu.CompilerParams(
            dimension_semantics=("arbitrary",), vmem_limit_bytes=VMEM_LIMIT_BYTES),
        name="ffn",
    )(x, w["w_up"], w["w_down"], w["ln2_g"], w["ln2_b"])


def _sample_proj_kernel(x_ref, w_ref, o_ref):
    o_ref[...] = _dot(x_ref[...].astype(BF16), w_ref[...])


def _sample_proj(layer, x, w_in_b):
    n = x.shape[0]
    tn = 1024
    return pl.pallas_call(
        _sample_proj_kernel,
        grid=(IN_WIDTH // tn,),
        in_specs=[
            pl.BlockSpec((n, D_MODEL), lambda j: (0, 0)),
            pl.BlockSpec((None, D_MODEL, tn), lambda j: (layer, 0, j)),
        ],
        out_specs=pl.BlockSpec((n, tn), lambda j: (0, j)),
        out_shape=jax.ShapeDtypeStruct((n, IN_WIDTH), F32),
        compiler_params=pltpu.CompilerParams(
            dimension_semantics=("arbitrary",), vmem_limit_bytes=VMEM_LIMIT_BYTES),
        name="sample_proj",
    )(x, w_in_b)


def _sample_state_kernel(cd_ref, proj_ref, cst_ref, cos_ref, sin_ref, dmask_ref, qdec_ref, kdec_ref,
                         sret_ref, kc_ref, vc_ref, cw_ref, gn_ref,
                         a_ref, b_ref, m_ref, u_ref, nret_ref,
                         o_scr, p_scr, m_scr):
    ns = sret_ref.shape[0]
    t_len = proj_ref.shape[0] // ns

    cb = proj_ref[:, OFF_CB:OFF_CB + CONV_WIDTH]
    u = proj_ref[:, OFF_CC:OFF_CC + CONV_WIDTH] * proj_ref[:, OFF_CH:OFF_CH + CONV_WIDTH]
    tok = lax.broadcasted_iota(jnp.int32, u.shape, 0) & (t_len - 1)
    st2 = cst_ref[...]
    st1 = pltpu.roll(st2, st2.shape[0] - 1, axis=0)
    u1 = jnp.where(tok == 0, st1, pltpu.roll(u, 1, axis=0))
    u2 = jnp.where(tok < 2, st2, pltpu.roll(u, 2, axis=0))
    conv_y = u2 * cw_ref[0:1, :] + u1 * cw_ref[1:2, :] + u * cw_ref[2:3, :]
    u_ref[...] = u
    a_ref[...] = (cb * conv_y).astype(BF16)

    rows = [slice(s * t_len, (s + 1) * t_len) for s in range(ns)]
    kcols = [slice(h * RET_DK, (h + 1) * RET_DK) for h in range(RET_HEADS)]
    vcols = [slice(h * RET_DV, (h + 1) * RET_DV) for h in range(RET_HEADS)]
    mcols = [slice(h * MEM_HEAD_DIM, (h + 1) * MEM_HEAD_DIM) for h in range(MEM_HEADS)]
    pcols = [slice(h * MEM_LEN, (h + 1) * MEM_LEN) for h in range(MEM_HEADS)]
    head_rows = [pl.ds(h, MEM_LEN, stride=MEM_HEADS) for h in range(MEM_HEADS)]

    cos = cos_ref[...]
    sin = sin_ref[...]
    q, kd = [], []
    for h in range(RET_HEADS):
        qh = _rotary(proj_ref[:, OFF_RQ + h * RET_DK:OFF_RQ + (h + 1) * RET_DK], cos, sin)
        kh = _rotary(proj_ref[:, OFF_RK + h * RET_DK:OFF_RK + (h + 1) * RET_DK], cos, sin) * (RET_DK ** -0.5)
        q.append(qh)
        kd.append(kh * kdec_ref[h])
        vb = proj_ref[:, OFF_RV + h * RET_DV:OFF_RV + (h + 1) * RET_DV].astype(BF16)
        masked = (_dot_nt(qh.astype(BF16), kh.astype(BF16)) * dmask_ref[h]).astype(BF16)
        o_scr[:, vcols[h]] = _dot(masked, vb)

    for s in range(ns):
        for h in range(RET_HEADS):
            s_prev = sret_ref[s, h]
            inter = _dot(q[h][rows[s]].astype(BF16), s_prev.astype(BF16))
            o_scr[rows[s], vcols[h]] += inter * qdec_ref[h]
            vb = proj_ref[rows[s], OFF_RV + h * RET_DV:OFF_RV + (h + 1) * RET_DV].astype(BF16)
            nret_ref[s, h] = s_prev * cd_ref[h] + _dot_tn(kd[h][rows[s]].astype(BF16), vb)

    for s in range(ns):
        for h in range(MEM_HEADS):
            mq = proj_ref[rows[s], OFF_MQ + h * MEM_HEAD_DIM:OFF_MQ + (h + 1) * MEM_HEAD_DIM].astype(BF16)
            p_scr[rows[s], pcols[h]] = _dot_nt(mq, kc_ref[s, head_rows[h], :].astype(BF16)) * (MEM_HEAD_DIM ** -0.5)
    for h in range(MEM_HEADS):
        p_scr[:, pcols[h]] = _softmax_rows(p_scr[:, pcols[h]])
    for s in range(ns):
        for h in range(MEM_HEADS):
            p = p_scr[rows[s], pcols[h]].astype(BF16)
            m_scr[rows[s], mcols[h]] = _dot(p, vc_ref[s, head_rows[h], :].astype(BF16))
    m_ref[...] = m_scr[...].astype(BF16)

    for h in range(RET_HEADS):
        rg = proj_ref[:, OFF_RG + h * RET_DV:OFF_RG + (h + 1) * RET_DV]
        gated = jax.nn.silu(rg) * (_group_norm(o_scr[:, vcols[h]]) * gn_ref[:, vcols[h]])
        b_ref[:, vcols[h]] = gated.astype(BF16)


def _sample_state_kernel_into(stack_ref, *refs):
    del stack_ref
    _sample_state_kernel(*refs)


def _sample_state(layer, proj, cst, tabs, state_ret, kc, vc, w, ret_stack):
    n = proj.shape[0]
    nseq = state_ret.shape[1]
    t_len = n // nseq
    ns = SAMPLE_SEQS
    rows = ns * t_len
    grid = (nseq // ns,)
    tok_spec = lambda width: pl.BlockSpec((rows, width), lambda i: (i, 0))
    in_specs = [
        pl.BlockSpec(memory_space=pltpu.SMEM),
        tok_spec(OFF_GL),
        tok_spec(CONV_WIDTH),
        _const((rows, RET_DK)),
        _const((rows, RET_DK)),
        _const((RET_HEADS, rows, rows)),
        _const((RET_HEADS, t_len, RET_DV)),
        _const((RET_HEADS, rows, RET_DK)),
        pl.BlockSpec((None, ns, RET_HEADS, RET_DK, RET_DV), lambda i: (layer, i, 0, 0, 0)),
        pl.BlockSpec((None, ns, MEM_LEN * MEM_HEADS, MEM_HEAD_DIM), lambda i: (layer, i, 0, 0)),
        pl.BlockSpec((None, ns, MEM_LEN * MEM_HEADS, MEM_HEAD_DIM), lambda i: (layer, i, 0, 0)),
        _resident((CONV_K, CONV_WIDTH), layer),
        _resident((1, RET_V_WIDTH), layer),
    ]
    out_specs = [
        tok_spec(CONV_WIDTH),
        tok_spec(RET_V_WIDTH),
        tok_spec(MEM_WIDTH),
        tok_spec(CONV_WIDTH),
        pl.BlockSpec((None, ns, RET_HEADS, RET_DK, RET_DV), lambda i: (layer, i, 0, 0, 0)),
    ]
    out_shape = [
        jax.ShapeDtypeStruct((n, CONV_WIDTH), BF16),
        jax.ShapeDtypeStruct((n, RET_V_WIDTH), BF16),
        jax.ShapeDtypeStruct((n, MEM_WIDTH), BF16),
        jax.ShapeDtypeStruct((n, CONV_WIDTH), F32),
        jax.ShapeDtypeStruct((DEPTH, nseq, RET_HEADS, RET_DK, RET_DV), F32),
    ]
    args = (tabs["cd"], proj, cst, tabs["cos"], tabs["sin"], tabs["dmask"], tabs["qdec"], tabs["kdec"],
            state_ret, kc, vc, w["conv_w"], w["ret_gn_g"])
    body, aliases = _sample_state_kernel, {}
    if ret_stack is not None:
        body, aliases = _sample_state_kernel_into, {0: len(out_shape) - 1}
        in_specs = [pl.BlockSpec(memory_space=pl.ANY)] + in_specs
        args = (ret_stack,) + args
    scratch = [
        pltpu.VMEM((rows, RET_V_WIDTH), F32),
        pltpu.VMEM((rows, MEM_HEADS * MEM_LEN), F32),
        pltpu.VMEM((rows, MEM_WIDTH), F32),
    ]
    return pl.pallas_call(
        body,
        grid=grid,
        in_specs=in_specs,
        out_specs=out_specs,
        out_shape=out_shape,
        scratch_shapes=scratch,
        input_output_aliases=aliases,
        compiler_params=pltpu.CompilerParams(
            dimension_semantics=("arbitrary",), vmem_limit_bytes=VMEM_LIMIT_BYTES),
        name="sample_state",
    )(*args)


def _sample_post_kernel(x_ref, a_ref, b_ref, m_ref, g0_ref, g1_ref, g2_ref,
                        bg_ref, wco_ref, wro_ref, wmo_ref, wo_ref, l1g_ref, l1b_ref,
                        wup_ref, wdn_ref, l2g_ref, l2b_ref, y_ref, x1_ref):
    gate_refs = (g0_ref, g1_ref, g2_ref)
    _merge_out_ln(x_ref, (a_ref, b_ref, m_ref), lambda i: gate_refs[i][...], bg_ref,
                  (wco_ref, wro_ref, wmo_ref), wo_ref, l1g_ref, l1b_ref, x1_ref)
    y_ref[...] = _ffn_ln(x1_ref[...], wup_ref, wdn_ref, l2g_ref, l2b_ref)


def _sample_post(layer, x, a_pre, b_pre, m_pre, proj, w):
    n = x.shape[0]
    tm = SAMPLE_POST_TILE
    tok = lambda width: pl.BlockSpec((tm, width), lambda i: (i, 0))
    gate = lambda k: pl.BlockSpec((tm, D_MODEL), lambda i: (i, OFF_GL // D_MODEL + k))
    return pl.pallas_call(
        _sample_post_kernel,
        grid=(n // tm,),
        in_specs=[
            tok(D_MODEL), tok(CONV_WIDTH), tok(RET_V_WIDTH), tok(MEM_WIDTH), gate(0), gate(1), gate(2),
            _resident((1, N_BRANCH * D_MODEL), layer),
            _resident((CONV_WIDTH, D_MODEL), layer),
            _resident((RET_V_WIDTH, D_MODEL), layer),
            _resident((MEM_WIDTH, D_MODEL), layer),
            _resident((D_MODEL, D_MODEL), layer),
            _resident((1, D_MODEL), layer),
            _resident((1, D_MODEL), layer),
            _resident((D_MODEL, D_FF), layer),
            _resident((D_FF, D_MODEL), layer),
            _resident((1, D_MODEL), layer),
            _resident((1, D_MODEL), layer),
        ],
        out_specs=tok(D_MODEL),
        out_shape=jax.ShapeDtypeStruct((n, D_MODEL), F32),
        scratch_shapes=[pltpu.VMEM((tm, D_MODEL), F32)],
        compiler_params=pltpu.CompilerParams(
            dimension_semantics=("arbitrary",), vmem_limit_bytes=VMEM_LIMIT_BYTES),
        name="sample_post",
    )(x, a_pre, b_pre, m_pre, proj, proj, proj,
      w["b_gate"], w["w_conv_out"], w["w_ret_out"], w["w_mem_out"], w["w_out"], w["ln1_g"], w["ln1_b"],
      w["w_up"], w["w_down"], w["ln2_g"], w["ln2_b"])


def _rotary_tables(pos, reps):
    half = RET_DK // 2
    inv = ROPE_BASE ** (-jnp.arange(half, dtype=F32) / half)
    ang = pos.astype(F32)[:, None] * inv[None, :]
    cos = jnp.cos(ang)
    sin = jnp.sin(ang)
    cos_full = jnp.concatenate([cos, cos], axis=-1)
    sin_signed = jnp.concatenate([-sin, sin], axis=-1)
    return jnp.tile(cos_full, (reps, 1)), jnp.tile(sin_signed, (reps, 1))


def _decay_tables(c):
    h = RET_HEADS
    log_g = jnp.log1p(-jnp.exp2(-5.0 - jnp.arange(h, dtype=F32)))
    idx = jnp.arange(c, dtype=F32)
    rel = idx[:, None] - idx[None, :]
    causal = rel >= 0
    dmask = jnp.where(causal[None], jnp.exp(jnp.where(causal, rel, 0.0)[None] * log_g[:, None, None]), 0.0)
    q_decay = jnp.exp((idx + 1.0)[None, :] * log_g[:, None])
    k_decay = jnp.exp((c - 1.0 - idx)[None, :] * log_g[:, None])
    chunk_decay = jnp.exp(c * log_g)
    return {
        "dmask": dmask,
        "qdec": jnp.broadcast_to(q_decay[:, :, None], (h, c, RET_DV)),
        "kdec": jnp.broadcast_to(k_decay[:, :, None], (h, c, RET_DK)),
        "cd": chunk_decay,
    }


def kernel(x_prompt, x_sample, cache_mem_k, cache_mem_v, state_conv, state_ret, mem_prompt,
           w_in, b_gate, conv_w, ret_gn_g, w_conv_out, w_ret_out, w_mem_out, w_out, w_mem_kv,
           ln1_g, ln1_b, w_up, w_down, ln2_g, ln2_b):
    bp, seq, _ = x_prompt.shape
    nseq, t_len, _ = x_sample.shape
    assert seq % PROMPT_TILE == 0 and PROMPT_TILE % RET_CHUNK == 0
    assert t_len == 8 and nseq % SAMPLE_SEQS == 0 and t_len % RET_CHUNK != 0

    row = lambda a: a.reshape(DEPTH, 1, a.shape[-1])
    w = {
        "w_in": w_in.astype(BF16), "b_gate": row(b_gate), "conv_w": conv_w, "ret_gn_g": row(ret_gn_g),
        "w_conv_out": w_conv_out.astype(BF16), "w_ret_out": w_ret_out.astype(BF16),
        "w_mem_out": w_mem_out.astype(BF16), "w_out": w_out.astype(BF16),
        "ln1_g": row(ln1_g), "ln1_b": row(ln1_b),
        "w_up": w_up.astype(BF16), "w_down": w_down.astype(BF16),
        "ln2_g": row(ln2_g), "ln2_b": row(ln2_b),
    }

    tabs_p = _decay_tables(RET_CHUNK)
    tabs_p["cos"], tabs_p["sin"] = _rotary_tables(jnp.arange(seq, dtype=jnp.int32), 1)
    tabs_s = _decay_tables(t_len)
    eye = jnp.eye(SAMPLE_SEQS, dtype=F32)
    tabs_s["dmask"] = (eye[None, :, None, :, None] * tabs_s["dmask"][:, None, :, None, :]).reshape(
        RET_HEADS, SAMPLE_SEQS * t_len, SAMPLE_SEQS * t_len)
    tabs_s["kdec"] = jnp.tile(tabs_s["kdec"], (1, SAMPLE_SEQS, 1))
    tabs_s["cos"], tabs_s["sin"] = _rotary_tables(PAST_LEN + jnp.arange(t_len, dtype=jnp.int32), SAMPLE_SEQS)

    mk_f, mv_f, mk_b, mv_b = _mem_kv(mem_prompt, w_mem_kv.astype(BF16))
    mk_b = mk_b.reshape(DEPTH, bp, MEM_LEN, MEM_WIDTH)
    mv_b = mv_b.reshape(DEPTH, bp, MEM_LEN, MEM_WIDTH)

    kc = cache_mem_k.reshape(DEPTH, nseq, MEM_LEN * MEM_HEADS, MEM_HEAD_DIM)
    vc = cache_mem_v.reshape(DEPTH, nseq, MEM_LEN * MEM_HEADS, MEM_HEAD_DIM)
    cst = jnp.concatenate(
        [state_conv, jnp.zeros((DEPTH, nseq, t_len - (CONV_K - 1), CONV_WIDTH), state_conv.dtype)], axis=2
    ).reshape(DEPTH, nseq * t_len, CONV_WIDTH)

    xp = x_prompt
    xs = x_sample.reshape(nseq * t_len, D_MODEL)
    cp_list, rp_list, cs_list = [], [], []
    ret_stack = None
    for l in range(DEPTH):
        xp1, ctail_p, ret_p = _prompt_mixer(l, xp, tabs_p, mk_b, mv_b, w)
        xp = _ffn(l, xp1.reshape(bp * seq, D_MODEL), w).reshape(bp, seq, D_MODEL)
        cp_list.append(ctail_p[:, 8 - (CONV_K - 1):, :])
        rp_list.append(ret_p)

        proj = _sample_proj(l, xs, w["w_in"])
        a_pre, b_pre, m_pre, u_s, ret_stack = _sample_state(l, proj, cst[l], tabs_s, state_ret, kc, vc, w, ret_stack)
        xs = _sample_post(l, xs, a_pre, b_pre, m_pre, proj, w)
        cs_list.append(u_s.reshape(nseq, t_len, CONV_WIDTH)[:, t_len - (CONV_K - 1):, :])

    mem_shape = (DEPTH, bp, MEM_LEN, MEM_HEADS, MEM_HEAD_DIM)
    return (xp, xs.reshape(nseq, t_len, D_MODEL), mk_f.reshape(mem_shape), mv_f.reshape(mem_shape),
            jnp.stack(cp_list), jnp.stack(rp_list), jnp.stack(cs_list), ret_stack)
```

```python
import functools

import jax
import jax.numpy as jnp
from jax import lax
from jax.experimental import pallas as pl
from jax.experimental.pallas import tpu as pltpu

D_MODEL = 1024
DEPTH = 4
CONV_WIDTH = 512
CONV_K = 3
RET_HEADS = 4
RET_DK = 128
RET_DV = 256
RET_QK_WIDTH = RET_HEADS * RET_DK
RET_V_WIDTH = RET_HEADS * RET_DV
RET_CHUNK = 128
MEM_LEN = 256
MEM_HEADS = 4
MEM_HEAD_DIM = 128
MEM_WIDTH = MEM_HEADS * MEM_HEAD_DIM
N_BRANCH = 3
D_FF = 4 * D_MODEL
ROPE_BASE = 10000.0
LN_EPS = 1e-5
GN_EPS = 1e-6
PAST_LEN = 16384
DEEPNORM_ALPHA = (2 * DEPTH) ** 0.25

OFF_CB = 0
OFF_CC = OFF_CB + CONV_WIDTH
OFF_CH = OFF_CC + CONV_WIDTH
OFF_RQ = OFF_CH + CONV_WIDTH
OFF_RK = OFF_RQ + RET_QK_WIDTH
OFF_RV = OFF_RK + RET_QK_WIDTH
OFF_RG = OFF_RV + RET_V_WIDTH
OFF_MQ = OFF_RG + RET_V_WIDTH
OFF_GL = OFF_MQ + MEM_WIDTH
IN_WIDTH = OFF_GL + N_BRANCH * D_MODEL

V7X_VMEM_BYTES = 64 * 1024 * 1024
VMEM_LIMIT_BYTES = V7X_VMEM_BYTES - 6 * 1024 * 1024

PROMPT_TILE = 512
FFN_TILE = 1024
FFN_CHUNK = 1024
SAMPLE_SEQS = 8
SAMPLE_POST_TILE = 512

BF16 = jnp.bfloat16
F32 = jnp.float32


def _dot(a, b):
    return jnp.dot(a, b, preferred_element_type=F32)


def _dot_nt(a, b):
    return lax.dot_general(a, b, (((1,), (1,)), ((), ())), preferred_element_type=F32)


def _dot_tn(a, b):
    return lax.dot_general(a, b, (((0,), (0,)), ((), ())), preferred_element_type=F32)


def _layer_norm(z, g, b):
    mu = jnp.mean(z, axis=-1, keepdims=True)
    zc = z - mu
    var = jnp.mean(zc * zc, axis=-1, keepdims=True)
    return zc * lax.rsqrt(var + LN_EPS) * g + b


def _group_norm(o):
    mu = jnp.mean(o, axis=-1, keepdims=True)
    oc = o - mu
    var = jnp.mean(oc * oc, axis=-1, keepdims=True)
    return oc * lax.rsqrt(var + GN_EPS)


def _rotary(xh, cos, sin_signed):
    return xh * cos + pltpu.roll(xh, RET_DK // 2, axis=1) * sin_signed


def _softmax_rows(s):
    m = jnp.max(s, axis=-1, keepdims=True)
    e = jnp.exp(s - m)
    return e / jnp.sum(e, axis=-1, keepdims=True)


def _merge_out_ln(x_ref, pre_refs, gate_logits, bg_ref, w_refs, wo_ref, g_ref, b_ref, y_ref):
    merged = None
    for i, (pre_ref, w_ref) in enumerate(zip(pre_refs, w_refs)):
        gate = jax.nn.sigmoid(gate_logits(i) + bg_ref[:, i * D_MODEL:(i + 1) * D_MODEL])
        term = gate * _dot(pre_ref[...], w_ref[...])
        merged = term if merged is None else merged + term
    merged = merged.astype(BF16)
    half = merged.shape[0] // 2
    for rows in (slice(0, half), slice(half, 2 * half)):
        z = DEEPNORM_ALPHA * x_ref[rows, :] + _dot(merged[rows], wo_ref[...])
        y_ref[rows, :] = _layer_norm(z, g_ref[...], b_ref[...])


def _ffn_ln(x, wup_ref, wdn_ref, g_ref, b_ref):
    xb = x.astype(BF16)
    acc = None
    for c in range(D_FF // FFN_CHUNK):
        cols = slice(c * FFN_CHUNK, (c + 1) * FFN_CHUNK)
        h = jnp.maximum(_dot(xb, wup_ref[:, cols]), 0.0)
        part = _dot((h * h).astype(BF16), wdn_ref[cols, :])
        acc = part if acc is None else acc + part
    return _layer_norm(DEEPNORM_ALPHA * x + acc, g_ref[...], b_ref[...])


MEM_KV_ROWS = 1024


def _mem_kv_kernel(mem_ref, w_ref, k_ref, v_ref, kb_ref, vb_ref):
    for r0 in range(0, mem_ref.shape[0], MEM_KV_ROWS):
        rows = slice(r0, r0 + MEM_KV_ROWS)
        kv = _dot(mem_ref[rows, :].astype(BF16), w_ref[...])
        kb_ref[rows, :] = kv[:, :MEM_WIDTH].astype(BF16)
        vb_ref[rows, :] = kv[:, MEM_WIDTH:].astype(BF16)
        for h in range(MEM_HEADS):
            head_rows = pl.ds(r0 * MEM_HEADS + h, MEM_KV_ROWS, stride=MEM_HEADS)
            k_ref[head_rows, :] = kv[:, h * MEM_HEAD_DIM:(h + 1) * MEM_HEAD_DIM]
            v_ref[head_rows, :] = kv[:, MEM_WIDTH + h * MEM_HEAD_DIM:MEM_WIDTH + (h + 1) * MEM_HEAD_DIM]


def _mem_kv(mem_prompt, w_mem_kv_b):
    bp = mem_prompt.shape[0]
    n = bp * MEM_LEN
    assert n % MEM_KV_ROWS == 0
    out_f = jax.ShapeDtypeStruct((DEPTH, n * MEM_HEADS, MEM_HEAD_DIM), F32)
    out_b = jax.ShapeDtypeStruct((DEPTH, n, MEM_WIDTH), BF16)
    f_spec = pl.BlockSpec((None, n * MEM_HEADS, MEM_HEAD_DIM), lambda l: (l, 0, 0))
    b_spec = pl.BlockSpec((None, n, MEM_WIDTH), lambda l: (l, 0, 0))
    return pl.pallas_call(
        _mem_kv_kernel,
        grid=(DEPTH,),
        in_specs=[
            _const((n, D_MODEL)),
            pl.BlockSpec((None, D_MODEL, 2 * MEM_WIDTH), lambda l: (l, 0, 0)),
        ],
        out_specs=[f_spec, f_spec, b_spec, b_spec],
        out_shape=[out_f, out_f, out_b, out_b],
        compiler_params=pltpu.CompilerParams(
            dimension_semantics=("arbitrary",), vmem_limit_bytes=VMEM_LIMIT_BYTES),
        name="mem_kv",
    )(mem_prompt.reshape(n, D_MODEL), w_mem_kv_b)


MIXER_INPUTS = 19
MIXER_OUTPUTS = 3


def _prompt_mixer_kernel(n_cast, *refs):
    (cd_ref, x_ref, cos_ref, sin_ref, dmask_ref, qdec_ref, kdec_ref, mk_ref, mv_ref,
     win_ref, bg_ref, cw_ref, gn_ref, wco_ref, wro_ref, wmo_ref, wo_ref, lg_ref, lb_ref) = refs[:MIXER_INPUTS]
    cast_src = refs[MIXER_INPUTS:MIXER_INPUTS + n_cast]
    outs = refs[MIXER_INPUTS + n_cast:]
    y_ref, ctail_ref, ret_ref = outs[:MIXER_OUTPUTS]
    cast_dst = outs[MIXER_OUTPUTS:MIXER_OUTPUTS + n_cast]
    (xb_ref, q_ref, k_ref, kd_ref, vb_ref, msk_ref, kv_ref,
     abuf_ref, bbuf_ref, mbuf_ref) = outs[MIXER_OUTPUTS + n_cast:]

    for src_ref, dst_ref in zip(cast_src, cast_dst):
        dst_ref[...] = src_ref[...].astype(BF16)

    t = pl.program_id(1)
    tq = x_ref.shape[0]
    chunks = [slice(j * RET_CHUNK, (j + 1) * RET_CHUNK) for j in range(tq // RET_CHUNK)]
    kcols = [slice(h * RET_DK, (h + 1) * RET_DK) for h in range(RET_HEADS)]
    vcols = [slice(h * RET_DV, (h + 1) * RET_DV) for h in range(RET_HEADS)]
    mcols = [slice(h * MEM_HEAD_DIM, (h + 1) * MEM_HEAD_DIM) for h in range(MEM_HEADS)]

    @pl.when(t == 0)
    def _():
        ret_ref[...] = jnp.zeros_like(ret_ref)
        ctail_ref[...] = jnp.zeros_like(ctail_ref)

    xb_ref[...] = x_ref[...].astype(BF16)

    def proj(lo, width):
        return _dot(xb_ref[...], win_ref[:, lo:lo + width])

    pq = proj(OFF_RQ, RET_QK_WIDTH)
    pk = proj(OFF_RK, RET_QK_WIDTH)
    vb_ref[...] = proj(OFF_RV, RET_V_WIDTH).astype(BF16)
    cos = cos_ref[...]
    sin = sin_ref[...]
    for h in range(RET_HEADS):
        q_ref[:, kcols[h]] = _rotary(pq[:, kcols[h]], cos, sin).astype(BF16)
        kh = _rotary(pk[:, kcols[h]], cos, sin) * (RET_DK ** -0.5)
        k_ref[:, kcols[h]] = kh.astype(BF16)
        for r in chunks:
            kd_ref[r, kcols[h]] = (kh[r] * kdec_ref[h]).astype(BF16)

    pm = proj(OFF_MQ, MEM_WIDTH).astype(BF16)
    mem_s = [_dot_nt(pm[:, mcols[h]], mk_ref[:, mcols[h]]) * (MEM_HEAD_DIM ** -0.5) for h in range(MEM_HEADS)]

    cb = proj(OFF_CB, CONV_WIDTH)
    u = proj(OFF_CC, CONV_WIDTH) * proj(OFF_CH, CONV_WIDTH)
    row = lax.broadcasted_iota(jnp.int32, u.shape, 0)
    prev1 = ctail_ref[7:8, :]
    prev2 = ctail_ref[6:7, :]
    u1 = jnp.where(row == 0, prev1, pltpu.roll(u, 1, axis=0))
    u2 = jnp.where(row == 0, prev2, jnp.where(row == 1, prev1, pltpu.roll(u, 2, axis=0)))
    conv_y = u2 * cw_ref[0:1, :] + u1 * cw_ref[1:2, :] + u * cw_ref[2:3, :]
    ctail_ref[...] = u[tq - 8:, :]
    abuf_ref[...] = (cb * conv_y).astype(BF16)

    for h in range(MEM_HEADS):
        p = _softmax_rows(mem_s[h])
        mbuf_ref[:, mcols[h]] = _dot(p.astype(BF16), mv_ref[:, mcols[h]]).astype(BF16)

    for h in range(RET_HEADS):
        for r in chunks:
            msk_ref[r, kcols[h]] = (_dot_nt(q_ref[r, kcols[h]], k_ref[r, kcols[h]]) * dmask_ref[h]).astype(BF16)

    for h in range(RET_HEADS):
        for j, r in enumerate(chunks):
            kv_ref[h, j] = _dot_tn(kd_ref[r, kcols[h]], vb_ref[r, vcols[h]])

    for h in range(RET_HEADS):
        pg = proj(OFF_RG + h * RET_DV, RET_DV)
        s_cur = ret_ref[h]
        for j, r in enumerate(chunks):
            intra = _dot(msk_ref[r, kcols[h]], vb_ref[r, vcols[h]])
            o = intra + _dot(q_ref[r, kcols[h]], s_cur.astype(BF16)) * qdec_ref[h]
            gated = jax.nn.silu(pg[r]) * (_group_norm(o) * gn_ref[:, vcols[h]])
            bbuf_ref[r, vcols[h]] = gated.astype(BF16)
            s_cur = s_cur * cd_ref[h] + kv_ref[h, j]
        ret_ref[h] = s_cur

    _merge_out_ln(x_ref, (abuf_ref, bbuf_ref, mbuf_ref), lambda i: proj(OFF_GL + i * D_MODEL, D_MODEL), bg_ref,
                  (wco_ref, wro_ref, wmo_ref), wo_ref, lg_ref, lb_ref, y_ref)


def _const(shape):
    nd = len(shape)
    return pl.BlockSpec(tuple(shape), lambda *_: (0,) * nd, pipeline_mode=pl.Buffered(1))


def _prompt_mixer(layer, x, tabs, mk_b, mv_b, w, to_cast):
    bp, seq, _ = x.shape
    tq = PROMPT_TILE
    grid = (bp, seq // tq)
    steps = bp * (seq // tq)
    steps_per_seq = seq // tq
    cast_in_specs, cast_out_specs, cast_out_shapes = [], [], []
    for a in to_cast:
        _, rows, cols = a.shape
        slab = rows // steps
        assert slab * steps == rows and slab % 16 == 0
        cast_in_specs.append(pl.BlockSpec((None, slab, cols), lambda b, t: (layer + 1, b * steps_per_seq + t, 0)))
        cast_out_specs.append(pl.BlockSpec((slab, cols), lambda b, t: (b * steps_per_seq + t, 0)))
        cast_out_shapes.append(jax.ShapeDtypeStruct((rows, cols), BF16))
    in_specs = [
        pl.BlockSpec(memory_space=pltpu.SMEM),
        pl.BlockSpec((None, tq, D_MODEL), lambda b, t: (b, t, 0)),
        pl.BlockSpec((tq, RET_DK), lambda b, t: (t, 0)),
        pl.BlockSpec((tq, RET_DK), lambda b, t: (t, 0)),
        _const((RET_HEADS, RET_CHUNK, RET_CHUNK)),
        _const((RET_HEADS, RET_CHUNK, RET_DV)),
        _const((RET_HEADS, RET_CHUNK, RET_DK)),
        pl.BlockSpec((None, None, MEM_LEN, MEM_WIDTH), lambda b, t: (layer, b, 0, 0)),
        pl.BlockSpec((None, None, MEM_LEN, MEM_WIDTH), lambda b, t: (layer, b, 0, 0)),
        _const((D_MODEL, IN_WIDTH)),
        _const((1, N_BRANCH * D_MODEL)),
        _const((CONV_K, CONV_WIDTH)),
        _const((1, RET_V_WIDTH)),
        _const((CONV_WIDTH, D_MODEL)),
        _const((RET_V_WIDTH, D_MODEL)),
        _const((MEM_WIDTH, D_MODEL)),
        _const((D_MODEL, D_MODEL)),
        _const((1, D_MODEL)),
        _const((1, D_MODEL)),
    ]
    out_specs = [
        pl.BlockSpec((None, tq, D_MODEL), lambda b, t: (b, t, 0)),
        pl.BlockSpec((None, 8, CONV_WIDTH), lambda b, t: (b, 0, 0)),
        pl.BlockSpec((None, RET_HEADS, RET_DK, RET_DV), lambda b, t: (b, 0, 0, 0)),
    ]
    out_shape = [
        jax.ShapeDtypeStruct((bp, seq, D_MODEL), F32),
        jax.ShapeDtypeStruct((bp, 8, CONV_WIDTH), F32),
        jax.ShapeDtypeStruct((bp, RET_HEADS, RET_DK, RET_DV), F32),
    ]
    return pl.pallas_call(
        functools.partial(_prompt_mixer_kernel, len(to_cast)),
        grid=grid,
        in_specs=in_specs + cast_in_specs,
        out_specs=out_specs + cast_out_specs,
        out_shape=out_shape + cast_out_shapes,
        scratch_shapes=[
            pltpu.VMEM((tq, D_MODEL), BF16),
            pltpu.VMEM((tq, RET_QK_WIDTH), BF16),
            pltpu.VMEM((tq, RET_QK_WIDTH), BF16),
            pltpu.VMEM((tq, RET_QK_WIDTH), BF16),
            pltpu.VMEM((tq, RET_V_WIDTH), BF16),
            pltpu.VMEM((tq, RET_QK_WIDTH), BF16),
            pltpu.VMEM((RET_HEADS, tq // RET_CHUNK, RET_DK, RET_DV), F32),
            pltpu.VMEM((tq, CONV_WIDTH), BF16),
            pltpu.VMEM((tq, RET_V_WIDTH), BF16),
            pltpu.VMEM((tq, MEM_WIDTH), BF16),
        ],
        compiler_params=pltpu.CompilerParams(
            dimension_semantics=("arbitrary", "arbitrary"), vmem_limit_bytes=VMEM_LIMIT_BYTES),
        name="prompt_mixer",
    )(tabs["cd"], x, tabs["cos"], tabs["sin"], tabs["dmask"], tabs["qdec"], tabs["kdec"], mk_b, mv_b,
      w["w_in"], w["b_gate"], w["conv_w"], w["ret_gn_g"], w["w_conv_out"], w["w_ret_out"], w["w_mem_out"],
      w["w_out"], w["ln1_g"], w["ln1_b"], *to_cast)


def _ffn_kernel(x_ref, wup_ref, wdn_ref, g_ref, b_ref, y_ref):
    half = x_ref.shape[0] // 2
    for rows in (slice(0, half), slice(half, 2 * half)):
        y_ref[rows, :] = _ffn_ln(x_ref[rows, :], wup_ref, wdn_ref, g_ref, b_ref)


def _ffn(x, w):
    n = x.shape[0]
    tm = FFN_TILE
    return pl.pallas_call(
        _ffn_kernel,
        grid=(n // tm,),
        in_specs=[
            pl.BlockSpec((tm, D_MODEL), lambda i: (i, 0)),
            _const((D_MODEL, D_FF)),
            _const((D_FF, D_MODEL)),
            _const((1, D_MODEL)),
            _const((1, D_MODEL)),
        ],
        out_specs=pl.BlockSpec((tm, D_MODEL), lambda i: (i, 0)),
        out_shape=jax.ShapeDtypeStruct((n, D_MODEL), F32),
        compiler_params=pltpu.CompilerParams(
            dimension_semantics=("arbitrary",), vmem_limit_bytes=VMEM_LIMIT_BYTES),
        name="ffn",
    )(x, w["w_up"], w["w_down"], w["ln2_g"], w["ln2_b"])


def _sample_proj_kernel(x_ref, w_ref, o_ref):
    o_ref[...] = _dot(x_ref[...].astype(BF16), w_ref[...])


def _sample_proj(x, w_in_b):
    n = x.shape[0]
    tn = 1024
    return pl.pallas_call(
        _sample_proj_kernel,
        grid=(IN_WIDTH // tn,),
        in_specs=[
            pl.BlockSpec((n, D_MODEL), lambda j: (0, 0)),
            pl.BlockSpec((D_MODEL, tn), lambda j: (0, j)),
        ],
        out_specs=pl.BlockSpec((n, tn), lambda j: (0, j)),
        out_shape=jax.ShapeDtypeStruct((n, IN_WIDTH), F32),
        compiler_params=pltpu.CompilerParams(
            dimension_semantics=("arbitrary",), vmem_limit_bytes=VMEM_LIMIT_BYTES),
        name="sample_proj",
    )(x, w_in_b)


def _sample_state_kernel(cd_ref, proj_ref, cst_ref, cos_ref, sin_ref, dmask_ref, qdec_ref, kdec_ref,
                         sret_ref, kc_ref, vc_ref, cw_ref, gn_ref,
                         a_ref, b_ref, m_ref, u_ref, nret_ref,
                         o_scr, p_scr, m_scr):
    ns = sret_ref.shape[0]
    t_len = proj_ref.shape[0] // ns

    cb = proj_ref[:, OFF_CB:OFF_CB + CONV_WIDTH]
    u = proj_ref[:, OFF_CC:OFF_CC + CONV_WIDTH] * proj_ref[:, OFF_CH:OFF_CH + CONV_WIDTH]
    tok = lax.broadcasted_iota(jnp.int32, u.shape, 0) & (t_len - 1)
    st2 = cst_ref[...]
    st1 = pltpu.roll(st2, st2.shape[0] - 1, axis=0)
    u1 = jnp.where(tok == 0, st1, pltpu.roll(u, 1, axis=0))
    u2 = jnp.where(tok < 2, st2, pltpu.roll(u, 2, axis=0))
    conv_y = u2 * cw_ref[0:1, :] + u1 * cw_ref[1:2, :] + u * cw_ref[2:3, :]
    u_ref[...] = u
    a_ref[...] = (cb * conv_y).astype(BF16)

    rows = [slice(s * t_len, (s + 1) * t_len) for s in range(ns)]
    kcols = [slice(h * RET_DK, (h + 1) * RET_DK) for h in range(RET_HEADS)]
    vcols = [slice(h * RET_DV, (h + 1) * RET_DV) for h in range(RET_HEADS)]
    mcols = [slice(h * MEM_HEAD_DIM, (h + 1) * MEM_HEAD_DIM) for h in range(MEM_HEADS)]
    pcols = [slice(h * MEM_LEN, (h + 1) * MEM_LEN) for h in range(MEM_HEADS)]
    head_rows = [pl.ds(h, MEM_LEN, stride=MEM_HEADS) for h in range(MEM_HEADS)]

    cos = cos_ref[...]
    sin = sin_ref[...]
    q, kd = [], []
    for h in range(RET_HEADS):
        qh = _rotary(proj_ref[:, OFF_RQ + h * RET_DK:OFF_RQ + (h + 1) * RET_DK], cos, sin)
        kh = _rotary(proj_ref[:, OFF_RK + h * RET_DK:OFF_RK + (h + 1) * RET_DK], cos, sin) * (RET_DK ** -0.5)
        q.append(qh)
        kd.append(kh * kdec_ref[h])
        vb = proj_ref[:, OFF_RV + h * RET_DV:OFF_RV + (h + 1) * RET_DV].astype(BF16)
        masked = (_dot_nt(qh.astype(BF16), kh.astype(BF16)) * dmask_ref[h]).astype(BF16)
        o_scr[:, vcols[h]] = _dot(masked, vb)

    for s in range(ns):
        for h in range(RET_HEADS):
            s_prev = sret_ref[s, h]
            inter = _dot(q[h][rows[s]].astype(BF16), s_prev.astype(BF16))
            o_scr[rows[s], vcols[h]] += inter * qdec_ref[h]
            vb = proj_ref[rows[s], OFF_RV + h * RET_DV:OFF_RV + (h + 1) * RET_DV].astype(BF16)
            nret_ref[s, h] = s_prev * cd_ref[h] + _dot_tn(kd[h][rows[s]].astype(BF16), vb)

    for s in range(ns):
        for h in range(MEM_HEADS):
            mq = proj_ref[rows[s], OFF_MQ + h * MEM_HEAD_DIM:OFF_MQ + (h + 1) * MEM_HEAD_DIM].astype(BF16)
            p_scr[rows[s], pcols[h]] = _dot_nt(mq, kc_ref[s, head_rows[h], :].astype(BF16)) * (MEM_HEAD_DIM ** -0.5)
    for h in range(MEM_HEADS):
        p_scr[:, pcols[h]] = _softmax_rows(p_scr[:, pcols[h]])
    for s in range(ns):
        for h in range(MEM_HEADS):
            p = p_scr[rows[s], pcols[h]].astype(BF16)
            m_scr[rows[s], mcols[h]] = _dot(p, vc_ref[s, head_rows[h], :].astype(BF16))
    m_ref[...] = m_scr[...].astype(BF16)

    for h in range(RET_HEADS):
        rg = proj_ref[:, OFF_RG + h * RET_DV:OFF_RG + (h + 1) * RET_DV]
        gated = jax.nn.silu(rg) * (_group_norm(o_scr[:, vcols[h]]) * gn_ref[:, vcols[h]])
        b_ref[:, vcols[h]] = gated.astype(BF16)


def _sample_state_kernel_into(stack_ref, *refs):
    del stack_ref
    _sample_state_kernel(*refs)


def _sample_state(layer, proj, cst, tabs, state_ret, kc, vc, w, ret_stack):
    n = proj.shape[0]
    nseq = state_ret.shape[1]
    t_len = n // nseq
    ns = SAMPLE_SEQS
    rows = ns * t_len
    grid = (nseq // ns,)
    tok_spec = lambda width: pl.BlockSpec((rows, width), lambda i: (i, 0))
    in_specs = [
        pl.BlockSpec(memory_space=pltpu.SMEM),
        tok_spec(OFF_GL),
        tok_spec(CONV_WIDTH),
        _const((rows, RET_DK)),
        _const((rows, RET_DK)),
        _const((RET_HEADS, rows, rows)),
        _const((RET_HEADS, t_len, RET_DV)),
        _const((RET_HEADS, rows, RET_DK)),
        pl.BlockSpec((None, ns, RET_HEADS, RET_DK, RET_DV), lambda i: (layer, i, 0, 0, 0)),
        pl.BlockSpec((None, ns, MEM_LEN * MEM_HEADS, MEM_HEAD_DIM), lambda i: (layer, i, 0, 0)),
        pl.BlockSpec((None, ns, MEM_LEN * MEM_HEADS, MEM_HEAD_DIM), lambda i: (layer, i, 0, 0)),
        _const((CONV_K, CONV_WIDTH)),
        _const((1, RET_V_WIDTH)),
    ]
    out_specs = [
        tok_spec(CONV_WIDTH),
        tok_spec(RET_V_WIDTH),
        tok_spec(MEM_WIDTH),
        tok_spec(CONV_WIDTH),
        pl.BlockSpec((None, ns, RET_HEADS, RET_DK, RET_DV), lambda i: (layer, i, 0, 0, 0)),
    ]
    out_shape = [
        jax.ShapeDtypeStruct((n, CONV_WIDTH), BF16),
        jax.ShapeDtypeStruct((n, RET_V_WIDTH), BF16),
        jax.ShapeDtypeStruct((n, MEM_WIDTH), BF16),
        jax.ShapeDtypeStruct((n, CONV_WIDTH), F32),
        jax.ShapeDtypeStruct((DEPTH, nseq, RET_HEADS, RET_DK, RET_DV), F32),
    ]
    args = (tabs["cd"], proj, cst, tabs["cos"], tabs["sin"], tabs["dmask"], tabs["qdec"], tabs["kdec"],
            state_ret, kc, vc, w["conv_w"], w["ret_gn_g"])
    body, aliases = _sample_state_kernel, {}
    if ret_stack is not None:
        body, aliases = _sample_state_kernel_into, {0: len(out_shape) - 1}
        in_specs = [pl.BlockSpec(memory_space=pl.ANY)] + in_specs
        args = (ret_stack,) + args
    scratch = [
        pltpu.VMEM((rows, RET_V_WIDTH), F32),
        pltpu.VMEM((rows, MEM_HEADS * MEM_LEN), F32),
        pltpu.VMEM((rows, MEM_WIDTH), F32),
    ]
    return pl.pallas_call(
        body,
        grid=grid,
        in_specs=in_specs,
        out_specs=out_specs,
        out_shape=out_shape,
        scratch_shapes=scratch,
        input_output_aliases=aliases,
        compiler_params=pltpu.CompilerParams(
            dimension_semantics=("arbitrary",), vmem_limit_bytes=VMEM_LIMIT_BYTES),
        name="sample_state",
    )(*args)


def _sample_post_kernel(x_ref, a_ref, b_ref, m_ref, g0_ref, g1_ref, g2_ref,
                        bg_ref, wco_ref, wro_ref, wmo_ref, wo_ref, l1g_ref, l1b_ref,
                        wup_ref, wdn_ref, l2g_ref, l2b_ref, y_ref, x1_ref):
    gate_refs = (g0_ref, g1_ref, g2_ref)
    _merge_out_ln(x_ref, (a_ref, b_ref, m_ref), lambda i: gate_refs[i][...], bg_ref,
                  (wco_ref, wro_ref, wmo_ref), wo_ref, l1g_ref, l1b_ref, x1_ref)
    y_ref[...] = _ffn_ln(x1_ref[...], wup_ref, wdn_ref, l2g_ref, l2b_ref)


def _sample_post(x, a_pre, b_pre, m_pre, proj, w):
    n = x.shape[0]
    tm = SAMPLE_POST_TILE
    tok = lambda width: pl.BlockSpec((tm, width), lambda i: (i, 0))
    gate = lambda k: pl.BlockSpec((tm, D_MODEL), lambda i: (i, OFF_GL // D_MODEL + k))
    return pl.pallas_call(
        _sample_post_kernel,
        grid=(n // tm,),
        in_specs=[
            tok(D_MODEL), tok(CONV_WIDTH), tok(RET_V_WIDTH), tok(MEM_WIDTH), gate(0), gate(1), gate(2),
            _const((1, N_BRANCH * D_MODEL)),
            _const((CONV_WIDTH, D_MODEL)),
            _const((RET_V_WIDTH, D_MODEL)),
            _const((MEM_WIDTH, D_MODEL)),
            _const((D_MODEL, D_MODEL)),
            _const((1, D_MODEL)),
            _const((1, D_MODEL)),
            _const((D_MODEL, D_FF)),
            _const((D_FF, D_MODEL)),
            _const((1, D_MODEL)),
            _const((1, D_MODEL)),
        ],
        out_specs=tok(D_MODEL),
        out_shape=jax.ShapeDtypeStruct((n, D_MODEL), F32),
        scratch_shapes=[pltpu.VMEM((tm, D_MODEL), F32)],
        compiler_params=pltpu.CompilerParams(
            dimension_semantics=("arbitrary",), vmem_limit_bytes=VMEM_LIMIT_BYTES),
        name="sample_post",
    )(x, a_pre, b_pre, m_pre, proj, proj, proj,
      w["b_gate"], w["w_conv_out"], w["w_ret_out"], w["w_mem_out"], w["w_out"], w["ln1_g"], w["ln1_b"],
      w["w_up"], w["w_down"], w["ln2_g"], w["ln2_b"])


def _rotary_tables(pos, reps):
    half = RET_DK // 2
    inv = ROPE_BASE ** (-jnp.arange(half, dtype=F32) / half)
    ang = pos.astype(F32)[:, None] * inv[None, :]
    cos = jnp.cos(ang)
    sin = jnp.sin(ang)
    cos_full = jnp.concatenate([cos, cos], axis=-1)
    sin_signed = jnp.concatenate([-sin, sin], axis=-1)
    return jnp.tile(cos_full, (reps, 1)), jnp.tile(sin_signed, (reps, 1))


def _decay_tables(c):
    h = RET_HEADS
    log_g = jnp.log1p(-jnp.exp2(-5.0 - jnp.arange(h, dtype=F32)))
    idx = jnp.arange(c, dtype=F32)
    rel = idx[:, None] - idx[None, :]
    causal = rel >= 0
    dmask = jnp.where(causal[None], jnp.exp(jnp.where(causal, rel, 0.0)[None] * log_g[:, None, None]), 0.0)
    q_decay = jnp.exp((idx + 1.0)[None, :] * log_g[:, None])
    k_decay = jnp.exp((c - 1.0 - idx)[None, :] * log_g[:, None])
    chunk_decay = jnp.exp(c * log_g)
    return {
        "dmask": dmask,
        "qdec": jnp.broadcast_to(q_decay[:, :, None], (h, c, RET_DV)),
        "kdec": jnp.broadcast_to(k_decay[:, :, None], (h, c, RET_DK)),
        "cd": chunk_decay,
    }


def kernel(x_prompt, x_sample, cache_mem_k, cache_mem_v, state_conv, state_ret, mem_prompt,
           w_in, b_gate, conv_w, ret_gn_g, w_conv_out, w_ret_out, w_mem_out, w_out, w_mem_kv,
           ln1_g, ln1_b, w_up, w_down, ln2_g, ln2_b):
    bp, seq, _ = x_prompt.shape
    nseq, t_len, _ = x_sample.shape
    assert seq % PROMPT_TILE == 0 and PROMPT_TILE % RET_CHUNK == 0
    assert t_len == 8 and nseq % SAMPLE_SEQS == 0 and t_len % RET_CHUNK != 0

    big = {"w_in": w_in, "w_conv_out": w_conv_out, "w_ret_out": w_ret_out, "w_mem_out": w_mem_out,
           "w_out": w_out, "w_up": w_up, "w_down": w_down}
    small = {"b_gate": b_gate, "ret_gn_g": ret_gn_g, "ln1_g": ln1_g, "ln1_b": ln1_b, "ln2_g": ln2_g, "ln2_b": ln2_b}
    big_b = {name: a[0].astype(BF16) for name, a in big.items()}

    tabs_p = _decay_tables(RET_CHUNK)
    tabs_p["cos"], tabs_p["sin"] = _rotary_tables(jnp.arange(seq, dtype=jnp.int32), 1)
    tabs_s = _decay_tables(t_len)
    eye = jnp.eye(SAMPLE_SEQS, dtype=F32)
    tabs_s["dmask"] = (eye[None, :, None, :, None] * tabs_s["dmask"][:, None, :, None, :]).reshape(
        RET_HEADS, SAMPLE_SEQS * t_len, SAMPLE_SEQS * t_len)
    tabs_s["kdec"] = jnp.tile(tabs_s["kdec"], (1, SAMPLE_SEQS, 1))
    tabs_s["cos"], tabs_s["sin"] = _rotary_tables(PAST_LEN + jnp.arange(t_len, dtype=jnp.int32), SAMPLE_SEQS)

    mk_f, mv_f, mk_b, mv_b = _mem_kv(mem_prompt, w_mem_kv.astype(BF16))
    mk_b = mk_b.reshape(DEPTH, bp, MEM_LEN, MEM_WIDTH)
    mv_b = mv_b.reshape(DEPTH, bp, MEM_LEN, MEM_WIDTH)

    kc = cache_mem_k.reshape(DEPTH, nseq, MEM_LEN * MEM_HEADS, MEM_HEAD_DIM)
    vc = cache_mem_v.reshape(DEPTH, nseq, MEM_LEN * MEM_HEADS, MEM_HEAD_DIM)
    cst = jnp.concatenate(
        [state_conv, jnp.zeros((DEPTH, nseq, t_len - (CONV_K - 1), CONV_WIDTH), state_conv.dtype)], axis=2
    ).reshape(DEPTH, nseq * t_len, CONV_WIDTH)

    xp = x_prompt
    xs = x_sample.reshape(nseq * t_len, D_MODEL)
    cp_list, rp_list, cs_list = [], [], []
    ret_stack = None
    for l in range(DEPTH):
        w = dict(big_b, conv_w=conv_w[l], **{name: a[l].reshape(1, -1) for name, a in small.items()})
        to_cast = list(big.values()) if l + 1 < DEPTH else []
        xp1, ctail_p, ret_p, *cast = _prompt_mixer(l, xp, tabs_p, mk_b, mv_b, w, to_cast)
        big_b = dict(zip(big, cast))
        xp = _ffn(xp1.reshape(bp * seq, D_MODEL), w).reshape(bp, seq, D_MODEL)
        cp_list.append(ctail_p[:, 8 - (CONV_K - 1):, :])
        rp_list.append(ret_p)

        proj = _sample_proj(xs, w["w_in"])
        a_pre, b_pre, m_pre, u_s, ret_stack = _sample_state(l, proj, cst[l], tabs_s, state_ret, kc, vc, w, ret_stack)
        xs = _sample_post(xs, a_pre, b_pre, m_pre, proj, w)
        cs_list.append(u_s.reshape(nseq, t_len, CONV_WIDTH)[:, t_len - (CONV_K - 1):, :])

    mem_shape = (DEPTH, bp, MEM_LEN, MEM_HEADS, MEM_HEAD_DIM)
    return (xp, xs.reshape(nseq, t_len, D_MODEL), mk_f.reshape(mem_shape), mv_f.reshape(mem_shape),
            jnp.stack(cp_list), jnp.stack(rp_list), jnp.stack(cs_list), ret_stack)
```

```python
import functools

import jax
import jax.numpy as jnp
from jax import lax
from jax.experimental import pallas as pl
from jax.experimental.pallas import tpu as pltpu

D_MODEL = 1024
DEPTH = 4
CONV_WIDTH = 512
CONV_K = 3
RET_HEADS = 4
RET_DK = 128
RET_DV = 256
RET_QK_WIDTH = RET_HEADS * RET_DK
RET_V_WIDTH = RET_HEADS * RET_DV
RET_CHUNK = 128
MEM_LEN = 256
MEM_HEADS = 4
MEM_HEAD_DIM = 128
MEM_WIDTH = MEM_HEADS * MEM_HEAD_DIM
N_BRANCH = 3
D_FF = 4 * D_MODEL
ROPE_BASE = 10000.0
LN_EPS = 1e-5
GN_EPS = 1e-6
PAST_LEN = 16384
DEEPNORM_ALPHA = (2 * DEPTH) ** 0.25

OFF_CB = 0
OFF_CC = OFF_CB + CONV_WIDTH
OFF_CH = OFF_CC + CONV_WIDTH
OFF_RQ = OFF_CH + CONV_WIDTH
OFF_RK = OFF_RQ + RET_QK_WIDTH
OFF_RV = OFF_RK + RET_QK_WIDTH
OFF_RG = OFF_RV + RET_V_WIDTH
OFF_MQ = OFF_RG + RET_V_WIDTH
OFF_GL = OFF_MQ + MEM_WIDTH
IN_WIDTH = OFF_GL + N_BRANCH * D_MODEL

V7X_VMEM_BYTES = 64 * 1024 * 1024
VMEM_LIMIT_BYTES = V7X_VMEM_BYTES - 6 * 1024 * 1024

PROMPT_TILE = 512
FFN_TILE = 512
FFN_CHUNK = 1024
SAMPLE_SEQS = 4
SAMPLE_POST_TILE = 512

BF16 = jnp.bfloat16
F32 = jnp.float32


def _dot(a, b):
    return jnp.dot(a, b, preferred_element_type=F32)


def _dot_nt(a, b):
    return lax.dot_general(a, b, (((1,), (1,)), ((), ())), preferred_element_type=F32)


def _dot_tn(a, b):
    return lax.dot_general(a, b, (((0,), (0,)), ((), ())), preferred_element_type=F32)


def _layer_norm(z, g, b):
    mu = jnp.mean(z, axis=-1, keepdims=True)
    zc = z - mu
    var = jnp.mean(zc * zc, axis=-1, keepdims=True)
    return zc * lax.rsqrt(var + LN_EPS) * g + b


def _group_norm(o):
    mu = jnp.mean(o, axis=-1, keepdims=True)
    oc = o - mu
    var = jnp.mean(oc * oc, axis=-1, keepdims=True)
    return oc * lax.rsqrt(var + GN_EPS)


def _rotary(xh, cos, sin_signed):
    return xh * cos + pltpu.roll(xh, RET_DK // 2, axis=1) * sin_signed


def _softmax_rows(s):
    m = jnp.max(s, axis=-1, keepdims=True)
    e = jnp.exp(s - m)
    return e / jnp.sum(e, axis=-1, keepdims=True)


def _merge_out_ln(x_ref, pre_refs, gate_logits, bg_ref, w_refs, wo_ref, g_ref, b_ref, y_ref):
    merged = None
    for i, (pre_ref, w_ref) in enumerate(zip(pre_refs, w_refs)):
        gate = jax.nn.sigmoid(gate_logits(i) + bg_ref[:, i * D_MODEL:(i + 1) * D_MODEL])
        term = gate * _dot(pre_ref[...], w_ref[...])
        merged = term if merged is None else merged + term
    merged = merged.astype(BF16)
    half = merged.shape[0] // 2
    for rows in (slice(0, half), slice(half, 2 * half)):
        z = DEEPNORM_ALPHA * x_ref[rows, :] + _dot(merged[rows], wo_ref[...])
        y_ref[rows, :] = _layer_norm(z, g_ref[...], b_ref[...])


def _ffn_ln(x, wup_ref, wdn_ref, g_ref, b_ref):
    xb = x.astype(BF16)
    acc = None
    for c in range(D_FF // FFN_CHUNK):
        cols = slice(c * FFN_CHUNK, (c + 1) * FFN_CHUNK)
        h = jnp.maximum(_dot(xb, wup_ref[:, cols]), 0.0)
        part = _dot((h * h).astype(BF16), wdn_ref[cols, :])
        acc = part if acc is None else acc + part
    return _layer_norm(DEEPNORM_ALPHA * x + acc, g_ref[...], b_ref[...])


MEM_KV_ROWS = 1024


def _mem_kv_kernel(mem_ref, w_ref, k_ref, v_ref, kb_ref, vb_ref):
    for r0 in range(0, mem_ref.shape[0], MEM_KV_ROWS):
        rows = slice(r0, r0 + MEM_KV_ROWS)
        kv = _dot(mem_ref[rows, :].astype(BF16), w_ref[...])
        kb_ref[rows, :] = kv[:, :MEM_WIDTH].astype(BF16)
        vb_ref[rows, :] = kv[:, MEM_WIDTH:].astype(BF16)
        for h in range(MEM_HEADS):
            head_rows = pl.ds(r0 * MEM_HEADS + h, MEM_KV_ROWS, stride=MEM_HEADS)
            k_ref[head_rows, :] = kv[:, h * MEM_HEAD_DIM:(h + 1) * MEM_HEAD_DIM]
            v_ref[head_rows, :] = kv[:, MEM_WIDTH + h * MEM_HEAD_DIM:MEM_WIDTH + (h + 1) * MEM_HEAD_DIM]


def _mem_kv(mem_prompt, w_mem_kv_b):
    bp = mem_prompt.shape[0]
    n = bp * MEM_LEN
    assert n % MEM_KV_ROWS == 0
    out_f = jax.ShapeDtypeStruct((DEPTH, n * MEM_HEADS, MEM_HEAD_DIM), F32)
    out_b = jax.ShapeDtypeStruct((DEPTH, n, MEM_WIDTH), BF16)
    f_spec = pl.BlockSpec((None, n * MEM_HEADS, MEM_HEAD_DIM), lambda l: (l, 0, 0))
    b_spec = pl.BlockSpec((None, n, MEM_WIDTH), lambda l: (l, 0, 0))
    return pl.pallas_call(
        _mem_kv_kernel,
        grid=(DEPTH,),
        in_specs=[
            _const((n, D_MODEL)),
            pl.BlockSpec((None, D_MODEL, 2 * MEM_WIDTH), lambda l: (l, 0, 0)),
        ],
        out_specs=[f_spec, f_spec, b_spec, b_spec],
        out_shape=[out_f, out_f, out_b, out_b],
        compiler_params=pltpu.CompilerParams(
            dimension_semantics=("arbitrary",), vmem_limit_bytes=VMEM_LIMIT_BYTES),
        name="mem_kv",
    )(mem_prompt.reshape(n, D_MODEL), w_mem_kv_b)


MIXER_INPUTS = 19
MIXER_OUTPUTS = 3


def _prompt_mixer_kernel(n_cast, *refs):
    (cd_ref, x_ref, cos_ref, sin_ref, dmask_ref, qdec_ref, kdec_ref, mk_ref, mv_ref,
     win_ref, bg_ref, cw_ref, gn_ref, wco_ref, wro_ref, wmo_ref, wo_ref, lg_ref, lb_ref) = refs[:MIXER_INPUTS]
    cast_src = refs[MIXER_INPUTS:MIXER_INPUTS + n_cast]
    outs = refs[MIXER_INPUTS + n_cast:]
    y_ref, ctail_ref, ret_ref = outs[:MIXER_OUTPUTS]
    cast_dst = outs[MIXER_OUTPUTS:MIXER_OUTPUTS + n_cast]
    (xb_ref, q_ref, k_ref, kd_ref, vb_ref, msk_ref, kv_ref,
     abuf_ref, bbuf_ref, mbuf_ref) = outs[MIXER_OUTPUTS + n_cast:]

    for src_ref, dst_ref in zip(cast_src, cast_dst):
        dst_ref[...] = src_ref[...].astype(BF16)

    t = pl.program_id(1)
    tq = x_ref.shape[0]
    chunks = [slice(j * RET_CHUNK, (j + 1) * RET_CHUNK) for j in range(tq // RET_CHUNK)]
    kcols = [slice(h * RET_DK, (h + 1) * RET_DK) for h in range(RET_HEADS)]
    vcols = [slice(h * RET_DV, (h + 1) * RET_DV) for h in range(RET_HEADS)]
    mcols = [slice(h * MEM_HEAD_DIM, (h + 1) * MEM_HEAD_DIM) for h in range(MEM_HEADS)]

    @pl.when(t == 0)
    def _():
        ret_ref[...] = jnp.zeros_like(ret_ref)
        ctail_ref[...] = jnp.zeros_like(ctail_ref)

    xb_ref[...] = x_ref[...].astype(BF16)

    def proj(lo, width):
        return _dot(xb_ref[...], win_ref[:, lo:lo + width])

    pq = proj(OFF_RQ, RET_QK_WIDTH)
    pk = proj(OFF_RK, RET_QK_WIDTH)
    vb_ref[...] = proj(OFF_RV, RET_V_WIDTH).astype(BF16)
    cos = cos_ref[...]
    sin = sin_ref[...]
    for h in range(RET_HEADS):
        q_ref[:, kcols[h]] = _rotary(pq[:, kcols[h]], cos, sin).astype(BF16)
        kh = _rotary(pk[:, kcols[h]], cos, sin) * (RET_DK ** -0.5)
        k_ref[:, kcols[h]] = kh.astype(BF16)
        for r in chunks:
            kd_ref[r, kcols[h]] = (kh[r] * kdec_ref[h]).astype(BF16)

    pm = proj(OFF_MQ, MEM_WIDTH).astype(BF16)
    mem_s = [_dot_nt(pm[:, mcols[h]], mk_ref[:, mcols[h]]) * (MEM_HEAD_DIM ** -0.5) for h in range(MEM_HEADS)]

    cb = proj(OFF_CB, CONV_WIDTH)
    u = proj(OFF_CC, CONV_WIDTH) * proj(OFF_CH, CONV_WIDTH)
    row = lax.broadcasted_iota(jnp.int32, u.shape, 0)
    prev1 = ctail_ref[7:8, :]
    prev2 = ctail_ref[6:7, :]
    u1 = jnp.where(row == 0, prev1, pltpu.roll(u, 1, axis=0))
    u2 = jnp.where(row == 0, prev2, jnp.where(row == 1, prev1, pltpu.roll(u, 2, axis=0)))
    conv_y = u2 * cw_ref[0:1, :] + u1 * cw_ref[1:2, :] + u * cw_ref[2:3, :]
    ctail_ref[...] = u[tq - 8:, :]
    abuf_ref[...] = (cb * conv_y).astype(BF16)

    for h in range(MEM_HEADS):
        p = _softmax_rows(mem_s[h])
        mbuf_ref[:, mcols[h]] = _dot(p.astype(BF16), mv_ref[:, mcols[h]]).astype(BF16)

    for h in range(RET_HEADS):
        for r in chunks:
            msk_ref[r, kcols[h]] = (_dot_nt(q_ref[r, kcols[h]], k_ref[r, kcols[h]]) * dmask_ref[h]).astype(BF16)

    for h in range(RET_HEADS):
        for j, r in enumerate(chunks):
            kv_ref[h, j] = _dot_tn(kd_ref[r, kcols[h]], vb_ref[r, vcols[h]])

    for h in range(RET_HEADS):
        pg = proj(OFF_RG + h * RET_DV, RET_DV)
        s_cur = ret_ref[h]
        for j, r in enumerate(chunks):
            intra = _dot(msk_ref[r, kcols[h]], vb_ref[r, vcols[h]])
            o = intra + _dot(q_ref[r, kcols[h]], s_cur.astype(BF16)) * qdec_ref[h]
            gated = jax.nn.silu(pg[r]) * (_group_norm(o) * gn_ref[:, vcols[h]])
            bbuf_ref[r, vcols[h]] = gated.astype(BF16)
            s_cur = s_cur * cd_ref[h] + kv_ref[h, j]
        ret_ref[h] = s_cur

    _merge_out_ln(x_ref, (abuf_ref, bbuf_ref, mbuf_ref), lambda i: proj(OFF_GL + i * D_MODEL, D_MODEL), bg_ref,
                  (wco_ref, wro_ref, wmo_ref), wo_ref, lg_ref, lb_ref, y_ref)


def _const(shape):
    nd = len(shape)
    return pl.BlockSpec(tuple(shape), lambda *_: (0,) * nd, pipeline_mode=pl.Buffered(1))


def _prompt_mixer(layer, x, tabs, mk_b, mv_b, w, to_cast):
    bp, seq, _ = x.shape
    tq = PROMPT_TILE
    grid = (bp, seq // tq)
    steps = bp * (seq // tq)
    steps_per_seq = seq // tq
    cast_in_specs, cast_out_specs, cast_out_shapes = [], [], []
    for a in to_cast:
        _, rows, cols = a.shape
        slab = rows // steps
        assert slab * steps == rows and slab % 16 == 0
        cast_in_specs.append(pl.BlockSpec((None, slab, cols), lambda b, t: (layer + 1, b * steps_per_seq + t, 0)))
        cast_out_specs.append(pl.BlockSpec((slab, cols), lambda b, t: (b * steps_per_seq + t, 0)))
        cast_out_shapes.append(jax.ShapeDtypeStruct((rows, cols), BF16))
    in_specs = [
        pl.BlockSpec(memory_space=pltpu.SMEM),
        pl.BlockSpec((None, tq, D_MODEL), lambda b, t: (b, t, 0)),
        pl.BlockSpec((tq, RET_DK), lambda b, t: (t, 0)),
        pl.BlockSpec((tq, RET_DK), lambda b, t: (t, 0)),
        _const((RET_HEADS, RET_CHUNK, RET_CHUNK)),
        _const((RET_HEADS, RET_CHUNK, RET_DV)),
        _const((RET_HEADS, RET_CHUNK, RET_DK)),
        pl.BlockSpec((None, None, MEM_LEN, MEM_WIDTH), lambda b, t: (layer, b, 0, 0)),
        pl.BlockSpec((None, None, MEM_LEN, MEM_WIDTH), lambda b, t: (layer, b, 0, 0)),
        _const((D_MODEL, IN_WIDTH)),
        _const((1, N_BRANCH * D_MODEL)),
        _const((CONV_K, CONV_WIDTH)),
        _const((1, RET_V_WIDTH)),
        _const((CONV_WIDTH, D_MODEL)),
        _const((RET_V_WIDTH, D_MODEL)),
        _const((MEM_WIDTH, D_MODEL)),
        _const((D_MODEL, D_MODEL)),
        _const((1, D_MODEL)),
        _const((1, D_MODEL)),
    ]
    out_specs = [
        pl.BlockSpec((None, tq, D_MODEL), lambda b, t: (b, t, 0)),
        pl.BlockSpec((None, 8, CONV_WIDTH), lambda b, t: (b, 0, 0)),
        pl.BlockSpec((None, RET_HEADS, RET_DK, RET_DV), lambda b, t: (b, 0, 0, 0)),
    ]
    out_shape = [
        jax.ShapeDtypeStruct((bp, seq, D_MODEL), F32),
        jax.ShapeDtypeStruct((bp, 8, CONV_WIDTH), F32),
        jax.ShapeDtypeStruct((bp, RET_HEADS, RET_DK, RET_DV), F32),
    ]
    return pl.pallas_call(
        functools.partial(_prompt_mixer_kernel, len(to_cast)),
        grid=grid,
        in_specs=in_specs + cast_in_specs,
        out_specs=out_specs + cast_out_specs,
        out_shape=out_shape + cast_out_shapes,
        scratch_shapes=[
            pltpu.VMEM((tq, D_MODEL), BF16),
            pltpu.VMEM((tq, RET_QK_WIDTH), BF16),
            pltpu.VMEM((tq, RET_QK_WIDTH), BF16),
            pltpu.VMEM((tq, RET_QK_WIDTH), BF16),
            pltpu.VMEM((tq, RET_V_WIDTH), BF16),
            pltpu.VMEM((tq, RET_QK_WIDTH), BF16),
            pltpu.VMEM((RET_HEADS, tq // RET_CHUNK, RET_DK, RET_DV), F32),
            pltpu.VMEM((tq, CONV_WIDTH), BF16),
            pltpu.VMEM((tq, RET_V_WIDTH), BF16),
            pltpu.VMEM((tq, MEM_WIDTH), BF16),
        ],
        compiler_params=pltpu.CompilerParams(
            dimension_semantics=("arbitrary", "arbitrary"), vmem_limit_bytes=VMEM_LIMIT_BYTES),
        name="prompt_mixer",
    )(tabs["cd"], x, tabs["cos"], tabs["sin"], tabs["dmask"], tabs["qdec"], tabs["kdec"], mk_b, mv_b,
      w["w_in"], w["b_gate"], w["conv_w"], w["ret_gn_g"], w["w_conv_out"], w["w_ret_out"], w["w_mem_out"],
      w["w_out"], w["ln1_g"], w["ln1_b"], *to_cast)


FFN_INPUTS = 5
STATE_INPUTS = 13
STATE_OUTPUTS = 5


def _ffn_state_kernel(aliased, *refs):
    if aliased:
        refs = refs[1:]
    x_ref, wup_ref, wdn_ref, g_ref, b_ref = refs[:FFN_INPUTS]
    state_in = refs[FFN_INPUTS:FFN_INPUTS + STATE_INPUTS]
    y_ref = refs[FFN_INPUTS + STATE_INPUTS]
    rest = refs[FFN_INPUTS + STATE_INPUTS + 1:]
    stages = _sample_state_stages(*state_in, *rest)
    assert len(stages) == D_FF // FFN_CHUNK

    x = x_ref[...]
    xb = x.astype(BF16)
    acc = None
    for c in range(D_FF // FFN_CHUNK):
        cols = slice(c * FFN_CHUNK, (c + 1) * FFN_CHUNK)
        h = jnp.maximum(_dot(xb, wup_ref[:, cols]), 0.0)
        stages[c]()
        part = _dot((h * h).astype(BF16), wdn_ref[cols, :])
        acc = part if acc is None else acc + part
    y_ref[...] = _layer_norm(DEEPNORM_ALPHA * x + acc, g_ref[...], b_ref[...])


def _ffn_and_sample_state(layer, x, w, proj, cst, tabs, state_ret, kc, vc, ret_stack):
    n = x.shape[0]
    tm = FFN_TILE
    ns = SAMPLE_SEQS
    n_tok = proj.shape[0]
    nseq = state_ret.shape[1]
    t_len = n_tok // nseq
    rows = ns * t_len
    assert n // tm == nseq // ns
    tok_spec = lambda width: pl.BlockSpec((rows, width), lambda i: (i, 0))
    in_specs = [
        pl.BlockSpec((tm, D_MODEL), lambda i: (i, 0)),
        _const((D_MODEL, D_FF)),
        _const((D_FF, D_MODEL)),
        _const((1, D_MODEL)),
        _const((1, D_MODEL)),
        pl.BlockSpec(memory_space=pltpu.SMEM),
        tok_spec(OFF_GL),
        tok_spec(CONV_WIDTH),
        _const((rows, RET_DK)),
        _const((rows, RET_DK)),
        _const((RET_HEADS, rows, rows)),
        _const((RET_HEADS, t_len, RET_DV)),
        _const((RET_HEADS, rows, RET_DK)),
        pl.BlockSpec((None, ns, RET_HEADS, RET_DK, RET_DV), lambda i: (layer, i, 0, 0, 0)),
        pl.BlockSpec((None, ns, MEM_LEN * MEM_HEADS, MEM_HEAD_DIM), lambda i: (layer, i, 0, 0)),
        pl.BlockSpec((None, ns, MEM_LEN * MEM_HEADS, MEM_HEAD_DIM), lambda i: (layer, i, 0, 0)),
        _const((CONV_K, CONV_WIDTH)),
        _const((1, RET_V_WIDTH)),
    ]
    out_specs = [
        pl.BlockSpec((tm, D_MODEL), lambda i: (i, 0)),
        tok_spec(CONV_WIDTH),
        tok_spec(RET_V_WIDTH),
        tok_spec(MEM_WIDTH),
        tok_spec(CONV_WIDTH),
        pl.BlockSpec((None, ns, RET_HEADS, RET_DK, RET_DV), lambda i: (layer, i, 0, 0, 0)),
    ]
    out_shape = [
        jax.ShapeDtypeStruct((n, D_MODEL), F32),
        jax.ShapeDtypeStruct((n_tok, CONV_WIDTH), BF16),
        jax.ShapeDtypeStruct((n_tok, RET_V_WIDTH), BF16),
        jax.ShapeDtypeStruct((n_tok, MEM_WIDTH), BF16),
        jax.ShapeDtypeStruct((n_tok, CONV_WIDTH), F32),
        jax.ShapeDtypeStruct((DEPTH, nseq, RET_HEADS, RET_DK, RET_DV), F32),
    ]
    assert len(in_specs) == FFN_INPUTS + STATE_INPUTS and len(out_specs) == 1 + STATE_OUTPUTS
    args = (x, w["w_up"], w["w_down"], w["ln2_g"], w["ln2_b"],
            tabs["cd"], proj, cst, tabs["cos"], tabs["sin"], tabs["dmask"], tabs["qdec"], tabs["kdec"],
            state_ret, kc, vc, w["conv_w"], w["ret_gn_g"])
    aliases = {}
    if ret_stack is not None:
        aliases = {0: len(out_shape) - 1}
        in_specs = [pl.BlockSpec(memory_space=pl.ANY)] + in_specs
        args = (ret_stack,) + args
    scratch = [
        pltpu.VMEM((rows, RET_V_WIDTH), F32),
        pltpu.VMEM((rows, MEM_HEADS * MEM_LEN), F32),
        pltpu.VMEM((rows, MEM_WIDTH), F32),
    ]
    return pl.pallas_call(
        functools.partial(_ffn_state_kernel, ret_stack is not None),
        grid=(n // tm,),
        in_specs=in_specs,
        out_specs=out_specs,
        out_shape=out_shape,
        scratch_shapes=scratch,
        input_output_aliases=aliases,
        compiler_params=pltpu.CompilerParams(
            dimension_semantics=("arbitrary",), vmem_limit_bytes=VMEM_LIMIT_BYTES),
        name="ffn_state",
    )(*args)


def _sample_proj_kernel(x_ref, w_ref, o_ref):
    o_ref[...] = _dot(x_ref[...].astype(BF16), w_ref[...])


def _sample_proj(x, w_in_b):
    n = x.shape[0]
    tn = 1024
    return pl.pallas_call(
        _sample_proj_kernel,
        grid=(IN_WIDTH // tn,),
        in_specs=[
            pl.BlockSpec((n, D_MODEL), lambda j: (0, 0)),
            pl.BlockSpec((D_MODEL, tn), lambda j: (0, j)),
        ],
        out_specs=pl.BlockSpec((n, tn), lambda j: (0, j)),
        out_shape=jax.ShapeDtypeStruct((n, IN_WIDTH), F32),
        compiler_params=pltpu.CompilerParams(
            dimension_semantics=("arbitrary",), vmem_limit_bytes=VMEM_LIMIT_BYTES),
        name="sample_proj",
    )(x, w_in_b)


def _sample_state_stages(cd_ref, proj_ref, cst_ref, cos_ref, sin_ref, dmask_ref, qdec_ref, kdec_ref,
                         sret_ref, kc_ref, vc_ref, cw_ref, gn_ref,
                         a_ref, b_ref, m_ref, u_ref, nret_ref,
                         o_scr, p_scr, m_scr):
    ns = sret_ref.shape[0]
    t_len = proj_ref.shape[0] // ns
    rows = [slice(s * t_len, (s + 1) * t_len) for s in range(ns)]
    kcols = [slice(h * RET_DK, (h + 1) * RET_DK) for h in range(RET_HEADS)]
    vcols = [slice(h * RET_DV, (h + 1) * RET_DV) for h in range(RET_HEADS)]
    mcols = [slice(h * MEM_HEAD_DIM, (h + 1) * MEM_HEAD_DIM) for h in range(MEM_HEADS)]
    pcols = [slice(h * MEM_LEN, (h + 1) * MEM_LEN) for h in range(MEM_HEADS)]
    head_rows = [pl.ds(h, MEM_LEN, stride=MEM_HEADS) for h in range(MEM_HEADS)]

    q, kd = [], []

    def conv_and_intra_chunk():
        cb = proj_ref[:, OFF_CB:OFF_CB + CONV_WIDTH]
        u = proj_ref[:, OFF_CC:OFF_CC + CONV_WIDTH] * proj_ref[:, OFF_CH:OFF_CH + CONV_WIDTH]
        tok = lax.broadcasted_iota(jnp.int32, u.shape, 0) & (t_len - 1)
        st2 = cst_ref[...]
        st1 = pltpu.roll(st2, st2.shape[0] - 1, axis=0)
        u1 = jnp.where(tok == 0, st1, pltpu.roll(u, 1, axis=0))
        u2 = jnp.where(tok < 2, st2, pltpu.roll(u, 2, axis=0))
        conv_y = u2 * cw_ref[0:1, :] + u1 * cw_ref[1:2, :] + u * cw_ref[2:3, :]
        u_ref[...] = u
        a_ref[...] = (cb * conv_y).astype(BF16)

        cos = cos_ref[...]
        sin = sin_ref[...]
        for h in range(RET_HEADS):
            qh = _rotary(proj_ref[:, OFF_RQ + h * RET_DK:OFF_RQ + (h + 1) * RET_DK], cos, sin)
            kh = _rotary(proj_ref[:, OFF_RK + h * RET_DK:OFF_RK + (h + 1) * RET_DK], cos, sin) * (RET_DK ** -0.5)
            q.append(qh)
            kd.append(kh * kdec_ref[h])
            vb = proj_ref[:, OFF_RV + h * RET_DV:OFF_RV + (h + 1) * RET_DV].astype(BF16)
            masked = (_dot_nt(qh.astype(BF16), kh.astype(BF16)) * dmask_ref[h]).astype(BF16)
            o_scr[:, vcols[h]] = _dot(masked, vb)

    def cross_chunk_and_state():
        for s in range(ns):
            for h in range(RET_HEADS):
                s_prev = sret_ref[s, h]
                inter = _dot(q[h][rows[s]].astype(BF16), s_prev.astype(BF16))
                o_scr[rows[s], vcols[h]] += inter * qdec_ref[h]
                vb = proj_ref[rows[s], OFF_RV + h * RET_DV:OFF_RV + (h + 1) * RET_DV].astype(BF16)
                nret_ref[s, h] = s_prev * cd_ref[h] + _dot_tn(kd[h][rows[s]].astype(BF16), vb)

    def attention_scores():
        for s in range(ns):
            for h in range(MEM_HEADS):
                mq = proj_ref[rows[s], OFF_MQ + h * MEM_HEAD_DIM:OFF_MQ + (h + 1) * MEM_HEAD_DIM].astype(BF16)
                p_scr[rows[s], pcols[h]] = (_dot_nt(mq, kc_ref[s, head_rows[h], :].astype(BF16))
                                            * (MEM_HEAD_DIM ** -0.5))

    def attention_values():
        for h in range(MEM_HEADS):
            p_scr[:, pcols[h]] = _softmax_rows(p_scr[:, pcols[h]])
        for s in range(ns):
            for h in range(MEM_HEADS):
                p = p_scr[rows[s], pcols[h]].astype(BF16)
                m_scr[rows[s], mcols[h]] = _dot(p, vc_ref[s, head_rows[h], :].astype(BF16))
        m_ref[...] = m_scr[...].astype(BF16)

    def norm_and_gate():
        for h in range(RET_HEADS):
            rg = proj_ref[:, OFF_RG + h * RET_DV:OFF_RG + (h + 1) * RET_DV]
            gated = jax.nn.silu(rg) * (_group_norm(o_scr[:, vcols[h]]) * gn_ref[:, vcols[h]])
            b_ref[:, vcols[h]] = gated.astype(BF16)

    def scores_then_norm_and_gate():
        attention_scores()
        norm_and_gate()

    return [conv_and_intra_chunk, cross_chunk_and_state, scores_then_norm_and_gate, attention_values]


def _sample_post_kernel(x_ref, a_ref, b_ref, m_ref, g0_ref, g1_ref, g2_ref,
                        bg_ref, wco_ref, wro_ref, wmo_ref, wo_ref, l1g_ref, l1b_ref,
                        wup_ref, wdn_ref, l2g_ref, l2b_ref, y_ref, x1_ref):
    gate_refs = (g0_ref, g1_ref, g2_ref)
    _merge_out_ln(x_ref, (a_ref, b_ref, m_ref), lambda i: gate_refs[i][...], bg_ref,
                  (wco_ref, wro_ref, wmo_ref), wo_ref, l1g_ref, l1b_ref, x1_ref)
    y_ref[...] = _ffn_ln(x1_ref[...], wup_ref, wdn_ref, l2g_ref, l2b_ref)


def _sample_post(x, a_pre, b_pre, m_pre, proj, w):
    n = x.shape[0]
    tm = SAMPLE_POST_TILE
    tok = lambda width: pl.BlockSpec((tm, width), lambda i: (i, 0))
    gate = lambda k: pl.BlockSpec((tm, D_MODEL), lambda i: (i, OFF_GL // D_MODEL + k))
    return pl.pallas_call(
        _sample_post_kernel,
        grid=(n // tm,),
        in_specs=[
            tok(D_MODEL), tok(CONV_WIDTH), tok(RET_V_WIDTH), tok(MEM_WIDTH), gate(0), gate(1), gate(2),
            _const((1, N_BRANCH * D_MODEL)),
            _const((CONV_WIDTH, D_MODEL)),
            _const((RET_V_WIDTH, D_MODEL)),
            _const((MEM_WIDTH, D_MODEL)),
            _const((D_MODEL, D_MODEL)),
            _const((1, D_MODEL)),
            _const((1, D_MODEL)),
            _const((D_MODEL, D_FF)),
            _const((D_FF, D_MODEL)),
            _const((1, D_MODEL)),
            _const((1, D_MODEL)),
        ],
        out_specs=tok(D_MODEL),
        out_shape=jax.ShapeDtypeStruct((n, D_MODEL), F32),
        scratch_shapes=[pltpu.VMEM((tm, D_MODEL), F32)],
        compiler_params=pltpu.CompilerParams(
            dimension_semantics=("arbitrary",), vmem_limit_bytes=VMEM_LIMIT_BYTES),
        name="sample_post",
    )(x, a_pre, b_pre, m_pre, proj, proj, proj,
      w["b_gate"], w["w_conv_out"], w["w_ret_out"], w["w_mem_out"], w["w_out"], w["ln1_g"], w["ln1_b"],
      w["w_up"], w["w_down"], w["ln2_g"], w["ln2_b"])


def _rotary_tables(pos, reps):
    half = RET_DK // 2
    inv = ROPE_BASE ** (-jnp.arange(half, dtype=F32) / half)
    ang = pos.astype(F32)[:, None] * inv[None, :]
    cos = jnp.cos(ang)
    sin = jnp.sin(ang)
    cos_full = jnp.concatenate([cos, cos], axis=-1)
    sin_signed = jnp.concatenate([-sin, sin], axis=-1)
    return jnp.tile(cos_full, (reps, 1)), jnp.tile(sin_signed, (reps, 1))


def _decay_tables(c):
    h = RET_HEADS
    log_g = jnp.log1p(-jnp.exp2(-5.0 - jnp.arange(h, dtype=F32)))
    idx = jnp.arange(c, dtype=F32)
    rel = idx[:, None] - idx[None, :]
    causal = rel >= 0
    dmask = jnp.where(causal[None], jnp.exp(jnp.where(causal, rel, 0.0)[None] * log_g[:, None, None]), 0.0)
    q_decay = jnp.exp((idx + 1.0)[None, :] * log_g[:, None])
    k_decay = jnp.exp((c - 1.0 - idx)[None, :] * log_g[:, None])
    chunk_decay = jnp.exp(c * log_g)
    return {
        "dmask": dmask,
        "qdec": jnp.broadcast_to(q_decay[:, :, None], (h, c, RET_DV)),
        "kdec": jnp.broadcast_to(k_decay[:, :, None], (h, c, RET_DK)),
        "cd": chunk_decay,
    }


def kernel(x_prompt, x_sample, cache_mem_k, cache_mem_v, state_conv, state_ret, mem_prompt,
           w_in, b_gate, conv_w, ret_gn_g, w_conv_out, w_ret_out, w_mem_out, w_out, w_mem_kv,
           ln1_g, ln1_b, w_up, w_down, ln2_g, ln2_b):
    bp, seq, _ = x_prompt.shape
    nseq, t_len, _ = x_sample.shape
    assert seq % PROMPT_TILE == 0 and PROMPT_TILE % RET_CHUNK == 0
    assert t_len == 8 and nseq % SAMPLE_SEQS == 0 and t_len % RET_CHUNK != 0

    big = {"w_in": w_in, "w_conv_out": w_conv_out, "w_ret_out": w_ret_out, "w_mem_out": w_mem_out,
           "w_out": w_out, "w_up": w_up, "w_down": w_down}
    small = {"b_gate": b_gate, "ret_gn_g": ret_gn_g, "ln1_g": ln1_g, "ln1_b": ln1_b, "ln2_g": ln2_g, "ln2_b": ln2_b}
    big_b = {name: a[0].astype(BF16) for name, a in big.items()}

    tabs_p = _decay_tables(RET_CHUNK)
    tabs_p["cos"], tabs_p["sin"] = _rotary_tables(jnp.arange(seq, dtype=jnp.int32), 1)
    tabs_s = _decay_tables(t_len)
    eye = jnp.eye(SAMPLE_SEQS, dtype=F32)
    tabs_s["dmask"] = (eye[None, :, None, :, None] * tabs_s["dmask"][:, None, :, None, :]).reshape(
        RET_HEADS, SAMPLE_SEQS * t_len, SAMPLE_SEQS * t_len)
    tabs_s["kdec"] = jnp.tile(tabs_s["kdec"], (1, SAMPLE_SEQS, 1))
    tabs_s["cos"], tabs_s["sin"] = _rotary_tables(PAST_LEN + jnp.arange(t_len, dtype=jnp.int32), SAMPLE_SEQS)

    mk_f, mv_f, mk_b, mv_b = _mem_kv(mem_prompt, w_mem_kv.astype(BF16))
    mk_b = mk_b.reshape(DEPTH, bp, MEM_LEN, MEM_WIDTH)
    mv_b = mv_b.reshape(DEPTH, bp, MEM_LEN, MEM_WIDTH)

    kc = cache_mem_k.reshape(DEPTH, nseq, MEM_LEN * MEM_HEADS, MEM_HEAD_DIM)
    vc = cache_mem_v.reshape(DEPTH, nseq, MEM_LEN * MEM_HEADS, MEM_HEAD_DIM)
    cst = jnp.concatenate(
        [state_conv, jnp.zeros((DEPTH, nseq, t_len - (CONV_K - 1), CONV_WIDTH), state_conv.dtype)], axis=2
    ).reshape(DEPTH, nseq * t_len, CONV_WIDTH)

    xp = x_prompt
    xs = x_sample.reshape(nseq * t_len, D_MODEL)
    cp_list, rp_list, cs_list = [], [], []
    ret_stack = None
    for l in range(DEPTH):
        w = dict(big_b, conv_w=conv_w[l], **{name: a[l].reshape(1, -1) for name, a in small.items()})
        to_cast = list(big.values()) if l + 1 < DEPTH else []
        xp1, ctail_p, ret_p, *cast = _prompt_mixer(l, xp, tabs_p, mk_b, mv_b, w, to_cast)
        big_b = dict(zip(big, cast))
        cp_list.append(ctail_p[:, 8 - (CONV_K - 1):, :])
        rp_list.append(ret_p)

        proj = _sample_proj(xs, w["w_in"])
        xp, a_pre, b_pre, m_pre, u_s, ret_stack = _ffn_and_sample_state(
            l, xp1.reshape(bp * seq, D_MODEL), w, proj, cst[l], tabs_s, state_ret, kc, vc, ret_stack)
        xp = xp.reshape(bp, seq, D_MODEL)
        xs = _sample_post(xs, a_pre, b_pre, m_pre, proj, w)
        cs_list.append(u_s.reshape(nseq, t_len, CONV_WIDTH)[:, t_len - (CONV_K - 1):, :])

    mem_shape = (DEPTH, bp, MEM_LEN, MEM_HEADS, MEM_HEAD_DIM)
    return (xp, xs.reshape(nseq, t_len, D_MODEL), mk_f.reshape(mem_shape), mv_f.reshape(mem_shape),
            jnp.stack(cp_list), jnp.stack(rp_list), jnp.stack(cs_list), ret_stack)
```

```python
import functools

import jax
import jax.numpy as jnp
from jax import lax
from jax.experimental import pallas as pl
from jax.experimental.pallas import tpu as pltpu

D_MODEL = 1024
DEPTH = 4
CONV_WIDTH = 512
CONV_K = 3
RET_HEADS = 4
RET_DK = 128
RET_DV = 256
RET_QK_WIDTH = RET_HEADS * RET_DK
RET_V_WIDTH = RET_HEADS * RET_DV
RET_CHUNK = 128
MEM_LEN = 256
MEM_HEADS = 4
MEM_HEAD_DIM = 128
MEM_WIDTH = MEM_HEADS * MEM_HEAD_DIM
N_BRANCH = 3
D_FF = 4 * D_MODEL
ROPE_BASE = 10000.0
LN_EPS = 1e-5
GN_EPS = 1e-6
PAST_LEN = 16384
DEEPNORM_ALPHA = (2 * DEPTH) ** 0.25

OFF_CB = 0
OFF_CC = OFF_CB + CONV_WIDTH
OFF_CH = OFF_CC + CONV_WIDTH
OFF_RQ = OFF_CH + CONV_WIDTH
OFF_RK = OFF_RQ + RET_QK_WIDTH
OFF_RV = OFF_RK + RET_QK_WIDTH
OFF_RG = OFF_RV + RET_V_WIDTH
OFF_MQ = OFF_RG + RET_V_WIDTH
OFF_GL = OFF_MQ + MEM_WIDTH
IN_WIDTH = OFF_GL + N_BRANCH * D_MODEL

V7X_VMEM_BYTES = 64 * 1024 * 1024
VMEM_LIMIT_BYTES = V7X_VMEM_BYTES - 6 * 1024 * 1024

PROMPT_TILE = 512
FFN_TILE = 512
FFN_CHUNK = 1024
SAMPLE_SEQS = 4
SAMPLE_POST_TILE = 512

BF16 = jnp.bfloat16
F32 = jnp.float32


def _dot(a, b):
    return jnp.dot(a, b, preferred_element_type=F32)


def _dot_nt(a, b):
    return lax.dot_general(a, b, (((1,), (1,)), ((), ())), preferred_element_type=F32)


def _dot_tn(a, b):
    return lax.dot_general(a, b, (((0,), (0,)), ((), ())), preferred_element_type=F32)


def _layer_norm(z, g, b):
    mu = jnp.mean(z, axis=-1, keepdims=True)
    zc = z - mu
    var = jnp.mean(zc * zc, axis=-1, keepdims=True)
    return zc * lax.rsqrt(var + LN_EPS) * g + b


def _group_norm(o):
    mu = jnp.mean(o, axis=-1, keepdims=True)
    oc = o - mu
    var = jnp.mean(oc * oc, axis=-1, keepdims=True)
    return oc * lax.rsqrt(var + GN_EPS)


def _rotary(xh, cos, sin_signed):
    return xh * cos + pltpu.roll(xh, RET_DK // 2, axis=1) * sin_signed


def _softmax_rows(s):
    m = jnp.max(s, axis=-1, keepdims=True)
    e = jnp.exp(s - m)
    return e / jnp.sum(e, axis=-1, keepdims=True)


def _merge_out_ln(x_ref, pre_refs, gate_logits, bg_ref, w_refs, wo_ref, g_ref, b_ref, y_ref):
    merged = None
    for i, (pre_ref, w_ref) in enumerate(zip(pre_refs, w_refs)):
        gate = jax.nn.sigmoid(gate_logits(i) + bg_ref[:, i * D_MODEL:(i + 1) * D_MODEL])
        term = gate * _dot(pre_ref[...], w_ref[...])
        merged = term if merged is None else merged + term
    merged = merged.astype(BF16)
    half = merged.shape[0] // 2
    for rows in (slice(0, half), slice(half, 2 * half)):
        z = DEEPNORM_ALPHA * x_ref[rows, :] + _dot(merged[rows], wo_ref[...])
        y_ref[rows, :] = _layer_norm(z, g_ref[...], b_ref[...])


def _ffn_ln(x, wup_ref, wdn_ref, g_ref, b_ref):
    xb = x.astype(BF16)
    acc = None
    for c in range(D_FF // FFN_CHUNK):
        cols = slice(c * FFN_CHUNK, (c + 1) * FFN_CHUNK)
        h = jnp.maximum(_dot(xb, wup_ref[:, cols]), 0.0)
        part = _dot((h * h).astype(BF16), wdn_ref[cols, :])
        acc = part if acc is None else acc + part
    return _layer_norm(DEEPNORM_ALPHA * x + acc, g_ref[...], b_ref[...])


MEM_KV_ROWS = 512
MEM_KV_OUTPUTS = 4


def _mem_kv_kernel(n_cast, mem_ref, w_ref, *refs):
    cast_src = refs[:n_cast]
    k_ref, v_ref, kb_ref, vb_ref = refs[n_cast:n_cast + MEM_KV_OUTPUTS]
    cast_dst = refs[n_cast + MEM_KV_OUTPUTS:n_cast + MEM_KV_OUTPUTS + n_cast]
    wb_ref = refs[-1]
    j = pl.program_id(1)

    for src_ref, dst_ref in zip(cast_src, cast_dst):
        dst_ref[...] = src_ref[...].astype(BF16)

    @pl.when(j == 0)
    def _():
        wb_ref[...] = w_ref[...].astype(BF16)

    rows = pl.ds(pl.multiple_of(j * MEM_KV_ROWS, MEM_KV_ROWS), MEM_KV_ROWS)
    kv = _dot(mem_ref[rows, :].astype(BF16), wb_ref[...])
    kb_ref[...] = kv[:, :MEM_WIDTH].astype(BF16)
    vb_ref[...] = kv[:, MEM_WIDTH:].astype(BF16)
    for h in range(MEM_HEADS):
        head_rows = pl.ds(h, MEM_KV_ROWS, stride=MEM_HEADS)
        k_ref[head_rows, :] = kv[:, h * MEM_HEAD_DIM:(h + 1) * MEM_HEAD_DIM]
        v_ref[head_rows, :] = kv[:, MEM_WIDTH + h * MEM_HEAD_DIM:MEM_WIDTH + (h + 1) * MEM_HEAD_DIM]


def _mem_kv(mem_prompt, w_mem_kv, to_cast):
    bp = mem_prompt.shape[0]
    n = bp * MEM_LEN
    nsub = n // MEM_KV_ROWS
    steps = DEPTH * nsub
    assert nsub * MEM_KV_ROWS == n
    cast_in_specs, cast_out_specs, cast_out_shapes = [], [], []
    for a in to_cast:
        _, rows, cols = a.shape
        slab = rows // steps
        assert slab * steps == rows and slab % 16 == 0
        cast_in_specs.append(pl.BlockSpec((None, slab, cols), lambda l, j: (0, l * nsub + j, 0)))
        cast_out_specs.append(pl.BlockSpec((slab, cols), lambda l, j: (l * nsub + j, 0)))
        cast_out_shapes.append(jax.ShapeDtypeStruct((rows, cols), BF16))
    out_f = jax.ShapeDtypeStruct((DEPTH, n * MEM_HEADS, MEM_HEAD_DIM), F32)
    out_b = jax.ShapeDtypeStruct((DEPTH, n, MEM_WIDTH), BF16)
    f_spec = pl.BlockSpec((None, MEM_KV_ROWS * MEM_HEADS, MEM_HEAD_DIM), lambda l, j: (l, j, 0))
    b_spec = pl.BlockSpec((None, MEM_KV_ROWS, MEM_WIDTH), lambda l, j: (l, j, 0))
    return pl.pallas_call(
        functools.partial(_mem_kv_kernel, len(to_cast)),
        grid=(DEPTH, nsub),
        in_specs=[
            _const((n, D_MODEL)),
            pl.BlockSpec((None, D_MODEL, 2 * MEM_WIDTH), lambda l, j: (l, 0, 0)),
        ] + cast_in_specs,
        out_specs=[f_spec, f_spec, b_spec, b_spec] + cast_out_specs,
        out_shape=[out_f, out_f, out_b, out_b] + cast_out_shapes,
        scratch_shapes=[pltpu.VMEM((D_MODEL, 2 * MEM_WIDTH), BF16)],
        compiler_params=pltpu.CompilerParams(
            dimension_semantics=("arbitrary", "arbitrary"), vmem_limit_bytes=VMEM_LIMIT_BYTES),
        name="mem_kv",
    )(mem_prompt.reshape(n, D_MODEL), w_mem_kv, *to_cast)


MIXER_INPUTS = 19
MIXER_OUTPUTS = 3


def _prompt_mixer_kernel(aliased, n_cast, *refs):
    if aliased:
        refs = refs[1:]
    (cd_ref, x_ref, cos_ref, sin_ref, dmask_ref, qdec_ref, kdec_ref, mk_ref, mv_ref,
     win_ref, bg_ref, cw_ref, gn_ref, wco_ref, wro_ref, wmo_ref, wo_ref, lg_ref, lb_ref) = refs[:MIXER_INPUTS]
    cast_src = refs[MIXER_INPUTS:MIXER_INPUTS + n_cast]
    outs = refs[MIXER_INPUTS + n_cast:]
    y_ref, ctail_ref, ret_ref = outs[:MIXER_OUTPUTS]
    cast_dst = outs[MIXER_OUTPUTS:MIXER_OUTPUTS + n_cast]
    (xb_ref, q_ref, k_ref, kd_ref, vb_ref, msk_ref, kv_ref,
     abuf_ref, bbuf_ref, mbuf_ref) = outs[MIXER_OUTPUTS + n_cast:]

    for src_ref, dst_ref in zip(cast_src, cast_dst):
        dst_ref[...] = src_ref[...].astype(BF16)

    t = pl.program_id(1)
    tq = x_ref.shape[0]
    chunks = [slice(j * RET_CHUNK, (j + 1) * RET_CHUNK) for j in range(tq // RET_CHUNK)]
    kcols = [slice(h * RET_DK, (h + 1) * RET_DK) for h in range(RET_HEADS)]
    vcols = [slice(h * RET_DV, (h + 1) * RET_DV) for h in range(RET_HEADS)]
    mcols = [slice(h * MEM_HEAD_DIM, (h + 1) * MEM_HEAD_DIM) for h in range(MEM_HEADS)]

    @pl.when(t == 0)
    def _():
        ret_ref[...] = jnp.zeros_like(ret_ref)
        ctail_ref[...] = jnp.zeros_like(ctail_ref)

    xb_ref[...] = x_ref[...].astype(BF16)

    def proj(lo, width):
        return _dot(xb_ref[...], win_ref[:, lo:lo + width])

    pq = proj(OFF_RQ, RET_QK_WIDTH)
    pk = proj(OFF_RK, RET_QK_WIDTH)
    vb_ref[...] = proj(OFF_RV, RET_V_WIDTH).astype(BF16)
    cos = cos_ref[...]
    sin = sin_ref[...]
    for h in range(RET_HEADS):
        q_ref[:, kcols[h]] = _rotary(pq[:, kcols[h]], cos, sin).astype(BF16)
        kh = _rotary(pk[:, kcols[h]], cos, sin) * (RET_DK ** -0.5)
        k_ref[:, kcols[h]] = kh.astype(BF16)
        for r in chunks:
            kd_ref[r, kcols[h]] = (kh[r] * kdec_ref[h]).astype(BF16)

    pm = proj(OFF_MQ, MEM_WIDTH).astype(BF16)
    mem_s = [_dot_nt(pm[:, mcols[h]], mk_ref[:, mcols[h]]) * (MEM_HEAD_DIM ** -0.5) for h in range(MEM_HEADS)]

    cb = proj(OFF_CB, CONV_WIDTH)
    u = proj(OFF_CC, CONV_WIDTH) * proj(OFF_CH, CONV_WIDTH)
    row = lax.broadcasted_iota(jnp.int32, u.shape, 0)
    prev1 = ctail_ref[7:8, :]
    prev2 = ctail_ref[6:7, :]
    u1 = jnp.where(row == 0, prev1, pltpu.roll(u, 1, axis=0))
    u2 = jnp.where(row == 0, prev2, jnp.where(row == 1, prev1, pltpu.roll(u, 2, axis=0)))
    conv_y = u2 * cw_ref[0:1, :] + u1 * cw_ref[1:2, :] + u * cw_ref[2:3, :]
    ctail_ref[...] = u[tq - 8:, :]
    abuf_ref[...] = (cb * conv_y).astype(BF16)

    for h in range(MEM_HEADS):
        p = _softmax_rows(mem_s[h])
        mbuf_ref[:, mcols[h]] = _dot(p.astype(BF16), mv_ref[:, mcols[h]]).astype(BF16)

    for h in range(RET_HEADS):
        for r in chunks:
            msk_ref[r, kcols[h]] = (_dot_nt(q_ref[r, kcols[h]], k_ref[r, kcols[h]]) * dmask_ref[h]).astype(BF16)

    for h in range(RET_HEADS):
        for j, r in enumerate(chunks):
            kv_ref[h, j] = _dot_tn(kd_ref[r, kcols[h]], vb_ref[r, vcols[h]])

    for h in range(RET_HEADS):
        pg = proj(OFF_RG + h * RET_DV, RET_DV)
        s_cur = ret_ref[h]
        for j, r in enumerate(chunks):
            intra = _dot(msk_ref[r, kcols[h]], vb_ref[r, vcols[h]])
            o = intra + _dot(q_ref[r, kcols[h]], s_cur.astype(BF16)) * qdec_ref[h]
            gated = jax.nn.silu(pg[r]) * (_group_norm(o) * gn_ref[:, vcols[h]])
            bbuf_ref[r, vcols[h]] = gated.astype(BF16)
            s_cur = s_cur * cd_ref[h] + kv_ref[h, j]
        ret_ref[h] = s_cur

    _merge_out_ln(x_ref, (abuf_ref, bbuf_ref, mbuf_ref), lambda i: proj(OFF_GL + i * D_MODEL, D_MODEL), bg_ref,
                  (wco_ref, wro_ref, wmo_ref), wo_ref, lg_ref, lb_ref, y_ref)


def _const(shape):
    nd = len(shape)
    return pl.BlockSpec(tuple(shape), lambda *_: (0,) * nd, pipeline_mode=pl.Buffered(1))


def _prompt_mixer(layer, x, tabs, mk_b, mv_b, w, to_cast, ret_stack):
    bp, seq, _ = x.shape
    tq = PROMPT_TILE
    grid = (bp, seq // tq)
    steps = bp * (seq // tq)
    steps_per_seq = seq // tq
    cast_in_specs, cast_out_specs, cast_out_shapes = [], [], []
    for a in to_cast:
        _, rows, cols = a.shape
        slab = rows // steps
        assert slab * steps == rows and slab % 16 == 0
        cast_in_specs.append(pl.BlockSpec((None, slab, cols), lambda b, t: (layer + 1, b * steps_per_seq + t, 0)))
        cast_out_specs.append(pl.BlockSpec((slab, cols), lambda b, t: (b * steps_per_seq + t, 0)))
        cast_out_shapes.append(jax.ShapeDtypeStruct((rows, cols), BF16))
    in_specs = [
        pl.BlockSpec(memory_space=pltpu.SMEM),
        pl.BlockSpec((None, tq, D_MODEL), lambda b, t: (b, t, 0)),
        pl.BlockSpec((tq, RET_DK), lambda b, t: (t, 0)),
        pl.BlockSpec((tq, RET_DK), lambda b, t: (t, 0)),
        _const((RET_HEADS, RET_CHUNK, RET_CHUNK)),
        _const((RET_HEADS, RET_CHUNK, RET_DV)),
        _const((RET_HEADS, RET_CHUNK, RET_DK)),
        pl.BlockSpec((None, None, MEM_LEN, MEM_WIDTH), lambda b, t: (layer, b, 0, 0)),
        pl.BlockSpec((None, None, MEM_LEN, MEM_WIDTH), lambda b, t: (layer, b, 0, 0)),
        _const((D_MODEL, IN_WIDTH)),
        _const((1, N_BRANCH * D_MODEL)),
        _const((CONV_K, CONV_WIDTH)),
        _const((1, RET_V_WIDTH)),
        _const((CONV_WIDTH, D_MODEL)),
        _const((RET_V_WIDTH, D_MODEL)),
        _const((MEM_WIDTH, D_MODEL)),
        _const((D_MODEL, D_MODEL)),
        _const((1, D_MODEL)),
        _const((1, D_MODEL)),
    ]
    out_specs = [
        pl.BlockSpec((None, tq, D_MODEL), lambda b, t: (b, t, 0)),
        pl.BlockSpec((None, 8, CONV_WIDTH), lambda b, t: (b, 0, 0)),
        pl.BlockSpec((None, None, RET_HEADS, RET_DK, RET_DV), lambda b, t: (layer, b, 0, 0, 0)),
    ]
    out_shape = [
        jax.ShapeDtypeStruct((bp, seq, D_MODEL), F32),
        jax.ShapeDtypeStruct((bp, 8, CONV_WIDTH), F32),
        jax.ShapeDtypeStruct((DEPTH, bp, RET_HEADS, RET_DK, RET_DV), F32),
    ]
    assert len(in_specs) == MIXER_INPUTS and len(out_specs) == MIXER_OUTPUTS
    args = (tabs["cd"], x, tabs["cos"], tabs["sin"], tabs["dmask"], tabs["qdec"], tabs["kdec"], mk_b, mv_b,
            w["w_in"], w["b_gate"], w["conv_w"], w["ret_gn_g"], w["w_conv_out"], w["w_ret_out"], w["w_mem_out"],
            w["w_out"], w["ln1_g"], w["ln1_b"], *to_cast)
    in_specs = in_specs + cast_in_specs
    aliases = {}
    if ret_stack is not None:
        aliases = {0: MIXER_OUTPUTS - 1}
        in_specs = [pl.BlockSpec(memory_space=pl.ANY)] + in_specs
        args = (ret_stack,) + args
    return pl.pallas_call(
        functools.partial(_prompt_mixer_kernel, ret_stack is not None, len(to_cast)),
        grid=grid,
        in_specs=in_specs,
        out_specs=out_specs + cast_out_specs,
        out_shape=out_shape + cast_out_shapes,
        scratch_shapes=[
            pltpu.VMEM((tq, D_MODEL), BF16),
            pltpu.VMEM((tq, RET_QK_WIDTH), BF16),
            pltpu.VMEM((tq, RET_QK_WIDTH), BF16),
            pltpu.VMEM((tq, RET_QK_WIDTH), BF16),
            pltpu.VMEM((tq, RET_V_WIDTH), BF16),
            pltpu.VMEM((tq, RET_QK_WIDTH), BF16),
            pltpu.VMEM((RET_HEADS, tq // RET_CHUNK, RET_DK, RET_DV), F32),
            pltpu.VMEM((tq, CONV_WIDTH), BF16),
            pltpu.VMEM((tq, RET_V_WIDTH), BF16),
            pltpu.VMEM((tq, MEM_WIDTH), BF16),
        ],
        input_output_aliases=aliases,
        compiler_params=pltpu.CompilerParams(
            dimension_semantics=("arbitrary", "arbitrary"), vmem_limit_bytes=VMEM_LIMIT_BYTES),
        name="prompt_mixer",
    )(*args)


FFN_INPUTS = 5
STATE_INPUTS = 13
STATE_OUTPUTS = 5


def _ffn_state_kernel(aliased, *refs):
    if aliased:
        refs = refs[1:]
    x_ref, wup_ref, wdn_ref, g_ref, b_ref = refs[:FFN_INPUTS]
    state_in = refs[FFN_INPUTS:FFN_INPUTS + STATE_INPUTS]
    y_ref = refs[FFN_INPUTS + STATE_INPUTS]
    rest = refs[FFN_INPUTS + STATE_INPUTS + 1:]
    stages = _sample_state_stages(*state_in, *rest)
    n_chunks = D_FF // FFN_CHUNK
    assert len(stages) == 2 * n_chunks - 1

    x = x_ref[...]
    xb = x.astype(BF16)
    acc = None
    for c in range(n_chunks):
        cols = slice(c * FFN_CHUNK, (c + 1) * FFN_CHUNK)
        h = jnp.maximum(_dot(xb, wup_ref[:, cols]), 0.0)
        stages[2 * c]()
        part = _dot((h * h).astype(BF16), wdn_ref[cols, :])
        if c + 1 < n_chunks:
            stages[2 * c + 1]()
        acc = part if acc is None else acc + part
    y_ref[...] = _layer_norm(DEEPNORM_ALPHA * x + acc, g_ref[...], b_ref[...])


def _ffn_and_sample_state(layer, x, w, proj, cst, tabs, state_ret, kc, vc, ret_stack):
    n = x.shape[0]
    tm = FFN_TILE
    ns = SAMPLE_SEQS
    n_tok = proj.shape[0]
    nseq = state_ret.shape[1]
    t_len = n_tok // nseq
    rows = ns * t_len
    assert n // tm == nseq // ns
    tok_spec = lambda width: pl.BlockSpec((rows, width), lambda i: (i, 0))
    in_specs = [
        pl.BlockSpec((tm, D_MODEL), lambda i: (i, 0)),
        _const((D_MODEL, D_FF)),
        _const((D_FF, D_MODEL)),
        _const((1, D_MODEL)),
        _const((1, D_MODEL)),
        pl.BlockSpec(memory_space=pltpu.SMEM),
        tok_spec(OFF_GL),
        tok_spec(CONV_WIDTH),
        _const((rows, RET_DK)),
        _const((rows, RET_DK)),
        _const((RET_HEADS, rows, rows)),
        _const((RET_HEADS, t_len, RET_DV)),
        _const((RET_HEADS, rows, RET_DK)),
        pl.BlockSpec((None, ns, RET_HEADS, RET_DK, RET_DV), lambda i: (layer, i, 0, 0, 0)),
        pl.BlockSpec((None, ns, MEM_LEN * MEM_HEADS, MEM_HEAD_DIM), lambda i: (layer, i, 0, 0)),
        pl.BlockSpec((None, ns, MEM_LEN * MEM_HEADS, MEM_HEAD_DIM), lambda i: (layer, i, 0, 0)),
        _const((CONV_K, CONV_WIDTH)),
        _const((1, RET_V_WIDTH)),
    ]
    out_specs = [
        pl.BlockSpec((tm, D_MODEL), lambda i: (i, 0)),
        tok_spec(CONV_WIDTH),
        tok_spec(RET_V_WIDTH),
        tok_spec(MEM_WIDTH),
        tok_spec(CONV_WIDTH),
        pl.BlockSpec((None, ns, RET_HEADS, RET_DK, RET_DV), lambda i: (layer, i, 0, 0, 0)),
    ]
    out_shape = [
        jax.ShapeDtypeStruct((n, D_MODEL), F32),
        jax.ShapeDtypeStruct((n_tok, CONV_WIDTH), BF16),
        jax.ShapeDtypeStruct((n_tok, RET_V_WIDTH), BF16),
        jax.ShapeDtypeStruct((n_tok, MEM_WIDTH), BF16),
        jax.ShapeDtypeStruct((n_tok, CONV_WIDTH), F32),
        jax.ShapeDtypeStruct((DEPTH, nseq, RET_HEADS, RET_DK, RET_DV), F32),
    ]
    assert len(in_specs) == FFN_INPUTS + STATE_INPUTS and len(out_specs) == 1 + STATE_OUTPUTS
    args = (x, w["w_up"], w["w_down"], w["ln2_g"], w["ln2_b"],
            tabs["cd"], proj, cst, tabs["cos"], tabs["sin"], tabs["dmask"], tabs["qdec"], tabs["kdec"],
            state_ret, kc, vc, w["conv_w"], w["ret_gn_g"])
    aliases = {}
    if ret_stack is not None:
        aliases = {0: len(out_shape) - 1}
        in_specs = [pl.BlockSpec(memory_space=pl.ANY)] + in_specs
        args = (ret_stack,) + args
    scratch = [
        pltpu.VMEM((rows, RET_V_WIDTH), F32),
        pltpu.VMEM((rows, MEM_HEADS * MEM_LEN), F32),
        pltpu.VMEM((rows, MEM_WIDTH), F32),
    ]
    return pl.pallas_call(
        functools.partial(_ffn_state_kernel, ret_stack is not None),
        grid=(n // tm,),
        in_specs=in_specs,
        out_specs=out_specs,
        out_shape=out_shape,
        scratch_shapes=scratch,
        input_output_aliases=aliases,
        compiler_params=pltpu.CompilerParams(
            dimension_semantics=("arbitrary",), vmem_limit_bytes=VMEM_LIMIT_BYTES),
        name="ffn_state",
    )(*args)


def _sample_proj_kernel(x_ref, w_ref, o_ref):
    o_ref[...] = _dot(x_ref[...].astype(BF16), w_ref[...])


def _sample_proj(x, w_in_b):
    n = x.shape[0]
    tn = 1024
    return pl.pallas_call(
        _sample_proj_kernel,
        grid=(IN_WIDTH // tn,),
        in_specs=[
            pl.BlockSpec((n, D_MODEL), lambda j: (0, 0)),
            pl.BlockSpec((D_MODEL, tn), lambda j: (0, j)),
        ],
        out_specs=pl.BlockSpec((n, tn), lambda j: (0, j)),
        out_shape=jax.ShapeDtypeStruct((n, IN_WIDTH), F32),
        compiler_params=pltpu.CompilerParams(
            dimension_semantics=("arbitrary",), vmem_limit_bytes=VMEM_LIMIT_BYTES),
        name="sample_proj",
    )(x, w_in_b)


def _sample_state_stages(cd_ref, proj_ref, cst_ref, cos_ref, sin_ref, dmask_ref, qdec_ref, kdec_ref,
                         sret_ref, kc_ref, vc_ref, cw_ref, gn_ref,
                         a_ref, b_ref, m_ref, u_ref, nret_ref,
                         o_scr, p_scr, m_scr):
    ns = sret_ref.shape[0]
    t_len = proj_ref.shape[0] // ns
    rows = [slice(s * t_len, (s + 1) * t_len) for s in range(ns)]
    kcols = [slice(h * RET_DK, (h + 1) * RET_DK) for h in range(RET_HEADS)]
    vcols = [slice(h * RET_DV, (h + 1) * RET_DV) for h in range(RET_HEADS)]
    mcols = [slice(h * MEM_HEAD_DIM, (h + 1) * MEM_HEAD_DIM) for h in range(MEM_HEADS)]
    pcols = [slice(h * MEM_LEN, (h + 1) * MEM_LEN) for h in range(MEM_HEADS)]
    head_rows = [pl.ds(h, MEM_LEN, stride=MEM_HEADS) for h in range(MEM_HEADS)]

    q, kd, masked = [], [], []

    def conv_and_scores():
        cb = proj_ref[:, OFF_CB:OFF_CB + CONV_WIDTH]
        u = proj_ref[:, OFF_CC:OFF_CC + CONV_WIDTH] * proj_ref[:, OFF_CH:OFF_CH + CONV_WIDTH]
        tok = lax.broadcasted_iota(jnp.int32, u.shape, 0) & (t_len - 1)
        st2 = cst_ref[...]
        st1 = pltpu.roll(st2, st2.shape[0] - 1, axis=0)
        u1 = jnp.where(tok == 0, st1, pltpu.roll(u, 1, axis=0))
        u2 = jnp.where(tok < 2, st2, pltpu.roll(u, 2, axis=0))
        conv_y = u2 * cw_ref[0:1, :] + u1 * cw_ref[1:2, :] + u * cw_ref[2:3, :]
        u_ref[...] = u
        a_ref[...] = (cb * conv_y).astype(BF16)

        cos = cos_ref[...]
        sin = sin_ref[...]
        for h in range(RET_HEADS):
            qh = _rotary(proj_ref[:, OFF_RQ + h * RET_DK:OFF_RQ + (h + 1) * RET_DK], cos, sin)
            kh = _rotary(proj_ref[:, OFF_RK + h * RET_DK:OFF_RK + (h + 1) * RET_DK], cos, sin) * (RET_DK ** -0.5)
            q.append(qh)
            kd.append(kh * kdec_ref[h])
            masked.append((_dot_nt(qh.astype(BF16), kh.astype(BF16)) * dmask_ref[h]).astype(BF16))

    def intra_chunk():
        for h in range(RET_HEADS):
            vb = proj_ref[:, OFF_RV + h * RET_DV:OFF_RV + (h + 1) * RET_DV].astype(BF16)
            o_scr[:, vcols[h]] = _dot(masked[h], vb)

    def cross_chunk_and_state(seqs):
        for s in seqs:
            for h in range(RET_HEADS):
                s_prev = sret_ref[s, h]
                inter = _dot(q[h][rows[s]].astype(BF16), s_prev.astype(BF16))
                o_scr[rows[s], vcols[h]] += inter * qdec_ref[h]
                vb = proj_ref[rows[s], OFF_RV + h * RET_DV:OFF_RV + (h + 1) * RET_DV].astype(BF16)
                nret_ref[s, h] = s_prev * cd_ref[h] + _dot_tn(kd[h][rows[s]].astype(BF16), vb)

    def attention_scores():
        for s in range(ns):
            for h in range(MEM_HEADS):
                mq = proj_ref[rows[s], OFF_MQ + h * MEM_HEAD_DIM:OFF_MQ + (h + 1) * MEM_HEAD_DIM].astype(BF16)
                p_scr[rows[s], pcols[h]] = (_dot_nt(mq, kc_ref[s, head_rows[h], :].astype(BF16))
                                            * (MEM_HEAD_DIM ** -0.5))

    def attention_values():
        for h in range(MEM_HEADS):
            p_scr[:, pcols[h]] = _softmax_rows(p_scr[:, pcols[h]])
        for s in range(ns):
            for h in range(MEM_HEADS):
                p = p_scr[rows[s], pcols[h]].astype(BF16)
                m_scr[rows[s], mcols[h]] = _dot(p, vc_ref[s, head_rows[h], :].astype(BF16))
        m_ref[...] = m_scr[...].astype(BF16)

    def norm_and_gate():
        for h in range(RET_HEADS):
            rg = proj_ref[:, OFF_RG + h * RET_DV:OFF_RG + (h + 1) * RET_DV]
            gated = jax.nn.silu(rg) * (_group_norm(o_scr[:, vcols[h]]) * gn_ref[:, vcols[h]])
            b_ref[:, vcols[h]] = gated.astype(BF16)

    first, second = list(range(ns // 2)), list(range(ns // 2, ns))
    return [conv_and_scores, intra_chunk,
            functools.partial(cross_chunk_and_state, first), functools.partial(cross_chunk_and_state, second),
            attention_scores, attention_values, norm_and_gate]


def _sample_post_kernel(x_ref, a_ref, b_ref, m_ref, g0_ref, g1_ref, g2_ref,
                        bg_ref, wco_ref, wro_ref, wmo_ref, wo_ref, l1g_ref, l1b_ref,
                        wup_ref, wdn_ref, l2g_ref, l2b_ref, y_ref, x1_ref):
    gate_refs = (g0_ref, g1_ref, g2_ref)
    _merge_out_ln(x_ref, (a_ref, b_ref, m_ref), lambda i: gate_refs[i][...], bg_ref,
                  (wco_ref, wro_ref, wmo_ref), wo_ref, l1g_ref, l1b_ref, x1_ref)
    y_ref[...] = _ffn_ln(x1_ref[...], wup_ref, wdn_ref, l2g_ref, l2b_ref)


def _sample_post(x, a_pre, b_pre, m_pre, proj, w):
    n = x.shape[0]
    tm = SAMPLE_POST_TILE
    tok = lambda width: pl.BlockSpec((tm, width), lambda i: (i, 0))
    gate = lambda k: pl.BlockSpec((tm, D_MODEL), lambda i: (i, OFF_GL // D_MODEL + k))
    return pl.pallas_call(
        _sample_post_kernel,
        grid=(n // tm,),
        in_specs=[
            tok(D_MODEL), tok(CONV_WIDTH), tok(RET_V_WIDTH), tok(MEM_WIDTH), gate(0), gate(1), gate(2),
            _const((1, N_BRANCH * D_MODEL)),
            _const((CONV_WIDTH, D_MODEL)),
            _const((RET_V_WIDTH, D_MODEL)),
            _const((MEM_WIDTH, D_MODEL)),
            _const((D_MODEL, D_MODEL)),
            _const((1, D_MODEL)),
            _const((1, D_MODEL)),
            _const((D_MODEL, D_FF)),
            _const((D_FF, D_MODEL)),
            _const((1, D_MODEL)),
            _const((1, D_MODEL)),
        ],
        out_specs=tok(D_MODEL),
        out_shape=jax.ShapeDtypeStruct((n, D_MODEL), F32),
        scratch_shapes=[pltpu.VMEM((tm, D_MODEL), F32)],
        compiler_params=pltpu.CompilerParams(
            dimension_semantics=("arbitrary",), vmem_limit_bytes=VMEM_LIMIT_BYTES),
        name="sample_post",
    )(x, a_pre, b_pre, m_pre, proj, proj, proj,
      w["b_gate"], w["w_conv_out"], w["w_ret_out"], w["w_mem_out"], w["w_out"], w["ln1_g"], w["ln1_b"],
      w["w_up"], w["w_down"], w["ln2_g"], w["ln2_b"])


def _rotary_tables(pos, reps):
    half = RET_DK // 2
    inv = ROPE_BASE ** (-jnp.arange(half, dtype=F32) / half)
    ang = pos.astype(F32)[:, None] * inv[None, :]
    cos = jnp.cos(ang)
    sin = jnp.sin(ang)
    cos_full = jnp.concatenate([cos, cos], axis=-1)
    sin_signed = jnp.concatenate([-sin, sin], axis=-1)
    return jnp.tile(cos_full, (reps, 1)), jnp.tile(sin_signed, (reps, 1))


def _decay_tables(c):
    h = RET_HEADS
    log_g = jnp.log1p(-jnp.exp2(-5.0 - jnp.arange(h, dtype=F32)))
    idx = jnp.arange(c, dtype=F32)
    rel = idx[:, None] - idx[None, :]
    causal = rel >= 0
    dmask = jnp.where(causal[None], jnp.exp(jnp.where(causal, rel, 0.0)[None] * log_g[:, None, None]), 0.0)
    q_decay = jnp.exp((idx + 1.0)[None, :] * log_g[:, None])
    k_decay = jnp.exp((c - 1.0 - idx)[None, :] * log_g[:, None])
    chunk_decay = jnp.exp(c * log_g)
    return {
        "dmask": dmask,
        "qdec": jnp.broadcast_to(q_decay[:, :, None], (h, c, RET_DV)),
        "kdec": jnp.broadcast_to(k_decay[:, :, None], (h, c, RET_DK)),
        "cd": chunk_decay,
    }


def kernel(x_prompt, x_sample, cache_mem_k, cache_mem_v, state_conv, state_ret, mem_prompt,
           w_in, b_gate, conv_w, ret_gn_g, w_conv_out, w_ret_out, w_mem_out, w_out, w_mem_kv,
           ln1_g, ln1_b, w_up, w_down, ln2_g, ln2_b):
    bp, seq, _ = x_prompt.shape
    nseq, t_len, _ = x_sample.shape
    assert seq % PROMPT_TILE == 0 and PROMPT_TILE % RET_CHUNK == 0
    assert t_len == 8 and nseq % SAMPLE_SEQS == 0 and t_len % RET_CHUNK != 0

    big = {"w_in": w_in, "w_conv_out": w_conv_out, "w_ret_out": w_ret_out, "w_mem_out": w_mem_out,
           "w_out": w_out, "w_up": w_up, "w_down": w_down}
    small = {"b_gate": b_gate, "ret_gn_g": ret_gn_g, "ln1_g": ln1_g, "ln1_b": ln1_b, "ln2_g": ln2_g, "ln2_b": ln2_b}

    tabs_p = _decay_tables(RET_CHUNK)
    tabs_p["cos"], tabs_p["sin"] = _rotary_tables(jnp.arange(seq, dtype=jnp.int32), 1)
    tabs_s = _decay_tables(t_len)
    eye = jnp.eye(SAMPLE_SEQS, dtype=F32)
    tabs_s["dmask"] = (eye[None, :, None, :, None] * tabs_s["dmask"][:, None, :, None, :]).reshape(
        RET_HEADS, SAMPLE_SEQS * t_len, SAMPLE_SEQS * t_len)
    tabs_s["kdec"] = jnp.tile(tabs_s["kdec"], (1, SAMPLE_SEQS, 1))
    tabs_s["cos"], tabs_s["sin"] = _rotary_tables(PAST_LEN + jnp.arange(t_len, dtype=jnp.int32), SAMPLE_SEQS)

    mk_f, mv_f, mk_b, mv_b, *cast = _mem_kv(mem_prompt, w_mem_kv, list(big.values()))
    big_b = dict(zip(big, cast))
    mk_b = mk_b.reshape(DEPTH, bp, MEM_LEN, MEM_WIDTH)
    mv_b = mv_b.reshape(DEPTH, bp, MEM_LEN, MEM_WIDTH)

    kc = cache_mem_k.reshape(DEPTH, nseq, MEM_LEN * MEM_HEADS, MEM_HEAD_DIM)
    vc = cache_mem_v.reshape(DEPTH, nseq, MEM_LEN * MEM_HEADS, MEM_HEAD_DIM)
    cst = jnp.concatenate(
        [state_conv, jnp.zeros((DEPTH, nseq, t_len - (CONV_K - 1), CONV_WIDTH), state_conv.dtype)], axis=2
    ).reshape(DEPTH, nseq * t_len, CONV_WIDTH)

    xp = x_prompt
    xs = x_sample.reshape(nseq * t_len, D_MODEL)
    cp_list, cs_list = [], []
    ret_stack = ret_stack_p = None
    for l in range(DEPTH):
        w = dict(big_b, conv_w=conv_w[l], **{name: a[l].reshape(1, -1) for name, a in small.items()})
        to_cast = list(big.values()) if l + 1 < DEPTH else []
        xp1, ctail_p, ret_stack_p, *cast = _prompt_mixer(l, xp, tabs_p, mk_b, mv_b, w, to_cast, ret_stack_p)
        big_b = dict(zip(big, cast))
        cp_list.append(ctail_p[:, 8 - (CONV_K - 1):, :])

        proj = _sample_proj(xs, w["w_in"])
        xp, a_pre, b_pre, m_pre, u_s, ret_stack = _ffn_and_sample_state(
            l, xp1.reshape(bp * seq, D_MODEL), w, proj, cst[l], tabs_s, state_ret, kc, vc, ret_stack)
        xp = xp.reshape(bp, seq, D_MODEL)
        xs = _sample_post(xs, a_pre, b_pre, m_pre, proj, w)
        cs_list.append(u_s.reshape(nseq, t_len, CONV_WIDTH)[:, t_len - (CONV_K - 1):, :])

    mem_shape = (DEPTH, bp, MEM_LEN, MEM_HEADS, MEM_HEAD_DIM)
    return (xp, xs.reshape(nseq, t_len, D_MODEL), mk_f.reshape(mem_shape), mv_f.reshape(mem_shape),
            jnp.stack(cp_list), ret_stack_p, jnp.stack(cs_list), ret_stack)
```

```python
import functools

import jax
import jax.numpy as jnp
from jax import lax
from jax.experimental import pallas as pl
from jax.experimental.pallas import tpu as pltpu

D_MODEL = 1024
DEPTH = 4
CONV_WIDTH = 512
CONV_K = 3
RET_HEADS = 4
RET_DK = 128
RET_DV = 256
RET_QK_WIDTH = RET_HEADS * RET_DK
RET_V_WIDTH = RET_HEADS * RET_DV
RET_CHUNK = 128
MEM_LEN = 256
MEM_HEADS = 4
MEM_HEAD_DIM = 128
MEM_WIDTH = MEM_HEADS * MEM_HEAD_DIM
N_BRANCH = 3
D_FF = 4 * D_MODEL
ROPE_BASE = 10000.0
LN_EPS = 1e-5
GN_EPS = 1e-6
PAST_LEN = 16384
DEEPNORM_ALPHA = (2 * DEPTH) ** 0.25

OFF_CB = 0
OFF_CC = OFF_CB + CONV_WIDTH
OFF_CH = OFF_CC + CONV_WIDTH
OFF_RQ = OFF_CH + CONV_WIDTH
OFF_RK = OFF_RQ + RET_QK_WIDTH
OFF_RV = OFF_RK + RET_QK_WIDTH
OFF_RG = OFF_RV + RET_V_WIDTH
OFF_MQ = OFF_RG + RET_V_WIDTH
OFF_GL = OFF_MQ + MEM_WIDTH
IN_WIDTH = OFF_GL + N_BRANCH * D_MODEL

V7X_VMEM_BYTES = 64 * 1024 * 1024
VMEM_LIMIT_BYTES = V7X_VMEM_BYTES - 6 * 1024 * 1024

PROMPT_TILE = 512
FFN_TILE = 512
FFN_CHUNK = 1024
SAMPLE_SEQS = 4
SAMPLE_POST_TILE = 512

BF16 = jnp.bfloat16
F32 = jnp.float32


def _dot(a, b):
    return jnp.dot(a, b, preferred_element_type=F32)


def _dot_nt(a, b):
    return lax.dot_general(a, b, (((1,), (1,)), ((), ())), preferred_element_type=F32)


def _dot_tn(a, b):
    return lax.dot_general(a, b, (((0,), (0,)), ((), ())), preferred_element_type=F32)


def _layer_norm(z, g, b):
    mu = jnp.mean(z, axis=-1, keepdims=True)
    zc = z - mu
    var = jnp.mean(zc * zc, axis=-1, keepdims=True)
    return zc * lax.rsqrt(var + LN_EPS) * g + b


def _group_norm(o):
    mu = jnp.mean(o, axis=-1, keepdims=True)
    oc = o - mu
    var = jnp.mean(oc * oc, axis=-1, keepdims=True)
    return oc * lax.rsqrt(var + GN_EPS)


def _rotary(xh, cos, sin_signed):
    return xh * cos + pltpu.roll(xh, RET_DK // 2, axis=1) * sin_signed


def _softmax_rows(s):
    m = jnp.max(s, axis=-1, keepdims=True)
    e = jnp.exp(s - m)
    return e / jnp.sum(e, axis=-1, keepdims=True)


def _merge_out_ln(x_ref, pre_refs, gate_logits, bg_ref, w_refs, wo_ref, ln_refs, y_ref):
    merged = None
    for i, (pre_ref, w_ref) in enumerate(zip(pre_refs, w_refs)):
        gate = jax.nn.sigmoid(gate_logits(i) + bg_ref[:, i * D_MODEL:(i + 1) * D_MODEL])
        term = gate * _dot(pre_ref[...], w_ref[...])
        merged = term if merged is None else merged + term
    merged = merged.astype(BF16)
    half = merged.shape[0] // 2
    for rows in (slice(0, half), slice(half, 2 * half)):
        z = DEEPNORM_ALPHA * x_ref[rows, :] + _dot(merged[rows], wo_ref[...])
        y_ref[rows, :] = z if ln_refs is None else _layer_norm(z, ln_refs[0][...], ln_refs[1][...])


def _ffn_ln(x, wup_ref, wdn_ref, g_ref, b_ref):
    xb = x.astype(BF16)
    acc = None
    for c in range(D_FF // FFN_CHUNK):
        cols = slice(c * FFN_CHUNK, (c + 1) * FFN_CHUNK)
        h = jnp.maximum(_dot(xb, wup_ref[:, cols]), 0.0)
        part = _dot((h * h).astype(BF16), wdn_ref[cols, :])
        acc = part if acc is None else acc + part
    return _layer_norm(DEEPNORM_ALPHA * x + acc, g_ref[...], b_ref[...])


MEM_KV_ROWS = 512
MEM_KV_OUTPUTS = 4


def _mem_kv_kernel(n_cast, mem_ref, w_ref, *refs):
    cast_src = refs[:n_cast]
    k_ref, v_ref, kb_ref, vb_ref = refs[n_cast:n_cast + MEM_KV_OUTPUTS]
    cast_dst = refs[n_cast + MEM_KV_OUTPUTS:n_cast + MEM_KV_OUTPUTS + n_cast]
    wb_ref = refs[-1]
    j = pl.program_id(1)

    for src_ref, dst_ref in zip(cast_src, cast_dst):
        dst_ref[...] = src_ref[...].astype(BF16)

    @pl.when(j == 0)
    def _():
        wb_ref[...] = w_ref[...].astype(BF16)

    rows = pl.ds(pl.multiple_of(j * MEM_KV_ROWS, MEM_KV_ROWS), MEM_KV_ROWS)
    kv = _dot(mem_ref[rows, :].astype(BF16), wb_ref[...])
    kb_ref[...] = kv[:, :MEM_WIDTH].astype(BF16)
    vb_ref[...] = kv[:, MEM_WIDTH:].astype(BF16)
    for h in range(MEM_HEADS):
        head_rows = pl.ds(h, MEM_KV_ROWS, stride=MEM_HEADS)
        k_ref[head_rows, :] = kv[:, h * MEM_HEAD_DIM:(h + 1) * MEM_HEAD_DIM]
        v_ref[head_rows, :] = kv[:, MEM_WIDTH + h * MEM_HEAD_DIM:MEM_WIDTH + (h + 1) * MEM_HEAD_DIM]


def _mem_kv(mem_prompt, w_mem_kv, to_cast):
    bp = mem_prompt.shape[0]
    n = bp * MEM_LEN
    nsub = n // MEM_KV_ROWS
    steps = DEPTH * nsub
    assert nsub * MEM_KV_ROWS == n
    cast_in_specs, cast_out_specs, cast_out_shapes = [], [], []
    for a in to_cast:
        _, rows, cols = a.shape
        slab = rows // steps
        assert slab * steps == rows and slab % 16 == 0
        cast_in_specs.append(pl.BlockSpec((None, slab, cols), lambda l, j: (0, l * nsub + j, 0)))
        cast_out_specs.append(pl.BlockSpec((slab, cols), lambda l, j: (l * nsub + j, 0)))
        cast_out_shapes.append(jax.ShapeDtypeStruct((rows, cols), BF16))
    out_f = jax.ShapeDtypeStruct((DEPTH, n * MEM_HEADS, MEM_HEAD_DIM), F32)
    out_b = jax.ShapeDtypeStruct((DEPTH, n, MEM_WIDTH), BF16)
    f_spec = pl.BlockSpec((None, MEM_KV_ROWS * MEM_HEADS, MEM_HEAD_DIM), lambda l, j: (l, j, 0))
    b_spec = pl.BlockSpec((None, MEM_KV_ROWS, MEM_WIDTH), lambda l, j: (l, j, 0))
    return pl.pallas_call(
        functools.partial(_mem_kv_kernel, len(to_cast)),
        grid=(DEPTH, nsub),
        in_specs=[
            _const((n, D_MODEL)),
            pl.BlockSpec((None, D_MODEL, 2 * MEM_WIDTH), lambda l, j: (l, 0, 0)),
        ] + cast_in_specs,
        out_specs=[f_spec, f_spec, b_spec, b_spec] + cast_out_specs,
        out_shape=[out_f, out_f, out_b, out_b] + cast_out_shapes,
        scratch_shapes=[pltpu.VMEM((D_MODEL, 2 * MEM_WIDTH), BF16)],
        compiler_params=pltpu.CompilerParams(
            dimension_semantics=("arbitrary", "arbitrary"), vmem_limit_bytes=VMEM_LIMIT_BYTES),
        name="mem_kv",
    )(mem_prompt.reshape(n, D_MODEL), w_mem_kv, *to_cast)


MIXER_INPUTS = 17
MIXER_OUTPUTS = 3


def _prompt_mixer_kernel(aliased, n_cast, *refs):
    if aliased:
        refs = refs[1:]
    (cd_ref, x_ref, cos_ref, sin_ref, dmask_ref, qdec_ref, kdec_ref, mk_ref, mv_ref,
     win_ref, bg_ref, cw_ref, gn_ref, wco_ref, wro_ref, wmo_ref, wo_ref) = refs[:MIXER_INPUTS]
    cast_src = refs[MIXER_INPUTS:MIXER_INPUTS + n_cast]
    outs = refs[MIXER_INPUTS + n_cast:]
    y_ref, ctail_ref, ret_ref = outs[:MIXER_OUTPUTS]
    cast_dst = outs[MIXER_OUTPUTS:MIXER_OUTPUTS + n_cast]
    (xb_ref, q_ref, k_ref, kd_ref, vb_ref, msk_ref, kv_ref,
     abuf_ref, bbuf_ref, mbuf_ref) = outs[MIXER_OUTPUTS + n_cast:]

    for src_ref, dst_ref in zip(cast_src, cast_dst):
        dst_ref[...] = src_ref[...].astype(BF16)

    t = pl.program_id(1)
    tq = x_ref.shape[0]
    chunks = [slice(j * RET_CHUNK, (j + 1) * RET_CHUNK) for j in range(tq // RET_CHUNK)]
    kcols = [slice(h * RET_DK, (h + 1) * RET_DK) for h in range(RET_HEADS)]
    vcols = [slice(h * RET_DV, (h + 1) * RET_DV) for h in range(RET_HEADS)]
    mcols = [slice(h * MEM_HEAD_DIM, (h + 1) * MEM_HEAD_DIM) for h in range(MEM_HEADS)]

    @pl.when(t == 0)
    def _():
        ret_ref[...] = jnp.zeros_like(ret_ref)
        ctail_ref[...] = jnp.zeros_like(ctail_ref)

    xb_ref[...] = x_ref[...].astype(BF16)

    def proj(lo, width):
        return _dot(xb_ref[...], win_ref[:, lo:lo + width])

    pq = proj(OFF_RQ, RET_QK_WIDTH)
    pk = proj(OFF_RK, RET_QK_WIDTH)
    vb_ref[...] = proj(OFF_RV, RET_V_WIDTH).astype(BF16)
    cos = cos_ref[...]
    sin = sin_ref[...]
    for h in range(RET_HEADS):
        q_ref[:, kcols[h]] = _rotary(pq[:, kcols[h]], cos, sin).astype(BF16)
        kh = _rotary(pk[:, kcols[h]], cos, sin) * (RET_DK ** -0.5)
        k_ref[:, kcols[h]] = kh.astype(BF16)
        for r in chunks:
            kd_ref[r, kcols[h]] = (kh[r] * kdec_ref[h]).astype(BF16)

    pm = proj(OFF_MQ, MEM_WIDTH).astype(BF16)
    mem_s = [_dot_nt(pm[:, mcols[h]], mk_ref[:, mcols[h]]) * (MEM_HEAD_DIM ** -0.5) for h in range(MEM_HEADS)]

    cb = proj(OFF_CB, CONV_WIDTH)
    u = proj(OFF_CC, CONV_WIDTH) * proj(OFF_CH, CONV_WIDTH)
    row = lax.broadcasted_iota(jnp.int32, u.shape, 0)
    prev1 = ctail_ref[7:8, :]
    prev2 = ctail_ref[6:7, :]
    u1 = jnp.where(row == 0, prev1, pltpu.roll(u, 1, axis=0))
    u2 = jnp.where(row == 0, prev2, jnp.where(row == 1, prev1, pltpu.roll(u, 2, axis=0)))
    conv_y = u2 * cw_ref[0:1, :] + u1 * cw_ref[1:2, :] + u * cw_ref[2:3, :]
    ctail_ref[...] = u[tq - 8:, :]
    abuf_ref[...] = (cb * conv_y).astype(BF16)

    for h in range(MEM_HEADS):
        p = _softmax_rows(mem_s[h])
        mbuf_ref[:, mcols[h]] = _dot(p.astype(BF16), mv_ref[:, mcols[h]]).astype(BF16)

    for h in range(RET_HEADS):
        for r in chunks:
            msk_ref[r, kcols[h]] = (_dot_nt(q_ref[r, kcols[h]], k_ref[r, kcols[h]]) * dmask_ref[h]).astype(BF16)

    for h in range(RET_HEADS):
        for j, r in enumerate(chunks):
            kv_ref[h, j] = _dot_tn(kd_ref[r, kcols[h]], vb_ref[r, vcols[h]])

    for h in range(RET_HEADS):
        pg = proj(OFF_RG + h * RET_DV, RET_DV)
        s_cur = ret_ref[h]
        for j, r in enumerate(chunks):
            intra = _dot(msk_ref[r, kcols[h]], vb_ref[r, vcols[h]])
            o = intra + _dot(q_ref[r, kcols[h]], s_cur.astype(BF16)) * qdec_ref[h]
            gated = jax.nn.silu(pg[r]) * (_group_norm(o) * gn_ref[:, vcols[h]])
            bbuf_ref[r, vcols[h]] = gated.astype(BF16)
            s_cur = s_cur * cd_ref[h] + kv_ref[h, j]
        ret_ref[h] = s_cur

    _merge_out_ln(x_ref, (abuf_ref, bbuf_ref, mbuf_ref), lambda i: proj(OFF_GL + i * D_MODEL, D_MODEL), bg_ref,
                  (wco_ref, wro_ref, wmo_ref), wo_ref, None, y_ref)


def _const(shape):
    nd = len(shape)
    return pl.BlockSpec(tuple(shape), lambda *_: (0,) * nd, pipeline_mode=pl.Buffered(1))


def _prompt_mixer(layer, x, tabs, mk_b, mv_b, w, to_cast, ret_stack):
    bp, seq, _ = x.shape
    tq = PROMPT_TILE
    grid = (bp, seq // tq)
    steps = bp * (seq // tq)
    steps_per_seq = seq // tq
    cast_in_specs, cast_out_specs, cast_out_shapes = [], [], []
    for a, src_layer in to_cast:
        _, rows, cols = a.shape
        slab = rows // steps
        assert slab * steps == rows and slab % 16 == 0
        cast_in_specs.append(pl.BlockSpec(
            (None, slab, cols), lambda b, t, src_layer=src_layer: (src_layer, b * steps_per_seq + t, 0)))
        cast_out_specs.append(pl.BlockSpec((slab, cols), lambda b, t: (b * steps_per_seq + t, 0)))
        cast_out_shapes.append(jax.ShapeDtypeStruct((rows, cols), BF16))
    in_specs = [
        pl.BlockSpec(memory_space=pltpu.SMEM),
        pl.BlockSpec((None, tq, D_MODEL), lambda b, t: (b, t, 0)),
        pl.BlockSpec((tq, RET_DK), lambda b, t: (t, 0)),
        pl.BlockSpec((tq, RET_DK), lambda b, t: (t, 0)),
        _const((RET_HEADS, RET_CHUNK, RET_CHUNK)),
        _const((RET_HEADS, RET_CHUNK, RET_DV)),
        _const((RET_HEADS, RET_CHUNK, RET_DK)),
        pl.BlockSpec((None, None, MEM_LEN, MEM_WIDTH), lambda b, t: (layer, b, 0, 0)),
        pl.BlockSpec((None, None, MEM_LEN, MEM_WIDTH), lambda b, t: (layer, b, 0, 0)),
        _const((D_MODEL, IN_WIDTH)),
        _const((1, N_BRANCH * D_MODEL)),
        _const((CONV_K, CONV_WIDTH)),
        _const((1, RET_V_WIDTH)),
        _const((CONV_WIDTH, D_MODEL)),
        _const((RET_V_WIDTH, D_MODEL)),
        _const((MEM_WIDTH, D_MODEL)),
        _const((D_MODEL, D_MODEL)),
    ]
    out_specs = [
        pl.BlockSpec((None, tq, D_MODEL), lambda b, t: (b, t, 0)),
        pl.BlockSpec((None, 8, CONV_WIDTH), lambda b, t: (b, 0, 0)),
        pl.BlockSpec((None, None, RET_HEADS, RET_DK, RET_DV), lambda b, t: (layer, b, 0, 0, 0)),
    ]
    out_shape = [
        jax.ShapeDtypeStruct((bp, seq, D_MODEL), F32),
        jax.ShapeDtypeStruct((bp, 8, CONV_WIDTH), F32),
        jax.ShapeDtypeStruct((DEPTH, bp, RET_HEADS, RET_DK, RET_DV), F32),
    ]
    assert len(in_specs) == MIXER_INPUTS and len(out_specs) == MIXER_OUTPUTS
    args = (tabs["cd"], x, tabs["cos"], tabs["sin"], tabs["dmask"], tabs["qdec"], tabs["kdec"], mk_b, mv_b,
            w["w_in"], w["b_gate"], w["conv_w"], w["ret_gn_g"], w["w_conv_out"], w["w_ret_out"], w["w_mem_out"],
            w["w_out"], *[a for a, _ in to_cast])
    in_specs = in_specs + cast_in_specs
    aliases = {}
    if ret_stack is not None:
        aliases = {0: MIXER_OUTPUTS - 1}
        in_specs = [pl.BlockSpec(memory_space=pl.ANY)] + in_specs
        args = (ret_stack,) + args
    return pl.pallas_call(
        functools.partial(_prompt_mixer_kernel, ret_stack is not None, len(to_cast)),
        grid=grid,
        in_specs=in_specs,
        out_specs=out_specs + cast_out_specs,
        out_shape=out_shape + cast_out_shapes,
        scratch_shapes=[
            pltpu.VMEM((tq, D_MODEL), BF16),
            pltpu.VMEM((tq, RET_QK_WIDTH), BF16),
            pltpu.VMEM((tq, RET_QK_WIDTH), BF16),
            pltpu.VMEM((tq, RET_QK_WIDTH), BF16),
            pltpu.VMEM((tq, RET_V_WIDTH), BF16),
            pltpu.VMEM((tq, RET_QK_WIDTH), BF16),
            pltpu.VMEM((RET_HEADS, tq // RET_CHUNK, RET_DK, RET_DV), F32),
            pltpu.VMEM((tq, CONV_WIDTH), BF16),
            pltpu.VMEM((tq, RET_V_WIDTH), BF16),
            pltpu.VMEM((tq, MEM_WIDTH), BF16),
        ],
        input_output_aliases=aliases,
        compiler_params=pltpu.CompilerParams(
            dimension_semantics=("arbitrary", "arbitrary"), vmem_limit_bytes=VMEM_LIMIT_BYTES),
        name="prompt_mixer",
    )(*args)


FFN_INPUTS = 8
STATE_INPUTS = 13
STATE_OUTPUTS = 5


def _ffn_state_kernel(aliased, *refs):
    if aliased:
        refs = refs[1:]
    z_first_ref, z_next_ref, l1g_ref, l1b_ref, wup_ref, wdn_ref, g_ref, b_ref = refs[:FFN_INPUTS]
    state_in = refs[FFN_INPUTS:FFN_INPUTS + STATE_INPUTS]
    y_ref = refs[FFN_INPUTS + STATE_INPUTS]
    rest = refs[FFN_INPUTS + STATE_INPUTS + 1:]
    x1_scr = rest[-1]
    stages = _sample_state_stages(*state_in, *rest[:-1])
    n_chunks = D_FF // FFN_CHUNK
    assert len(stages) == 2 * n_chunks - 1

    @pl.when(pl.program_id(0) == 0)
    def _():
        x1_scr[...] = _layer_norm(z_first_ref[...], l1g_ref[...], l1b_ref[...])

    x = x1_scr[...]
    x1_scr[...] = _layer_norm(z_next_ref[...], l1g_ref[...], l1b_ref[...])
    xb = x.astype(BF16)
    acc = None
    for c in range(n_chunks):
        cols = slice(c * FFN_CHUNK, (c + 1) * FFN_CHUNK)
        h = jnp.maximum(_dot(xb, wup_ref[:, cols]), 0.0)
        stages[2 * c]()
        part = _dot((h * h).astype(BF16), wdn_ref[cols, :])
        if c + 1 < n_chunks:
            stages[2 * c + 1]()
        acc = part if acc is None else acc + part
    y_ref[...] = _layer_norm(DEEPNORM_ALPHA * x + acc, g_ref[...], b_ref[...])


def _ffn_and_sample_state(layer, z, w, proj, cst, tabs, state_ret, kc, vc, ret_stack):
    n = z.shape[0]
    tm = FFN_TILE
    ns = SAMPLE_SEQS
    n_tok = proj.shape[0]
    nseq = state_ret.shape[1]
    t_len = n_tok // nseq
    rows = ns * t_len
    n_tiles = n // tm
    assert n_tiles == nseq // ns
    tok_spec = lambda width: pl.BlockSpec((rows, width), lambda i: (i, 0))
    in_specs = [
        pl.BlockSpec((tm, D_MODEL), lambda i: (0, 0)),
        pl.BlockSpec((tm, D_MODEL), lambda i: (jnp.minimum(i + 1, n_tiles - 1), 0)),
        _const((1, D_MODEL)),
        _const((1, D_MODEL)),
        _const((D_MODEL, D_FF)),
        _const((D_FF, D_MODEL)),
        _const((1, D_MODEL)),
        _const((1, D_MODEL)),
        pl.BlockSpec(memory_space=pltpu.SMEM),
        tok_spec(OFF_GL),
        tok_spec(CONV_WIDTH),
        _const((rows, RET_DK)),
        _const((rows, RET_DK)),
        _const((RET_HEADS, rows, rows)),
        _const((RET_HEADS, t_len, RET_DV)),
        _const((RET_HEADS, rows, RET_DK)),
        pl.BlockSpec((None, ns, RET_HEADS, RET_DK, RET_DV), lambda i: (layer, i, 0, 0, 0)),
        pl.BlockSpec((None, ns, MEM_LEN * MEM_HEADS, MEM_HEAD_DIM), lambda i: (layer, i, 0, 0)),
        pl.BlockSpec((None, ns, MEM_LEN * MEM_HEADS, MEM_HEAD_DIM), lambda i: (layer, i, 0, 0)),
        _const((CONV_K, CONV_WIDTH)),
        _const((1, RET_V_WIDTH)),
    ]
    out_specs = [
        pl.BlockSpec((tm, D_MODEL), lambda i: (i, 0)),
        tok_spec(CONV_WIDTH),
        tok_spec(RET_V_WIDTH),
        tok_spec(MEM_WIDTH),
        tok_spec(CONV_WIDTH),
        pl.BlockSpec((None, ns, RET_HEADS, RET_DK, RET_DV), lambda i: (layer, i, 0, 0, 0)),
    ]
    out_shape = [
        jax.ShapeDtypeStruct((n, D_MODEL), F32),
        jax.ShapeDtypeStruct((n_tok, CONV_WIDTH), BF16),
        jax.ShapeDtypeStruct((n_tok, RET_V_WIDTH), BF16),
        jax.ShapeDtypeStruct((n_tok, MEM_WIDTH), BF16),
        jax.ShapeDtypeStruct((n_tok, CONV_WIDTH), F32),
        jax.ShapeDtypeStruct((DEPTH, nseq, RET_HEADS, RET_DK, RET_DV), F32),
    ]
    assert len(in_specs) == FFN_INPUTS + STATE_INPUTS and len(out_specs) == 1 + STATE_OUTPUTS
    args = (z, z, w["ln1_g"], w["ln1_b"], w["w_up"], w["w_down"], w["ln2_g"], w["ln2_b"],
            tabs["cd"], proj, cst, tabs["cos"], tabs["sin"], tabs["dmask"], tabs["qdec"], tabs["kdec"],
            state_ret, kc, vc, w["conv_w"], w["ret_gn_g"])
    aliases = {}
    if ret_stack is not None:
        aliases = {0: len(out_shape) - 1}
        in_specs = [pl.BlockSpec(memory_space=pl.ANY)] + in_specs
        args = (ret_stack,) + args
    scratch = [
        pltpu.VMEM((rows, RET_V_WIDTH), F32),
        pltpu.VMEM((rows, MEM_HEADS * MEM_LEN), F32),
        pltpu.VMEM((rows, MEM_WIDTH), F32),
        pltpu.VMEM((tm, D_MODEL), F32),
    ]
    return pl.pallas_call(
        functools.partial(_ffn_state_kernel, ret_stack is not None),
        grid=(n // tm,),
        in_specs=in_specs,
        out_specs=out_specs,
        out_shape=out_shape,
        scratch_shapes=scratch,
        input_output_aliases=aliases,
        compiler_params=pltpu.CompilerParams(
            dimension_semantics=("arbitrary",), vmem_limit_bytes=VMEM_LIMIT_BYTES),
        name="ffn_state",
    )(*args)


def _sample_proj_kernel(x_ref, w_ref, o_ref):
    o_ref[...] = _dot(x_ref[...].astype(BF16), w_ref[...])


def _sample_proj(x, w_in_b):
    n = x.shape[0]
    tn = 1024
    return pl.pallas_call(
        _sample_proj_kernel,
        grid=(IN_WIDTH // tn,),
        in_specs=[
            pl.BlockSpec((n, D_MODEL), lambda j: (0, 0)),
            pl.BlockSpec((D_MODEL, tn), lambda j: (0, j)),
        ],
        out_specs=pl.BlockSpec((n, tn), lambda j: (0, j)),
        out_shape=jax.ShapeDtypeStruct((n, IN_WIDTH), F32),
        compiler_params=pltpu.CompilerParams(
            dimension_semantics=("arbitrary",), vmem_limit_bytes=VMEM_LIMIT_BYTES),
        name="sample_proj",
    )(x, w_in_b)


def _sample_state_stages(cd_ref, proj_ref, cst_ref, cos_ref, sin_ref, dmask_ref, qdec_ref, kdec_ref,
                         sret_ref, kc_ref, vc_ref, cw_ref, gn_ref,
                         a_ref, b_ref, m_ref, u_ref, nret_ref,
                         o_scr, p_scr, m_scr):
    ns = sret_ref.shape[0]
    t_len = proj_ref.shape[0] // ns
    rows = [slice(s * t_len, (s + 1) * t_len) for s in range(ns)]
    kcols = [slice(h * RET_DK, (h + 1) * RET_DK) for h in range(RET_HEADS)]
    vcols = [slice(h * RET_DV, (h + 1) * RET_DV) for h in range(RET_HEADS)]
    mcols = [slice(h * MEM_HEAD_DIM, (h + 1) * MEM_HEAD_DIM) for h in range(MEM_HEADS)]
    pcols = [slice(h * MEM_LEN, (h + 1) * MEM_LEN) for h in range(MEM_HEADS)]
    head_rows = [pl.ds(h, MEM_LEN, stride=MEM_HEADS) for h in range(MEM_HEADS)]

    q, kd, masked = [], [], []

    def conv_and_scores():
        cb = proj_ref[:, OFF_CB:OFF_CB + CONV_WIDTH]
        u = proj_ref[:, OFF_CC:OFF_CC + CONV_WIDTH] * proj_ref[:, OFF_CH:OFF_CH + CONV_WIDTH]
        tok = lax.broadcasted_iota(jnp.int32, u.shape, 0) & (t_len - 1)
        st2 = cst_ref[...]
        st1 = pltpu.roll(st2, st2.shape[0] - 1, axis=0)
        u1 = jnp.where(tok == 0, st1, pltpu.roll(u, 1, axis=0))
        u2 = jnp.where(tok < 2, st2, pltpu.roll(u, 2, axis=0))
        conv_y = u2 * cw_ref[0:1, :] + u1 * cw_ref[1:2, :] + u * cw_ref[2:3, :]
        u_ref[...] = u
        a_ref[...] = (cb * conv_y).astype(BF16)

        cos = cos_ref[...]
        sin = sin_ref[...]
        for h in range(RET_HEADS):
            qh = _rotary(proj_ref[:, OFF_RQ + h * RET_DK:OFF_RQ + (h + 1) * RET_DK], cos, sin)
            kh = _rotary(proj_ref[:, OFF_RK + h * RET_DK:OFF_RK + (h + 1) * RET_DK], cos, sin) * (RET_DK ** -0.5)
            q.append(qh)
            kd.append(kh * kdec_ref[h])
            masked.append((_dot_nt(qh.astype(BF16), kh.astype(BF16)) * dmask_ref[h]).astype(BF16))

    def intra_chunk():
        for h in range(RET_HEADS):
            vb = proj_ref[:, OFF_RV + h * RET_DV:OFF_RV + (h + 1) * RET_DV].astype(BF16)
            o_scr[:, vcols[h]] = _dot(masked[h], vb)

    def cross_chunk_and_state(seqs):
        for s in seqs:
            for h in range(RET_HEADS):
                s_prev = sret_ref[s, h]
                inter = _dot(q[h][rows[s]].astype(BF16), s_prev.astype(BF16))
                o_scr[rows[s], vcols[h]] += inter * qdec_ref[h]
                vb = proj_ref[rows[s], OFF_RV + h * RET_DV:OFF_RV + (h + 1) * RET_DV].astype(BF16)
                nret_ref[s, h] = s_prev * cd_ref[h] + _dot_tn(kd[h][rows[s]].astype(BF16), vb)

    def attention_scores():
        for s in range(ns):
            for h in range(MEM_HEADS):
                mq = proj_ref[rows[s], OFF_MQ + h * MEM_HEAD_DIM:OFF_MQ + (h + 1) * MEM_HEAD_DIM].astype(BF16)
                p_scr[rows[s], pcols[h]] = (_dot_nt(mq, kc_ref[s, head_rows[h], :].astype(BF16))
                                            * (MEM_HEAD_DIM ** -0.5))

    def attention_values():
        for h in range(MEM_HEADS):
            p_scr[:, pcols[h]] = _softmax_rows(p_scr[:, pcols[h]])
        for s in range(ns):
            for h in range(MEM_HEADS):
                p = p_scr[rows[s], pcols[h]].astype(BF16)
                m_scr[rows[s], mcols[h]] = _dot(p, vc_ref[s, head_rows[h], :].astype(BF16))
        m_ref[...] = m_scr[...].astype(BF16)

    def norm_and_gate():
        for h in range(RET_HEADS):
            rg = proj_ref[:, OFF_RG + h * RET_DV:OFF_RG + (h + 1) * RET_DV]
            gated = jax.nn.silu(rg) * (_group_norm(o_scr[:, vcols[h]]) * gn_ref[:, vcols[h]])
            b_ref[:, vcols[h]] = gated.astype(BF16)

    first, second = list(range(ns // 2)), list(range(ns // 2, ns))
    return [conv_and_scores, intra_chunk,
            functools.partial(cross_chunk_and_state, first), functools.partial(cross_chunk_and_state, second),
            attention_scores, attention_values, norm_and_gate]


def _sample_post_kernel(x_ref, a_ref, b_ref, m_ref, g0_ref, g1_ref, g2_ref,
                        bg_ref, wco_ref, wro_ref, wmo_ref, wo_ref, l1g_ref, l1b_ref,
                        wup_ref, wdn_ref, l2g_ref, l2b_ref, y_ref, x1_ref):
    gate_refs = (g0_ref, g1_ref, g2_ref)
    _merge_out_ln(x_ref, (a_ref, b_ref, m_ref), lambda i: gate_refs[i][...], bg_ref,
                  (wco_ref, wro_ref, wmo_ref), wo_ref, (l1g_ref, l1b_ref), x1_ref)
    y_ref[...] = _ffn_ln(x1_ref[...], wup_ref, wdn_ref, l2g_ref, l2b_ref)


def _sample_post(x, a_pre, b_pre, m_pre, proj, w):
    n = x.shape[0]
    tm = SAMPLE_POST_TILE
    tok = lambda width: pl.BlockSpec((tm, width), lambda i: (i, 0))
    gate = lambda k: pl.BlockSpec((tm, D_MODEL), lambda i: (i, OFF_GL // D_MODEL + k))
    return pl.pallas_call(
        _sample_post_kernel,
        grid=(n // tm,),
        in_specs=[
            tok(D_MODEL), tok(CONV_WIDTH), tok(RET_V_WIDTH), tok(MEM_WIDTH), gate(0), gate(1), gate(2),
            _const((1, N_BRANCH * D_MODEL)),
            _const((CONV_WIDTH, D_MODEL)),
            _const((RET_V_WIDTH, D_MODEL)),
            _const((MEM_WIDTH, D_MODEL)),
            _const((D_MODEL, D_MODEL)),
            _const((1, D_MODEL)),
            _const((1, D_MODEL)),
            _const((D_MODEL, D_FF)),
            _const((D_FF, D_MODEL)),
            _const((1, D_MODEL)),
            _const((1, D_MODEL)),
        ],
        out_specs=tok(D_MODEL),
        out_shape=jax.ShapeDtypeStruct((n, D_MODEL), F32),
        scratch_shapes=[pltpu.VMEM((tm, D_MODEL), F32)],
        compiler_params=pltpu.CompilerParams(
            dimension_semantics=("arbitrary",), vmem_limit_bytes=VMEM_LIMIT_BYTES),
        name="sample_post",
    )(x, a_pre, b_pre, m_pre, proj, proj, proj,
      w["b_gate"], w["w_conv_out"], w["w_ret_out"], w["w_mem_out"], w["w_out"], w["ln1_g"], w["ln1_b"],
      w["w_up"], w["w_down"], w["ln2_g"], w["ln2_b"])


def _rotary_tables(pos, reps):
    half = RET_DK // 2
    inv = ROPE_BASE ** (-jnp.arange(half, dtype=F32) / half)
    ang = pos.astype(F32)[:, None] * inv[None, :]
    cos = jnp.cos(ang)
    sin = jnp.sin(ang)
    cos_full = jnp.concatenate([cos, cos], axis=-1)
    sin_signed = jnp.concatenate([-sin, sin], axis=-1)
    return jnp.tile(cos_full, (reps, 1)), jnp.tile(sin_signed, (reps, 1))


def _decay_tables(c):
    h = RET_HEADS
    log_g = jnp.log1p(-jnp.exp2(-5.0 - jnp.arange(h, dtype=F32)))
    idx = jnp.arange(c, dtype=F32)
    rel = idx[:, None] - idx[None, :]
    causal = rel >= 0
    dmask = jnp.where(causal[None], jnp.exp(jnp.where(causal, rel, 0.0)[None] * log_g[:, None, None]), 0.0)
    q_decay = jnp.exp((idx + 1.0)[None, :] * log_g[:, None])
    k_decay = jnp.exp((c - 1.0 - idx)[None, :] * log_g[:, None])
    chunk_decay = jnp.exp(c * log_g)
    return {
        "dmask": dmask,
        "qdec": jnp.broadcast_to(q_decay[:, :, None], (h, c, RET_DV)),
        "kdec": jnp.broadcast_to(k_decay[:, :, None], (h, c, RET_DK)),
        "cd": chunk_decay,
    }


def kernel(x_prompt, x_sample, cache_mem_k, cache_mem_v, state_conv, state_ret, mem_prompt,
           w_in, b_gate, conv_w, ret_gn_g, w_conv_out, w_ret_out, w_mem_out, w_out, w_mem_kv,
           ln1_g, ln1_b, w_up, w_down, ln2_g, ln2_b):
    bp, seq, _ = x_prompt.shape
    nseq, t_len, _ = x_sample.shape
    assert seq % PROMPT_TILE == 0 and PROMPT_TILE % RET_CHUNK == 0
    assert t_len == 8 and nseq % SAMPLE_SEQS == 0 and t_len % RET_CHUNK != 0

    mixer_w = {"w_in": w_in, "w_conv_out": w_conv_out, "w_ret_out": w_ret_out, "w_mem_out": w_mem_out, "w_out": w_out}
    mlp_w = {"w_up": w_up, "w_down": w_down}
    small = {"b_gate": b_gate, "ret_gn_g": ret_gn_g, "ln1_g": ln1_g, "ln1_b": ln1_b, "ln2_g": ln2_g, "ln2_b": ln2_b}

    tabs_p = _decay_tables(RET_CHUNK)
    tabs_p["cos"], tabs_p["sin"] = _rotary_tables(jnp.arange(seq, dtype=jnp.int32), 1)
    tabs_s = _decay_tables(t_len)
    eye = jnp.eye(SAMPLE_SEQS, dtype=F32)
    tabs_s["dmask"] = (eye[None, :, None, :, None] * tabs_s["dmask"][:, None, :, None, :]).reshape(
        RET_HEADS, SAMPLE_SEQS * t_len, SAMPLE_SEQS * t_len)
    tabs_s["kdec"] = jnp.tile(tabs_s["kdec"], (1, SAMPLE_SEQS, 1))
    tabs_s["cos"], tabs_s["sin"] = _rotary_tables(PAST_LEN + jnp.arange(t_len, dtype=jnp.int32), SAMPLE_SEQS)

    mk_f, mv_f, mk_b, mv_b, *cast = _mem_kv(mem_prompt, w_mem_kv, list(mixer_w.values()))
    mixer_b = dict(zip(mixer_w, cast))
    mk_b = mk_b.reshape(DEPTH, bp, MEM_LEN, MEM_WIDTH)
    mv_b = mv_b.reshape(DEPTH, bp, MEM_LEN, MEM_WIDTH)

    kc = cache_mem_k.reshape(DEPTH, nseq, MEM_LEN * MEM_HEADS, MEM_HEAD_DIM)
    vc = cache_mem_v.reshape(DEPTH, nseq, MEM_LEN * MEM_HEADS, MEM_HEAD_DIM)
    cst = jnp.concatenate(
        [state_conv, jnp.zeros((DEPTH, nseq, t_len - (CONV_K - 1), CONV_WIDTH), state_conv.dtype)], axis=2
    ).reshape(DEPTH, nseq * t_len, CONV_WIDTH)

    xp = x_prompt
    xs = x_sample.reshape(nseq * t_len, D_MODEL)
    cp_list, cs_list = [], []
    ret_stack = ret_stack_p = None
    for l in range(DEPTH):
        w = dict(mixer_b, conv_w=conv_w[l], **{name: a[l].reshape(1, -1) for name, a in small.items()})
        to_cast = [(a, l) for a in mlp_w.values()]
        if l + 1 < DEPTH:
            to_cast += [(a, l + 1) for a in mixer_w.values()]
        zp, ctail_p, ret_stack_p, *cast = _prompt_mixer(l, xp, tabs_p, mk_b, mv_b, w, to_cast, ret_stack_p)
        w.update(zip(mlp_w, cast[:len(mlp_w)]))
        mixer_b = dict(zip(mixer_w, cast[len(mlp_w):]))
        cp_list.append(ctail_p[:, 8 - (CONV_K - 1):, :])

        proj = _sample_proj(xs, w["w_in"])
        xp, a_pre, b_pre, m_pre, u_s, ret_stack = _ffn_and_sample_state(
            l, zp.reshape(bp * seq, D_MODEL), w, proj, cst[l], tabs_s, state_ret, kc, vc, ret_stack)
        xp = xp.reshape(bp, seq, D_MODEL)
        xs = _sample_post(xs, a_pre, b_pre, m_pre, proj, w)
        cs_list.append(u_s.reshape(nseq, t_len, CONV_WIDTH)[:, t_len - (CONV_K - 1):, :])

    mem_shape = (DEPTH, bp, MEM_LEN, MEM_HEADS, MEM_HEAD_DIM)
    return (xp, xs.reshape(nseq, t_len, D_MODEL), mk_f.reshape(mem_shape), mv_f.reshape(mem_shape),
            jnp.stack(cp_list), ret_stack_p, jnp.stack(cs_list), ret_stack)
```

```python
import functools

import jax
import jax.numpy as jnp
from jax import lax
from jax.experimental import pallas as pl
from jax.experimental.pallas import tpu as pltpu

D_MODEL = 1024
DEPTH = 4
CONV_WIDTH = 512
CONV_K = 3
RET_HEADS = 4
RET_DK = 128
RET_DV = 256
RET_QK_WIDTH = RET_HEADS * RET_DK
RET_V_WIDTH = RET_HEADS * RET_DV
RET_CHUNK = 128
MEM_LEN = 256
MEM_HEADS = 4
MEM_HEAD_DIM = 128
MEM_WIDTH = MEM_HEADS * MEM_HEAD_DIM
N_BRANCH = 3
D_FF = 4 * D_MODEL
ROPE_BASE = 10000.0
LN_EPS = 1e-5
GN_EPS = 1e-6
PAST_LEN = 16384
DEEPNORM_ALPHA = (2 * DEPTH) ** 0.25

OFF_CB = 0
OFF_CC = OFF_CB + CONV_WIDTH
OFF_CH = OFF_CC + CONV_WIDTH
OFF_RQ = OFF_CH + CONV_WIDTH
OFF_RK = OFF_RQ + RET_QK_WIDTH
OFF_RV = OFF_RK + RET_QK_WIDTH
OFF_RG = OFF_RV + RET_V_WIDTH
OFF_MQ = OFF_RG + RET_V_WIDTH
OFF_GL = OFF_MQ + MEM_WIDTH
IN_WIDTH = OFF_GL + N_BRANCH * D_MODEL

V7X_VMEM_BYTES = 64 * 1024 * 1024
VMEM_LIMIT_BYTES = V7X_VMEM_BYTES - 6 * 1024 * 1024

PROMPT_TILE = 512
MIXER_CHUNK = 256
FFN_TILE = 512
FFN_CHUNK = 1024
SAMPLE_SEQS = 4
SAMPLE_POST_TILE = 512
SAMPLE_PROJ_COLS = 2048

BF16 = jnp.bfloat16
F32 = jnp.float32


def _dot(a, b):
    return jnp.dot(a, b, preferred_element_type=F32)


def _dot_nt(a, b):
    return lax.dot_general(a, b, (((1,), (1,)), ((), ())), preferred_element_type=F32)


def _dot_tn(a, b):
    return lax.dot_general(a, b, (((0,), (0,)), ((), ())), preferred_element_type=F32)


def _layer_norm(z, g, b):
    mu = jnp.mean(z, axis=-1, keepdims=True)
    zc = z - mu
    var = jnp.mean(zc * zc, axis=-1, keepdims=True)
    return zc * lax.rsqrt(var + LN_EPS) * g + b


def _group_norm(o):
    mu = jnp.mean(o, axis=-1, keepdims=True)
    oc = o - mu
    var = jnp.mean(oc * oc, axis=-1, keepdims=True)
    return oc * lax.rsqrt(var + GN_EPS)


def _rotary(xh, cos, sin_signed):
    return xh * cos + pltpu.roll(xh, RET_DK // 2, axis=1) * sin_signed


def _softmax_rows(s):
    m = jnp.max(s, axis=-1, keepdims=True)
    e = jnp.exp(s - m)
    return e / jnp.sum(e, axis=-1, keepdims=True)


def _merge_out_ln(x_ref, pre_refs, gate_logits, bg_ref, w_refs, wo_ref, ln_refs, y_ref):
    merged = None
    for i, (pre_ref, w_ref) in enumerate(zip(pre_refs, w_refs)):
        gate = jax.nn.sigmoid(gate_logits(i) + bg_ref[:, i * D_MODEL:(i + 1) * D_MODEL])
        term = gate * _dot(pre_ref[...], w_ref[...])
        merged = term if merged is None else merged + term
    merged = merged.astype(BF16)
    half = merged.shape[0] // 2
    for rows in (slice(0, half), slice(half, 2 * half)):
        z = DEEPNORM_ALPHA * x_ref[rows, :] + _dot(merged[rows], wo_ref[...])
        y_ref[rows, :] = z if ln_refs is None else _layer_norm(z, ln_refs[0][...], ln_refs[1][...])


def _ffn_ln(x, wup_ref, wdn_ref, g_ref, b_ref):
    xb = x.astype(BF16)
    acc = None
    for c in range(D_FF // FFN_CHUNK):
        cols = slice(c * FFN_CHUNK, (c + 1) * FFN_CHUNK)
        h = jnp.maximum(_dot(xb, wup_ref[:, cols]), 0.0)
        part = _dot((h * h).astype(BF16), wdn_ref[cols, :])
        acc = part if acc is None else acc + part
    return _layer_norm(DEEPNORM_ALPHA * x + acc, g_ref[...], b_ref[...])


MEM_KV_ROWS = 512
MEM_KV_OUTPUTS = 4


def _mem_kv_kernel(n_cast, mem_ref, w_ref, *refs):
    cast_src = refs[:n_cast]
    k_ref, v_ref, kb_ref, vb_ref = refs[n_cast:n_cast + MEM_KV_OUTPUTS]
    cast_dst = refs[n_cast + MEM_KV_OUTPUTS:n_cast + MEM_KV_OUTPUTS + n_cast]
    wb_ref = refs[-1]
    j = pl.program_id(1)

    for src_ref, dst_ref in zip(cast_src, cast_dst):
        dst_ref[...] = src_ref[...].astype(BF16)

    @pl.when(j == 0)
    def _():
        wb_ref[...] = w_ref[...].astype(BF16)

    rows = pl.ds(pl.multiple_of(j * MEM_KV_ROWS, MEM_KV_ROWS), MEM_KV_ROWS)
    kv = _dot(mem_ref[rows, :].astype(BF16), wb_ref[...])
    kb_ref[...] = kv[:, :MEM_WIDTH].astype(BF16)
    vb_ref[...] = kv[:, MEM_WIDTH:].astype(BF16)
    for h in range(MEM_HEADS):
        head_rows = pl.ds(h, MEM_KV_ROWS, stride=MEM_HEADS)
        k_ref[head_rows, :] = kv[:, h * MEM_HEAD_DIM:(h + 1) * MEM_HEAD_DIM]
        v_ref[head_rows, :] = kv[:, MEM_WIDTH + h * MEM_HEAD_DIM:MEM_WIDTH + (h + 1) * MEM_HEAD_DIM]


def _mem_kv(mem_prompt, w_mem_kv, to_cast):
    bp = mem_prompt.shape[0]
    n = bp * MEM_LEN
    nsub = n // MEM_KV_ROWS
    steps = DEPTH * nsub
    assert nsub * MEM_KV_ROWS == n
    cast_in_specs, cast_out_specs, cast_out_shapes = [], [], []
    for a in to_cast:
        _, rows, cols = a.shape
        slab = rows // steps
        assert slab * steps == rows and slab % 16 == 0
        cast_in_specs.append(pl.BlockSpec((None, slab, cols), lambda l, j: (0, l * nsub + j, 0)))
        cast_out_specs.append(pl.BlockSpec((slab, cols), lambda l, j: (l * nsub + j, 0)))
        cast_out_shapes.append(jax.ShapeDtypeStruct((rows, cols), BF16))
    out_f = jax.ShapeDtypeStruct((DEPTH, n * MEM_HEADS, MEM_HEAD_DIM), F32)
    out_b = jax.ShapeDtypeStruct((DEPTH, n, MEM_WIDTH), BF16)
    f_spec = pl.BlockSpec((None, MEM_KV_ROWS * MEM_HEADS, MEM_HEAD_DIM), lambda l, j: (l, j, 0))
    b_spec = pl.BlockSpec((None, MEM_KV_ROWS, MEM_WIDTH), lambda l, j: (l, j, 0))
    return pl.pallas_call(
        functools.partial(_mem_kv_kernel, len(to_cast)),
        grid=(DEPTH, nsub),
        in_specs=[
            _const((n, D_MODEL)),
            pl.BlockSpec((None, D_MODEL, 2 * MEM_WIDTH), lambda l, j: (l, 0, 0)),
        ] + cast_in_specs,
        out_specs=[f_spec, f_spec, b_spec, b_spec] + cast_out_specs,
        out_shape=[out_f, out_f, out_b, out_b] + cast_out_shapes,
        scratch_shapes=[pltpu.VMEM((D_MODEL, 2 * MEM_WIDTH), BF16)],
        compiler_params=pltpu.CompilerParams(
            dimension_semantics=("arbitrary", "arbitrary"), vmem_limit_bytes=VMEM_LIMIT_BYTES),
        name="mem_kv",
    )(mem_prompt.reshape(n, D_MODEL), w_mem_kv, *to_cast)


MIXER_INPUTS = 17
MIXER_OUTPUTS = 3


def _prompt_mixer_kernel(aliased, n_cast, *refs):
    if aliased:
        refs = refs[1:]
    (cd_ref, x_ref, cos_ref, sin_ref, dmask_ref, qdec_ref, kdec_ref, mk_ref, mv_ref,
     win_ref, bg_ref, cw_ref, gn_ref, wco_ref, wro_ref, wmo_ref, wo_ref) = refs[:MIXER_INPUTS]
    cast_src = refs[MIXER_INPUTS:MIXER_INPUTS + n_cast]
    outs = refs[MIXER_INPUTS + n_cast:]
    y_ref, ctail_ref, ret_ref = outs[:MIXER_OUTPUTS]
    cast_dst = outs[MIXER_OUTPUTS:MIXER_OUTPUTS + n_cast]
    (xb_ref, q_ref, k_ref, kd_ref, vb_ref, msk_ref, kv_ref,
     abuf_ref, bbuf_ref, mbuf_ref) = outs[MIXER_OUTPUTS + n_cast:]

    for src_ref, dst_ref in zip(cast_src, cast_dst):
        dst_ref[...] = src_ref[...].astype(BF16)

    t = pl.program_id(1)
    tq = x_ref.shape[0]
    chunk = dmask_ref.shape[1]
    chunks = [slice(j * chunk, (j + 1) * chunk) for j in range(tq // chunk)]
    scols = [slice(h * chunk, (h + 1) * chunk) for h in range(RET_HEADS)]
    kcols = [slice(h * RET_DK, (h + 1) * RET_DK) for h in range(RET_HEADS)]
    vcols = [slice(h * RET_DV, (h + 1) * RET_DV) for h in range(RET_HEADS)]
    mcols = [slice(h * MEM_HEAD_DIM, (h + 1) * MEM_HEAD_DIM) for h in range(MEM_HEADS)]

    @pl.when(t == 0)
    def _():
        ret_ref[...] = jnp.zeros_like(ret_ref)
        ctail_ref[...] = jnp.zeros_like(ctail_ref)

    xb_ref[...] = x_ref[...].astype(BF16)

    def proj(lo, width):
        return _dot(xb_ref[...], win_ref[:, lo:lo + width])

    pq = proj(OFF_RQ, RET_QK_WIDTH)
    pk = proj(OFF_RK, RET_QK_WIDTH)
    vb_ref[...] = proj(OFF_RV, RET_V_WIDTH).astype(BF16)
    cos = cos_ref[...]
    sin = sin_ref[...]
    for h in range(RET_HEADS):
        q_ref[:, kcols[h]] = _rotary(pq[:, kcols[h]], cos, sin).astype(BF16)
        kh = _rotary(pk[:, kcols[h]], cos, sin) * (RET_DK ** -0.5)
        k_ref[:, kcols[h]] = kh.astype(BF16)
        for r in chunks:
            kd_ref[r, kcols[h]] = (kh[r] * kdec_ref[h]).astype(BF16)

    pm = proj(OFF_MQ, MEM_WIDTH).astype(BF16)
    mem_s = [_dot_nt(pm[:, mcols[h]], mk_ref[:, mcols[h]]) * (MEM_HEAD_DIM ** -0.5) for h in range(MEM_HEADS)]

    cb = proj(OFF_CB, CONV_WIDTH)
    u = proj(OFF_CC, CONV_WIDTH) * proj(OFF_CH, CONV_WIDTH)
    row = lax.broadcasted_iota(jnp.int32, u.shape, 0)
    prev1 = ctail_ref[7:8, :]
    prev2 = ctail_ref[6:7, :]
    u1 = jnp.where(row == 0, prev1, pltpu.roll(u, 1, axis=0))
    u2 = jnp.where(row == 0, prev2, jnp.where(row == 1, prev1, pltpu.roll(u, 2, axis=0)))
    conv_y = u2 * cw_ref[0:1, :] + u1 * cw_ref[1:2, :] + u * cw_ref[2:3, :]
    ctail_ref[...] = u[tq - 8:, :]
    abuf_ref[...] = (cb * conv_y).astype(BF16)

    for h in range(MEM_HEADS):
        p = _softmax_rows(mem_s[h])
        mbuf_ref[:, mcols[h]] = _dot(p.astype(BF16), mv_ref[:, mcols[h]]).astype(BF16)

    for h in range(RET_HEADS):
        for r in chunks:
            msk_ref[r, scols[h]] = (_dot_nt(q_ref[r, kcols[h]], k_ref[r, kcols[h]]) * dmask_ref[h]).astype(BF16)

    for h in range(RET_HEADS):
        for j, r in enumerate(chunks):
            kv_ref[h, j] = _dot_tn(kd_ref[r, kcols[h]], vb_ref[r, vcols[h]])

    for h in range(RET_HEADS):
        pg = proj(OFF_RG + h * RET_DV, RET_DV)
        s_cur = ret_ref[h]
        for j, r in enumerate(chunks):
            intra = _dot(msk_ref[r, scols[h]], vb_ref[r, vcols[h]])
            o = intra + _dot(q_ref[r, kcols[h]], s_cur.astype(BF16)) * qdec_ref[h]
            gated = jax.nn.silu(pg[r]) * (_group_norm(o) * gn_ref[:, vcols[h]])
            bbuf_ref[r, vcols[h]] = gated.astype(BF16)
            s_cur = s_cur * cd_ref[h] + kv_ref[h, j]
        ret_ref[h] = s_cur

    _merge_out_ln(x_ref, (abuf_ref, bbuf_ref, mbuf_ref), lambda i: proj(OFF_GL + i * D_MODEL, D_MODEL), bg_ref,
                  (wco_ref, wro_ref, wmo_ref), wo_ref, None, y_ref)


def _const(shape):
    nd = len(shape)
    return pl.BlockSpec(tuple(shape), lambda *_: (0,) * nd, pipeline_mode=pl.Buffered(1))


def _prompt_mixer(layer, x, tabs, mk_b, mv_b, w, to_cast, ret_stack):
    bp, seq, _ = x.shape
    tq = PROMPT_TILE
    grid = (bp, seq // tq)
    steps = bp * (seq // tq)
    steps_per_seq = seq // tq
    cast_in_specs, cast_out_specs, cast_out_shapes = [], [], []
    for a, src_layer in to_cast:
        _, rows, cols = a.shape
        slab = rows // steps
        assert slab * steps == rows and slab % 16 == 0
        cast_in_specs.append(pl.BlockSpec(
            (None, slab, cols), lambda b, t, src_layer=src_layer: (src_layer, b * steps_per_seq + t, 0)))
        cast_out_specs.append(pl.BlockSpec((slab, cols), lambda b, t: (b * steps_per_seq + t, 0)))
        cast_out_shapes.append(jax.ShapeDtypeStruct((rows, cols), BF16))
    in_specs = [
        pl.BlockSpec(memory_space=pltpu.SMEM),
        pl.BlockSpec((None, tq, D_MODEL), lambda b, t: (b, t, 0)),
        pl.BlockSpec((tq, RET_DK), lambda b, t: (t, 0)),
        pl.BlockSpec((tq, RET_DK), lambda b, t: (t, 0)),
        _const((RET_HEADS, MIXER_CHUNK, MIXER_CHUNK)),
        _const((RET_HEADS, MIXER_CHUNK, RET_DV)),
        _const((RET_HEADS, MIXER_CHUNK, RET_DK)),
        pl.BlockSpec((None, None, MEM_LEN, MEM_WIDTH), lambda b, t: (layer, b, 0, 0)),
        pl.BlockSpec((None, None, MEM_LEN, MEM_WIDTH), lambda b, t: (layer, b, 0, 0)),
        _const((D_MODEL, IN_WIDTH)),
        _const((1, N_BRANCH * D_MODEL)),
        _const((CONV_K, CONV_WIDTH)),
        _const((1, RET_V_WIDTH)),
        _const((CONV_WIDTH, D_MODEL)),
        _const((RET_V_WIDTH, D_MODEL)),
        _const((MEM_WIDTH, D_MODEL)),
        _const((D_MODEL, D_MODEL)),
    ]
    out_specs = [
        pl.BlockSpec((None, tq, D_MODEL), lambda b, t: (b, t, 0)),
        pl.BlockSpec((None, 8, CONV_WIDTH), lambda b, t: (b, 0, 0)),
        pl.BlockSpec((None, None, RET_HEADS, RET_DK, RET_DV), lambda b, t: (layer, b, 0, 0, 0)),
    ]
    out_shape = [
        jax.ShapeDtypeStruct((bp, seq, D_MODEL), F32),
        jax.ShapeDtypeStruct((bp, 8, CONV_WIDTH), F32),
        jax.ShapeDtypeStruct((DEPTH, bp, RET_HEADS, RET_DK, RET_DV), F32),
    ]
    assert len(in_specs) == MIXER_INPUTS and len(out_specs) == MIXER_OUTPUTS
    args = (tabs["cd"], x, tabs["cos"], tabs["sin"], tabs["dmask"], tabs["qdec"], tabs["kdec"], mk_b, mv_b,
            w["w_in"], w["b_gate"], w["conv_w"], w["ret_gn_g"], w["w_conv_out"], w["w_ret_out"], w["w_mem_out"],
            w["w_out"], *[a for a, _ in to_cast])
    in_specs = in_specs + cast_in_specs
    aliases = {}
    if ret_stack is not None:
        aliases = {0: MIXER_OUTPUTS - 1}
        in_specs = [pl.BlockSpec(memory_space=pl.ANY)] + in_specs
        args = (ret_stack,) + args
    return pl.pallas_call(
        functools.partial(_prompt_mixer_kernel, ret_stack is not None, len(to_cast)),
        grid=grid,
        in_specs=in_specs,
        out_specs=out_specs + cast_out_specs,
        out_shape=out_shape + cast_out_shapes,
        scratch_shapes=[
            pltpu.VMEM((tq, D_MODEL), BF16),
            pltpu.VMEM((tq, RET_QK_WIDTH), BF16),
            pltpu.VMEM((tq, RET_QK_WIDTH), BF16),
            pltpu.VMEM((tq, RET_QK_WIDTH), BF16),
            pltpu.VMEM((tq, RET_V_WIDTH), BF16),
            pltpu.VMEM((tq, RET_HEADS * MIXER_CHUNK), BF16),
            pltpu.VMEM((RET_HEADS, tq // MIXER_CHUNK, RET_DK, RET_DV), F32),
            pltpu.VMEM((tq, CONV_WIDTH), BF16),
            pltpu.VMEM((tq, RET_V_WIDTH), BF16),
            pltpu.VMEM((tq, MEM_WIDTH), BF16),
        ],
        input_output_aliases=aliases,
        compiler_params=pltpu.CompilerParams(
            dimension_semantics=("arbitrary", "arbitrary"), vmem_limit_bytes=VMEM_LIMIT_BYTES),
        name="prompt_mixer",
    )(*args)


FFN_INPUTS = 8
STATE_INPUTS = 13
STATE_OUTPUTS = 5


def _ffn_state_kernel(aliased, *refs):
    if aliased:
        refs = refs[1:]
    z_first_ref, z_next_ref, l1g_ref, l1b_ref, wup_ref, wdn_ref, g_ref, b_ref = refs[:FFN_INPUTS]
    state_in = refs[FFN_INPUTS:FFN_INPUTS + STATE_INPUTS]
    y_ref = refs[FFN_INPUTS + STATE_INPUTS]
    rest = refs[FFN_INPUTS + STATE_INPUTS + 1:]
    x1_scr = rest[-1]
    stages = _sample_state_stages(*state_in, *rest[:-1])
    n_chunks = D_FF // FFN_CHUNK
    assert len(stages) == 2 * n_chunks - 1

    @pl.when(pl.program_id(0) == 0)
    def _():
        x1_scr[...] = _layer_norm(z_first_ref[...], l1g_ref[...], l1b_ref[...])

    x = x1_scr[...]
    x1_scr[...] = _layer_norm(z_next_ref[...], l1g_ref[...], l1b_ref[...])
    xb = x.astype(BF16)
    acc = None
    for c in range(n_chunks):
        cols = slice(c * FFN_CHUNK, (c + 1) * FFN_CHUNK)
        h = jnp.maximum(_dot(xb, wup_ref[:, cols]), 0.0)
        stages[2 * c]()
        part = _dot((h * h).astype(BF16), wdn_ref[cols, :])
        if c + 1 < n_chunks:
            stages[2 * c + 1]()
        acc = part if acc is None else acc + part
    y_ref[...] = _layer_norm(DEEPNORM_ALPHA * x + acc, g_ref[...], b_ref[...])


def _ffn_and_sample_state(layer, z, w, proj, cst, tabs, state_ret, kc, vc, ret_stack):
    n = z.shape[0]
    tm = FFN_TILE
    ns = SAMPLE_SEQS
    n_tok = proj.shape[0]
    nseq = state_ret.shape[1]
    t_len = n_tok // nseq
    rows = ns * t_len
    n_tiles = n // tm
    assert n_tiles == nseq // ns
    tok_spec = lambda width: pl.BlockSpec((rows, width), lambda i: (i, 0))
    in_specs = [
        pl.BlockSpec((tm, D_MODEL), lambda i: (0, 0)),
        pl.BlockSpec((tm, D_MODEL), lambda i: (jnp.minimum(i + 1, n_tiles - 1), 0)),
        _const((1, D_MODEL)),
        _const((1, D_MODEL)),
        _const((D_MODEL, D_FF)),
        _const((D_FF, D_MODEL)),
        _const((1, D_MODEL)),
        _const((1, D_MODEL)),
        pl.BlockSpec(memory_space=pltpu.SMEM),
        tok_spec(OFF_GL),
        tok_spec(CONV_WIDTH),
        _const((rows, RET_DK)),
        _const((rows, RET_DK)),
        _const((RET_HEADS, rows, rows)),
        _const((RET_HEADS, t_len, RET_DV)),
        _const((RET_HEADS, rows, RET_DK)),
        pl.BlockSpec((None, ns, RET_HEADS, RET_DK, RET_DV), lambda i: (layer, i, 0, 0, 0)),
        pl.BlockSpec((None, ns, MEM_LEN * MEM_HEADS, MEM_HEAD_DIM), lambda i: (layer, i, 0, 0)),
        pl.BlockSpec((None, ns, MEM_LEN * MEM_HEADS, MEM_HEAD_DIM), lambda i: (layer, i, 0, 0)),
        _const((CONV_K, CONV_WIDTH)),
        _const((1, RET_V_WIDTH)),
    ]
    out_specs = [
        pl.BlockSpec((tm, D_MODEL), lambda i: (i, 0)),
        tok_spec(CONV_WIDTH),
        tok_spec(RET_V_WIDTH),
        tok_spec(MEM_WIDTH),
        tok_spec(CONV_WIDTH),
        pl.BlockSpec((None, ns, RET_HEADS, RET_DK, RET_DV), lambda i: (layer, i, 0, 0, 0)),
    ]
    out_shape = [
        jax.ShapeDtypeStruct((n, D_MODEL), F32),
        jax.ShapeDtypeStruct((n_tok, CONV_WIDTH), BF16),
        jax.ShapeDtypeStruct((n_tok, RET_V_WIDTH), BF16),
        jax.ShapeDtypeStruct((n_tok, MEM_WIDTH), BF16),
        jax.ShapeDtypeStruct((n_tok, CONV_WIDTH), F32),
        jax.ShapeDtypeStruct((DEPTH, nseq, RET_HEADS, RET_DK, RET_DV), F32),
    ]
    assert len(in_specs) == FFN_INPUTS + STATE_INPUTS and len(out_specs) == 1 + STATE_OUTPUTS
    args = (z, z, w["ln1_g"], w["ln1_b"], w["w_up"], w["w_down"], w["ln2_g"], w["ln2_b"],
            tabs["cd"], proj, cst, tabs["cos"], tabs["sin"], tabs["dmask"], tabs["qdec"], tabs["kdec"],
            state_ret, kc, vc, w["conv_w"], w["ret_gn_g"])
    aliases = {}
    if ret_stack is not None:
        aliases = {0: len(out_shape) - 1}
        in_specs = [pl.BlockSpec(memory_space=pl.ANY)] + in_specs
        args = (ret_stack,) + args
    scratch = [
        pltpu.VMEM((rows, RET_V_WIDTH), F32),
        pltpu.VMEM((rows, MEM_HEADS * MEM_LEN), F32),
        pltpu.VMEM((rows, MEM_WIDTH), F32),
        pltpu.VMEM((tm, D_MODEL), F32),
    ]
    return pl.pallas_call(
        functools.partial(_ffn_state_kernel, ret_stack is not None),
        grid=(n // tm,),
        in_specs=in_specs,
        out_specs=out_specs,
        out_shape=out_shape,
        scratch_shapes=scratch,
        input_output_aliases=aliases,
        compiler_params=pltpu.CompilerParams(
            dimension_semantics=("arbitrary",), vmem_limit_bytes=VMEM_LIMIT_BYTES),
        name="ffn_state",
    )(*args)


def _sample_proj_kernel(x_ref, w_ref, o_ref):
    o_ref[...] = _dot(x_ref[...].astype(BF16), w_ref[...])


def _sample_proj(x, w_in_b):
    n = x.shape[0]
    tn = SAMPLE_PROJ_COLS
    return pl.pallas_call(
        _sample_proj_kernel,
        grid=(IN_WIDTH // tn,),
        in_specs=[
            pl.BlockSpec((n, D_MODEL), lambda j: (0, 0)),
            pl.BlockSpec((D_MODEL, tn), lambda j: (0, j)),
        ],
        out_specs=pl.BlockSpec((n, tn), lambda j: (0, j)),
        out_shape=jax.ShapeDtypeStruct((n, IN_WIDTH), F32),
        compiler_params=pltpu.CompilerParams(
            dimension_semantics=("arbitrary",), vmem_limit_bytes=VMEM_LIMIT_BYTES),
        name="sample_proj",
    )(x, w_in_b)


def _sample_state_stages(cd_ref, proj_ref, cst_ref, cos_ref, sin_ref, dmask_ref, qdec_ref, kdec_ref,
                         sret_ref, kc_ref, vc_ref, cw_ref, gn_ref,
                         a_ref, b_ref, m_ref, u_ref, nret_ref,
                         o_scr, p_scr, m_scr):
    ns = sret_ref.shape[0]
    t_len = proj_ref.shape[0] // ns
    rows = [slice(s * t_len, (s + 1) * t_len) for s in range(ns)]
    kcols = [slice(h * RET_DK, (h + 1) * RET_DK) for h in range(RET_HEADS)]
    vcols = [slice(h * RET_DV, (h + 1) * RET_DV) for h in range(RET_HEADS)]
    mcols = [slice(h * MEM_HEAD_DIM, (h + 1) * MEM_HEAD_DIM) for h in range(MEM_HEADS)]
    pcols = [slice(h * MEM_LEN, (h + 1) * MEM_LEN) for h in range(MEM_HEADS)]
    head_rows = [pl.ds(h, MEM_LEN, stride=MEM_HEADS) for h in range(MEM_HEADS)]

    q, kd, masked = [], [], []

    def conv_and_scores():
        cb = proj_ref[:, OFF_CB:OFF_CB + CONV_WIDTH]
        u = proj_ref[:, OFF_CC:OFF_CC + CONV_WIDTH] * proj_ref[:, OFF_CH:OFF_CH + CONV_WIDTH]
        tok = lax.broadcasted_iota(jnp.int32, u.shape, 0) & (t_len - 1)
        st2 = cst_ref[...]
        st1 = pltpu.roll(st2, st2.shape[0] - 1, axis=0)
        u1 = jnp.where(tok == 0, st1, pltpu.roll(u, 1, axis=0))
        u2 = jnp.where(tok < 2, st2, pltpu.roll(u, 2, axis=0))
        conv_y = u2 * cw_ref[0:1, :] + u1 * cw_ref[1:2, :] + u * cw_ref[2:3, :]
        u_ref[...] = u
        a_ref[...] = (cb * conv_y).astype(BF16)

        cos = cos_ref[...]
        sin = sin_ref[...]
        for h in range(RET_HEADS):
            qh = _rotary(proj_ref[:, OFF_RQ + h * RET_DK:OFF_RQ + (h + 1) * RET_DK], cos, sin)
            kh = _rotary(proj_ref[:, OFF_RK + h * RET_DK:OFF_RK + (h + 1) * RET_DK], cos, sin) * (RET_DK ** -0.5)
            q.append(qh)
            kd.append(kh * kdec_ref[h])
            masked.append((_dot_nt(qh.astype(BF16), kh.astype(BF16)) * dmask_ref[h]).astype(BF16))

    def intra_chunk():
        for h in range(RET_HEADS):
            vb = proj_ref[:, OFF_RV + h * RET_DV:OFF_RV + (h + 1) * RET_DV].astype(BF16)
            o_scr[:, vcols[h]] = _dot(masked[h], vb)

    def cross_chunk_and_state(seqs):
        for s in seqs:
            for h in range(RET_HEADS):
                s_prev = sret_ref[s, h]
                inter = _dot(q[h][rows[s]].astype(BF16), s_prev.astype(BF16))
                o_scr[rows[s], vcols[h]] += inter * qdec_ref[h]
                vb = proj_ref[rows[s], OFF_RV + h * RET_DV:OFF_RV + (h + 1) * RET_DV].astype(BF16)
                nret_ref[s, h] = s_prev * cd_ref[h] + _dot_tn(kd[h][rows[s]].astype(BF16), vb)

    def attention_scores():
        for s in range(ns):
            for h in range(MEM_HEADS):
                mq = proj_ref[rows[s], OFF_MQ + h * MEM_HEAD_DIM:OFF_MQ + (h + 1) * MEM_HEAD_DIM].astype(BF16)
                p_scr[rows[s], pcols[h]] = (_dot_nt(mq, kc_ref[s, head_rows[h], :].astype(BF16))
                                            * (MEM_HEAD_DIM ** -0.5))

    def attention_values():
        for h in range(MEM_HEADS):
            p_scr[:, pcols[h]] = _softmax_rows(p_scr[:, pcols[h]])
        for s in range(ns):
            for h in range(MEM_HEADS):
                p = p_scr[rows[s], pcols[h]].astype(BF16)
                m_scr[rows[s], mcols[h]] = _dot(p, vc_ref[s, head_rows[h], :].astype(BF16))
        m_ref[...] = m_scr[...].astype(BF16)

    def norm_and_gate():
        for h in range(RET_HEADS):
            rg = proj_ref[:, OFF_RG + h * RET_DV:OFF_RG + (h + 1) * RET_DV]
            gated = jax.nn.silu(rg) * (_group_norm(o_scr[:, vcols[h]]) * gn_ref[:, vcols[h]])
            b_ref[:, vcols[h]] = gated.astype(BF16)

    first, second = list(range(ns // 2)), list(range(ns // 2, ns))
    return [conv_and_scores, intra_chunk,
            functools.partial(cross_chunk_and_state, first), functools.partial(cross_chunk_and_state, second),
            attention_scores, attention_values, norm_and_gate]


def _sample_post_kernel(x_ref, a_ref, b_ref, m_ref, g0_ref, g1_ref, g2_ref,
                        bg_ref, wco_ref, wro_ref, wmo_ref, wo_ref, l1g_ref, l1b_ref,
                        wup_ref, wdn_ref, l2g_ref, l2b_ref, y_ref, x1_ref):
    gate_refs = (g0_ref, g1_ref, g2_ref)
    _merge_out_ln(x_ref, (a_ref, b_ref, m_ref), lambda i: gate_refs[i][...], bg_ref,
                  (wco_ref, wro_ref, wmo_ref), wo_ref, (l1g_ref, l1b_ref), x1_ref)
    y_ref[...] = _ffn_ln(x1_ref[...], wup_ref, wdn_ref, l2g_ref, l2b_ref)


def _sample_post(x, a_pre, b_pre, m_pre, proj, w):
    n = x.shape[0]
    tm = SAMPLE_POST_TILE
    tok = lambda width: pl.BlockSpec((tm, width), lambda i: (i, 0))
    gate = lambda k: pl.BlockSpec((tm, D_MODEL), lambda i: (i, OFF_GL // D_MODEL + k))
    return pl.pallas_call(
        _sample_post_kernel,
        grid=(n // tm,),
        in_specs=[
            tok(D_MODEL), tok(CONV_WIDTH), tok(RET_V_WIDTH), tok(MEM_WIDTH), gate(0), gate(1), gate(2),
            _const((1, N_BRANCH * D_MODEL)),
            _const((CONV_WIDTH, D_MODEL)),
            _const((RET_V_WIDTH, D_MODEL)),
            _const((MEM_WIDTH, D_MODEL)),
            _const((D_MODEL, D_MODEL)),
            _const((1, D_MODEL)),
            _const((1, D_MODEL)),
            _const((D_MODEL, D_FF)),
            _const((D_FF, D_MODEL)),
            _const((1, D_MODEL)),
            _const((1, D_MODEL)),
        ],
        out_specs=tok(D_MODEL),
        out_shape=jax.ShapeDtypeStruct((n, D_MODEL), F32),
        scratch_shapes=[pltpu.VMEM((tm, D_MODEL), F32)],
        compiler_params=pltpu.CompilerParams(
            dimension_semantics=("arbitrary",), vmem_limit_bytes=VMEM_LIMIT_BYTES),
        name="sample_post",
    )(x, a_pre, b_pre, m_pre, proj, proj, proj,
      w["b_gate"], w["w_conv_out"], w["w_ret_out"], w["w_mem_out"], w["w_out"], w["ln1_g"], w["ln1_b"],
      w["w_up"], w["w_down"], w["ln2_g"], w["ln2_b"])


def _rotary_tables(pos, reps):
    half = RET_DK // 2
    inv = ROPE_BASE ** (-jnp.arange(half, dtype=F32) / half)
    ang = pos.astype(F32)[:, None] * inv[None, :]
    cos = jnp.cos(ang)
    sin = jnp.sin(ang)
    cos_full = jnp.concatenate([cos, cos], axis=-1)
    sin_signed = jnp.concatenate([-sin, sin], axis=-1)
    return jnp.tile(cos_full, (reps, 1)), jnp.tile(sin_signed, (reps, 1))


def _decay_tables(c):
    h = RET_HEADS
    log_g = jnp.log1p(-jnp.exp2(-5.0 - jnp.arange(h, dtype=F32)))
    idx = jnp.arange(c, dtype=F32)
    rel = idx[:, None] - idx[None, :]
    causal = rel >= 0
    dmask = jnp.where(causal[None], jnp.exp(jnp.where(causal, rel, 0.0)[None] * log_g[:, None, None]), 0.0)
    q_decay = jnp.exp((idx + 1.0)[None, :] * log_g[:, None])
    k_decay = jnp.exp((c - 1.0 - idx)[None, :] * log_g[:, None])
    chunk_decay = jnp.exp(c * log_g)
    return {
        "dmask": dmask,
        "qdec": jnp.broadcast_to(q_decay[:, :, None], (h, c, RET_DV)),
        "kdec": jnp.broadcast_to(k_decay[:, :, None], (h, c, RET_DK)),
        "cd": chunk_decay,
    }


def kernel(x_prompt, x_sample, cache_mem_k, cache_mem_v, state_conv, state_ret, mem_prompt,
           w_in, b_gate, conv_w, ret_gn_g, w_conv_out, w_ret_out, w_mem_out, w_out, w_mem_kv,
           ln1_g, ln1_b, w_up, w_down, ln2_g, ln2_b):
    bp, seq, _ = x_prompt.shape
    nseq, t_len, _ = x_sample.shape
    assert seq % PROMPT_TILE == 0 and PROMPT_TILE % MIXER_CHUNK == 0
    assert t_len == 8 and nseq % SAMPLE_SEQS == 0 and t_len % RET_CHUNK != 0

    mixer_w = {"w_in": w_in, "w_conv_out": w_conv_out, "w_ret_out": w_ret_out, "w_mem_out": w_mem_out, "w_out": w_out}
    mlp_w = {"w_up": w_up, "w_down": w_down}
    small = {"b_gate": b_gate, "ret_gn_g": ret_gn_g, "ln1_g": ln1_g, "ln1_b": ln1_b, "ln2_g": ln2_g, "ln2_b": ln2_b}

    tabs_p = _decay_tables(MIXER_CHUNK)
    tabs_p["cos"], tabs_p["sin"] = _rotary_tables(jnp.arange(seq, dtype=jnp.int32), 1)
    tabs_s = _decay_tables(t_len)
    eye = jnp.eye(SAMPLE_SEQS, dtype=F32)
    tabs_s["dmask"] = (eye[None, :, None, :, None] * tabs_s["dmask"][:, None, :, None, :]).reshape(
        RET_HEADS, SAMPLE_SEQS * t_len, SAMPLE_SEQS * t_len)
    tabs_s["kdec"] = jnp.tile(tabs_s["kdec"], (1, SAMPLE_SEQS, 1))
    tabs_s["cos"], tabs_s["sin"] = _rotary_tables(PAST_LEN + jnp.arange(t_len, dtype=jnp.int32), SAMPLE_SEQS)

    mk_f, mv_f, mk_b, mv_b, *cast = _mem_kv(mem_prompt, w_mem_kv, list(mixer_w.values()))
    mixer_b = dict(zip(mixer_w, cast))
    mk_b = mk_b.reshape(DEPTH, bp, MEM_LEN, MEM_WIDTH)
    mv_b = mv_b.reshape(DEPTH, bp, MEM_LEN, MEM_WIDTH)

    kc = cache_mem_k.reshape(DEPTH, nseq, MEM_LEN * MEM_HEADS, MEM_HEAD_DIM)
    vc = cache_mem_v.reshape(DEPTH, nseq, MEM_LEN * MEM_HEADS, MEM_HEAD_DIM)
    cst = jnp.concatenate(
        [state_conv, jnp.zeros((DEPTH, nseq, t_len - (CONV_K - 1), CONV_WIDTH), state_conv.dtype)], axis=2
    ).reshape(DEPTH, nseq * t_len, CONV_WIDTH)

    xp = x_prompt
    xs = x_sample.reshape(nseq * t_len, D_MODEL)
    cp_list, cs_list = [], []
    ret_stack = ret_stack_p = None
    for l in range(DEPTH):
        w = dict(mixer_b, conv_w=conv_w[l], **{name: a[l].reshape(1, -1) for name, a in small.items()})
        to_cast = [(a, l) for a in mlp_w.values()]
        if l + 1 < DEPTH:
            to_cast += [(a, l + 1) for a in mixer_w.values()]
        zp, ctail_p, ret_stack_p, *cast = _prompt_mixer(l, xp, tabs_p, mk_b, mv_b, w, to_cast, ret_stack_p)
        w.update(zip(mlp_w, cast[:len(mlp_w)]))
        mixer_b = dict(zip(mixer_w, cast[len(mlp_w):]))
        cp_list.append(ctail_p[:, 8 - (CONV_K - 1):, :])

        proj = _sample_proj(xs, w["w_in"])
        xp, a_pre, b_pre, m_pre, u_s, ret_stack = _ffn_and_sample_state(
            l, zp.reshape(bp * seq, D_MODEL), w, proj, cst[l], tabs_s, state_ret, kc, vc, ret_stack)
        xp = xp.reshape(bp, seq, D_MODEL)
        xs = _sample_post(xs, a_pre, b_pre, m_pre, proj, w)
        cs_list.append(u_s.reshape(nseq, t_len, CONV_WIDTH)[:, t_len - (CONV_K - 1):, :])

    mem_shape = (DEPTH, bp, MEM_LEN, MEM_HEADS, MEM_HEAD_DIM)
    return (xp, xs.reshape(nseq, t_len, D_MODEL), mk_f.reshape(mem_shape), mv_f.reshape(mem_shape),
            jnp.stack(cp_list), ret_stack_p, jnp.stack(cs_list), ret_stack)
```

```python
import functools

import jax
import jax.numpy as jnp
from jax import lax
from jax.experimental import pallas as pl
from jax.experimental.pallas import tpu as pltpu

D_MODEL = 1024
DEPTH = 4
CONV_WIDTH = 512
CONV_K = 3
RET_HEADS = 4
RET_DK = 128
RET_DV = 256
RET_QK_WIDTH = RET_HEADS * RET_DK
RET_V_WIDTH = RET_HEADS * RET_DV
RET_CHUNK = 128
MEM_LEN = 256
MEM_HEADS = 4
MEM_HEAD_DIM = 128
MEM_WIDTH = MEM_HEADS * MEM_HEAD_DIM
N_BRANCH = 3
D_FF = 4 * D_MODEL
ROPE_BASE = 10000.0
LN_EPS = 1e-5
GN_EPS = 1e-6
PAST_LEN = 16384
DEEPNORM_ALPHA = (2 * DEPTH) ** 0.25

OFF_CB = 0
OFF_CC = OFF_CB + CONV_WIDTH
OFF_CH = OFF_CC + CONV_WIDTH
OFF_RQ = OFF_CH + CONV_WIDTH
OFF_RK = OFF_RQ + RET_QK_WIDTH
OFF_RV = OFF_RK + RET_QK_WIDTH
OFF_RG = OFF_RV + RET_V_WIDTH
OFF_MQ = OFF_RG + RET_V_WIDTH
OFF_GL = OFF_MQ + MEM_WIDTH
IN_WIDTH = OFF_GL + N_BRANCH * D_MODEL

V7X_VMEM_BYTES = 64 * 1024 * 1024
VMEM_LIMIT_BYTES = V7X_VMEM_BYTES - 6 * 1024 * 1024

PROMPT_TILE = 512
MIXER_CHUNK = 256
FFN_TILE = 512
FFN_CHUNK = 1024
SAMPLE_SEQS = 4
SAMPLE_POST_TILE = 512
SAMPLE_PROJ_COLS = 2048

BF16 = jnp.bfloat16
F32 = jnp.float32


def _dot(a, b):
    return jnp.dot(a, b, preferred_element_type=F32)


def _dot_nt(a, b):
    return lax.dot_general(a, b, (((1,), (1,)), ((), ())), preferred_element_type=F32)


def _dot_tn(a, b):
    return lax.dot_general(a, b, (((0,), (0,)), ((), ())), preferred_element_type=F32)


def _layer_norm(z, g, b):
    mu = jnp.mean(z, axis=-1, keepdims=True)
    zc = z - mu
    var = jnp.mean(zc * zc, axis=-1, keepdims=True)
    return zc * lax.rsqrt(var + LN_EPS) * g + b


def _group_norm(o):
    mu = jnp.mean(o, axis=-1, keepdims=True)
    oc = o - mu
    var = jnp.mean(oc * oc, axis=-1, keepdims=True)
    return oc * lax.rsqrt(var + GN_EPS)


def _rotary(xh, cos, sin_signed):
    return xh * cos + pltpu.roll(xh, RET_DK // 2, axis=1) * sin_signed


def _softmax_rows(s):
    m = jnp.max(s, axis=-1, keepdims=True)
    e = jnp.exp(s - m)
    return e / jnp.sum(e, axis=-1, keepdims=True)


def _merge_out_ln(x_ref, pre_refs, gate_logits, bg_ref, w_refs, wo_ref, ln_refs, y_ref):
    merged = None
    for i, (pre_ref, w_ref) in enumerate(zip(pre_refs, w_refs)):
        gate = jax.nn.sigmoid(gate_logits(i) + bg_ref[:, i * D_MODEL:(i + 1) * D_MODEL])
        term = gate * _dot(pre_ref[...], w_ref[...])
        merged = term if merged is None else merged + term
    merged = merged.astype(BF16)
    half = merged.shape[0] // 2
    for rows in (slice(0, half), slice(half, 2 * half)):
        z = DEEPNORM_ALPHA * x_ref[rows, :] + _dot(merged[rows], wo_ref[...])
        y_ref[rows, :] = z if ln_refs is None else _layer_norm(z, ln_refs[0][...], ln_refs[1][...])


def _ffn_ln(x, wup_ref, wdn_ref, g_ref, b_ref):
    xb = x.astype(BF16)
    acc = None
    for c in range(D_FF // FFN_CHUNK):
        cols = slice(c * FFN_CHUNK, (c + 1) * FFN_CHUNK)
        h = jnp.maximum(_dot(xb, wup_ref[:, cols]), 0.0)
        part = _dot((h * h).astype(BF16), wdn_ref[cols, :])
        acc = part if acc is None else acc + part
    return _layer_norm(DEEPNORM_ALPHA * x + acc, g_ref[...], b_ref[...])


MEM_KV_ROWS = 512
MEM_KV_OUTPUTS = 4


def _mem_kv_kernel(n_cast, mem_ref, w_ref, *refs):
    cast_src = refs[:n_cast]
    k_ref, v_ref, kb_ref, vb_ref = refs[n_cast:n_cast + MEM_KV_OUTPUTS]
    cast_dst = refs[n_cast + MEM_KV_OUTPUTS:n_cast + MEM_KV_OUTPUTS + n_cast]
    wb_ref = refs[-1]
    j = pl.program_id(1)

    for src_ref, dst_ref in zip(cast_src, cast_dst):
        dst_ref[...] = src_ref[...].astype(BF16)

    @pl.when(j == 0)
    def _():
        wb_ref[...] = w_ref[...].astype(BF16)

    rows = pl.ds(pl.multiple_of(j * MEM_KV_ROWS, MEM_KV_ROWS), MEM_KV_ROWS)
    kv = _dot(mem_ref[rows, :].astype(BF16), wb_ref[...])
    kb_ref[...] = kv[:, :MEM_WIDTH].astype(BF16)
    vb_ref[...] = kv[:, MEM_WIDTH:].astype(BF16)
    for h in range(MEM_HEADS):
        head_rows = pl.ds(h, MEM_KV_ROWS, stride=MEM_HEADS)
        k_ref[head_rows, :] = kv[:, h * MEM_HEAD_DIM:(h + 1) * MEM_HEAD_DIM]
        v_ref[head_rows, :] = kv[:, MEM_WIDTH + h * MEM_HEAD_DIM:MEM_WIDTH + (h + 1) * MEM_HEAD_DIM]


def _mem_kv(mem_prompt, w_mem_kv, to_cast):
    bp = mem_prompt.shape[0]
    n = bp * MEM_LEN
    nsub = n // MEM_KV_ROWS
    steps = DEPTH * nsub
    assert nsub * MEM_KV_ROWS == n
    cast_in_specs, cast_out_specs, cast_out_shapes = [], [], []
    for a in to_cast:
        _, rows, cols = a.shape
        slab = rows // steps
        assert slab * steps == rows and slab % 16 == 0
        cast_in_specs.append(pl.BlockSpec((None, slab, cols), lambda l, j: (0, l * nsub + j, 0)))
        cast_out_specs.append(pl.BlockSpec((slab, cols), lambda l, j: (l * nsub + j, 0)))
        cast_out_shapes.append(jax.ShapeDtypeStruct((rows, cols), BF16))
    out_f = jax.ShapeDtypeStruct((DEPTH, n * MEM_HEADS, MEM_HEAD_DIM), F32)
    out_b = jax.ShapeDtypeStruct((DEPTH, n, MEM_WIDTH), BF16)
    f_spec = pl.BlockSpec((None, MEM_KV_ROWS * MEM_HEADS, MEM_HEAD_DIM), lambda l, j: (l, j, 0))
    b_spec = pl.BlockSpec((None, MEM_KV_ROWS, MEM_WIDTH), lambda l, j: (l, j, 0))
    return pl.pallas_call(
        functools.partial(_mem_kv_kernel, len(to_cast)),
        grid=(DEPTH, nsub),
        in_specs=[
            _const((n, D_MODEL)),
            pl.BlockSpec((None, D_MODEL, 2 * MEM_WIDTH), lambda l, j: (l, 0, 0)),
        ] + cast_in_specs,
        out_specs=[f_spec, f_spec, b_spec, b_spec] + cast_out_specs,
        out_shape=[out_f, out_f, out_b, out_b] + cast_out_shapes,
        scratch_shapes=[pltpu.VMEM((D_MODEL, 2 * MEM_WIDTH), BF16)],
        compiler_params=pltpu.CompilerParams(
            dimension_semantics=("arbitrary", "arbitrary"), vmem_limit_bytes=VMEM_LIMIT_BYTES),
        name="mem_kv",
    )(mem_prompt.reshape(n, D_MODEL), w_mem_kv, *to_cast)


MIXER_INPUTS = 17
MIXER_OUTPUTS = 3


def _prompt_mixer_kernel(aliased, n_cast, *refs):
    if aliased:
        refs = refs[1:]
    (cd_ref, x_ref, cos_ref, sin_ref, dmask_ref, qdec_ref, kdec_ref, mk_ref, mv_ref,
     win_ref, bg_ref, cw_ref, gn_ref, wco_ref, wro_ref, wmo_ref, wo_ref) = refs[:MIXER_INPUTS]
    cast_src = refs[MIXER_INPUTS:MIXER_INPUTS + n_cast]
    outs = refs[MIXER_INPUTS + n_cast:]
    y_ref, ctail_ref, ret_ref = outs[:MIXER_OUTPUTS]
    cast_dst = outs[MIXER_OUTPUTS:MIXER_OUTPUTS + n_cast]
    (xb_ref, q_ref, k_ref, kd_ref, vb_ref, msk_ref, kv_ref,
     abuf_ref, bbuf_ref, mbuf_ref) = outs[MIXER_OUTPUTS + n_cast:]

    for src_ref, dst_ref in zip(cast_src, cast_dst):
        dst_ref[...] = src_ref[...].astype(BF16)

    t = pl.program_id(1)
    tq = x_ref.shape[0]
    chunk = dmask_ref.shape[1]
    chunks = [slice(j * chunk, (j + 1) * chunk) for j in range(tq // chunk)]
    scols = [slice(h * chunk, (h + 1) * chunk) for h in range(RET_HEADS)]
    kcols = [slice(h * RET_DK, (h + 1) * RET_DK) for h in range(RET_HEADS)]
    vcols = [slice(h * RET_DV, (h + 1) * RET_DV) for h in range(RET_HEADS)]
    mcols = [slice(h * MEM_HEAD_DIM, (h + 1) * MEM_HEAD_DIM) for h in range(MEM_HEADS)]

    @pl.when(t == 0)
    def _():
        ret_ref[...] = jnp.zeros_like(ret_ref)
        ctail_ref[...] = jnp.zeros_like(ctail_ref)

    xb_ref[...] = x_ref[...].astype(BF16)

    def proj(lo, width):
        return _dot(xb_ref[...], win_ref[:, lo:lo + width])

    pq = proj(OFF_RQ, RET_QK_WIDTH)
    pk = proj(OFF_RK, RET_QK_WIDTH)
    vb_ref[...] = proj(OFF_RV, RET_V_WIDTH).astype(BF16)
    cos = cos_ref[...]
    sin = sin_ref[...]
    for h in range(RET_HEADS):
        q_ref[:, kcols[h]] = _rotary(pq[:, kcols[h]], cos, sin).astype(BF16)
        kh = _rotary(pk[:, kcols[h]], cos, sin) * (RET_DK ** -0.5)
        k_ref[:, kcols[h]] = kh.astype(BF16)
        for r in chunks:
            kd_ref[r, kcols[h]] = (kh[r] * kdec_ref[h]).astype(BF16)

    pm = proj(OFF_MQ, MEM_WIDTH).astype(BF16)
    mem_s = [_dot_nt(pm[:, mcols[h]], mk_ref[:, mcols[h]]) * (MEM_HEAD_DIM ** -0.5) for h in range(MEM_HEADS)]

    cb = proj(OFF_CB, CONV_WIDTH)
    u = proj(OFF_CC, CONV_WIDTH) * proj(OFF_CH, CONV_WIDTH)
    row = lax.broadcasted_iota(jnp.int32, u.shape, 0)
    prev1 = ctail_ref[7:8, :]
    prev2 = ctail_ref[6:7, :]
    u1 = jnp.where(row == 0, prev1, pltpu.roll(u, 1, axis=0))
    u2 = jnp.where(row == 0, prev2, jnp.where(row == 1, prev1, pltpu.roll(u, 2, axis=0)))
    conv_y = u2 * cw_ref[0:1, :] + u1 * cw_ref[1:2, :] + u * cw_ref[2:3, :]
    ctail_ref[...] = u[tq - 8:, :]
    abuf_ref[...] = (cb * conv_y).astype(BF16)

    for h in range(MEM_HEADS):
        p = _softmax_rows(mem_s[h])
        mbuf_ref[:, mcols[h]] = _dot(p.astype(BF16), mv_ref[:, mcols[h]]).astype(BF16)

    for h in range(RET_HEADS):
        for r in chunks:
            msk_ref[r, scols[h]] = (_dot_nt(q_ref[r, kcols[h]], k_ref[r, kcols[h]]) * dmask_ref[h]).astype(BF16)

    for h in range(RET_HEADS):
        for j, r in enumerate(chunks):
            kv_ref[h, j] = _dot_tn(kd_ref[r, kcols[h]], vb_ref[r, vcols[h]])

    for h in range(RET_HEADS):
        pg = proj(OFF_RG + h * RET_DV, RET_DV)
        s_cur = ret_ref[h]
        for j, r in enumerate(chunks):
            intra = _dot(msk_ref[r, scols[h]], vb_ref[r, vcols[h]])
            o = intra + _dot(q_ref[r, kcols[h]], s_cur.astype(BF16)) * qdec_ref[h]
            gated = jax.nn.silu(pg[r]) * (_group_norm(o) * gn_ref[:, vcols[h]])
            bbuf_ref[r, vcols[h]] = gated.astype(BF16)
            s_cur = s_cur * cd_ref[h] + kv_ref[h, j]
        ret_ref[h] = s_cur

    _merge_out_ln(x_ref, (abuf_ref, bbuf_ref, mbuf_ref), lambda i: proj(OFF_GL + i * D_MODEL, D_MODEL), bg_ref,
                  (wco_ref, wro_ref, wmo_ref), wo_ref, None, y_ref)


def _const(shape):
    nd = len(shape)
    return pl.BlockSpec(tuple(shape), lambda *_: (0,) * nd, pipeline_mode=pl.Buffered(1))


def _layer_const(shape, layer):
    nd = len(shape)
    return pl.BlockSpec((None,) + tuple(shape), lambda *_: (layer,) + (0,) * nd, pipeline_mode=pl.Buffered(1))


def _prompt_mixer(layer, x, tabs, mk_b, mv_b, w, to_cast, ret_stack):
    bp, seq, _ = x.shape
    tq = PROMPT_TILE
    grid = (bp, seq // tq)
    steps = bp * (seq // tq)
    steps_per_seq = seq // tq
    cast_in_specs, cast_out_specs, cast_out_shapes = [], [], []
    for a, src_layer in to_cast:
        _, rows, cols = a.shape
        slab = rows // steps
        assert slab * steps == rows and slab % 16 == 0
        cast_in_specs.append(pl.BlockSpec(
            (None, slab, cols), lambda b, t, src_layer=src_layer: (src_layer, b * steps_per_seq + t, 0)))
        cast_out_specs.append(pl.BlockSpec((slab, cols), lambda b, t: (b * steps_per_seq + t, 0)))
        cast_out_shapes.append(jax.ShapeDtypeStruct((rows, cols), BF16))
    in_specs = [
        pl.BlockSpec(memory_space=pltpu.SMEM),
        pl.BlockSpec((None, tq, D_MODEL), lambda b, t: (b, t, 0)),
        pl.BlockSpec((tq, RET_DK), lambda b, t: (t, 0)),
        pl.BlockSpec((tq, RET_DK), lambda b, t: (t, 0)),
        _const((RET_HEADS, MIXER_CHUNK, MIXER_CHUNK)),
        _const((RET_HEADS, MIXER_CHUNK, RET_DV)),
        _const((RET_HEADS, MIXER_CHUNK, RET_DK)),
        pl.BlockSpec((None, None, MEM_LEN, MEM_WIDTH), lambda b, t: (layer, b, 0, 0)),
        pl.BlockSpec((None, None, MEM_LEN, MEM_WIDTH), lambda b, t: (layer, b, 0, 0)),
        _const((D_MODEL, IN_WIDTH)),
        _layer_const((1, N_BRANCH * D_MODEL), layer),
        _layer_const((CONV_K, CONV_WIDTH), layer),
        _layer_const((1, RET_V_WIDTH), layer),
        _const((CONV_WIDTH, D_MODEL)),
        _const((RET_V_WIDTH, D_MODEL)),
        _const((MEM_WIDTH, D_MODEL)),
        _const((D_MODEL, D_MODEL)),
    ]
    out_specs = [
        pl.BlockSpec((None, tq, D_MODEL), lambda b, t: (b, t, 0)),
        pl.BlockSpec((None, 8, CONV_WIDTH), lambda b, t: (b, 0, 0)),
        pl.BlockSpec((None, None, RET_HEADS, RET_DK, RET_DV), lambda b, t: (layer, b, 0, 0, 0)),
    ]
    out_shape = [
        jax.ShapeDtypeStruct((bp, seq, D_MODEL), F32),
        jax.ShapeDtypeStruct((bp, 8, CONV_WIDTH), F32),
        jax.ShapeDtypeStruct((DEPTH, bp, RET_HEADS, RET_DK, RET_DV), F32),
    ]
    assert len(in_specs) == MIXER_INPUTS and len(out_specs) == MIXER_OUTPUTS
    args = (tabs["cd"], x, tabs["cos"], tabs["sin"], tabs["dmask"], tabs["qdec"], tabs["kdec"], mk_b, mv_b,
            w["w_in"], w["b_gate"], w["conv_w"], w["ret_gn_g"], w["w_conv_out"], w["w_ret_out"], w["w_mem_out"],
            w["w_out"], *[a for a, _ in to_cast])
    in_specs = in_specs + cast_in_specs
    aliases = {}
    if ret_stack is not None:
        aliases = {0: MIXER_OUTPUTS - 1}
        in_specs = [pl.BlockSpec(memory_space=pl.ANY)] + in_specs
        args = (ret_stack,) + args
    return pl.pallas_call(
        functools.partial(_prompt_mixer_kernel, ret_stack is not None, len(to_cast)),
        grid=grid,
        in_specs=in_specs,
        out_specs=out_specs + cast_out_specs,
        out_shape=out_shape + cast_out_shapes,
        scratch_shapes=[
            pltpu.VMEM((tq, D_MODEL), BF16),
            pltpu.VMEM((tq, RET_QK_WIDTH), BF16),
            pltpu.VMEM((tq, RET_QK_WIDTH), BF16),
            pltpu.VMEM((tq, RET_QK_WIDTH), BF16),
            pltpu.VMEM((tq, RET_V_WIDTH), BF16),
            pltpu.VMEM((tq, RET_HEADS * MIXER_CHUNK), BF16),
            pltpu.VMEM((RET_HEADS, tq // MIXER_CHUNK, RET_DK, RET_DV), F32),
            pltpu.VMEM((tq, CONV_WIDTH), BF16),
            pltpu.VMEM((tq, RET_V_WIDTH), BF16),
            pltpu.VMEM((tq, MEM_WIDTH), BF16),
        ],
        input_output_aliases=aliases,
        compiler_params=pltpu.CompilerParams(
            dimension_semantics=("arbitrary", "arbitrary"), vmem_limit_bytes=VMEM_LIMIT_BYTES),
        name="prompt_mixer",
    )(*args)


FFN_INPUTS = 8
STATE_INPUTS = 13
STATE_OUTPUTS = 5


def _ffn_state_kernel(aliased, *refs):
    if aliased:
        refs = refs[1:]
    z_first_ref, z_next_ref, l1g_ref, l1b_ref, wup_ref, wdn_ref, g_ref, b_ref = refs[:FFN_INPUTS]
    state_in = refs[FFN_INPUTS:FFN_INPUTS + STATE_INPUTS]
    y_ref = refs[FFN_INPUTS + STATE_INPUTS]
    rest = refs[FFN_INPUTS + STATE_INPUTS + 1:]
    x1_scr = rest[-1]
    stages = _sample_state_stages(*state_in, *rest[:-1])
    n_chunks = D_FF // FFN_CHUNK
    assert len(stages) == 2 * n_chunks - 1

    @pl.when(pl.program_id(0) == 0)
    def _():
        x1_scr[...] = _layer_norm(z_first_ref[...], l1g_ref[...], l1b_ref[...])

    x = x1_scr[...]
    x1_scr[...] = _layer_norm(z_next_ref[...], l1g_ref[...], l1b_ref[...])
    xb = x.astype(BF16)
    acc = None
    for c in range(n_chunks):
        cols = slice(c * FFN_CHUNK, (c + 1) * FFN_CHUNK)
        h = jnp.maximum(_dot(xb, wup_ref[:, cols]), 0.0)
        stages[2 * c]()
        part = _dot((h * h).astype(BF16), wdn_ref[cols, :])
        if c + 1 < n_chunks:
            stages[2 * c + 1]()
        acc = part if acc is None else acc + part
    y_ref[...] = _layer_norm(DEEPNORM_ALPHA * x + acc, g_ref[...], b_ref[...])


def _ffn_and_sample_state(layer, z, w, proj, cst, tabs, state_ret, kc, vc, ret_stack):
    n = z.shape[0]
    tm = FFN_TILE
    ns = SAMPLE_SEQS
    n_tok = proj.shape[0]
    nseq = state_ret.shape[1]
    t_len = n_tok // nseq
    rows = ns * t_len
    n_tiles = n // tm
    assert n_tiles == nseq // ns
    tok_spec = lambda width: pl.BlockSpec((rows, width), lambda i: (i, 0))
    in_specs = [
        pl.BlockSpec((tm, D_MODEL), lambda i: (0, 0)),
        pl.BlockSpec((tm, D_MODEL), lambda i: (jnp.minimum(i + 1, n_tiles - 1), 0)),
        _layer_const((1, D_MODEL), layer),
        _layer_const((1, D_MODEL), layer),
        _const((D_MODEL, D_FF)),
        _const((D_FF, D_MODEL)),
        _layer_const((1, D_MODEL), layer),
        _layer_const((1, D_MODEL), layer),
        pl.BlockSpec(memory_space=pltpu.SMEM),
        tok_spec(OFF_GL),
        pl.BlockSpec((None, rows, CONV_WIDTH), lambda i: (layer, i, 0)),
        _const((rows, RET_DK)),
        _const((rows, RET_DK)),
        _const((RET_HEADS, rows, rows)),
        _const((RET_HEADS, t_len, RET_DV)),
        _const((RET_HEADS, rows, RET_DK)),
        pl.BlockSpec((None, ns, RET_HEADS, RET_DK, RET_DV), lambda i: (layer, i, 0, 0, 0)),
        pl.BlockSpec((None, ns, MEM_LEN * MEM_HEADS, MEM_HEAD_DIM), lambda i: (layer, i, 0, 0)),
        pl.BlockSpec((None, ns, MEM_LEN * MEM_HEADS, MEM_HEAD_DIM), lambda i: (layer, i, 0, 0)),
        _layer_const((CONV_K, CONV_WIDTH), layer),
        _layer_const((1, RET_V_WIDTH), layer),
    ]
    out_specs = [
        pl.BlockSpec((tm, D_MODEL), lambda i: (i, 0)),
        tok_spec(CONV_WIDTH),
        tok_spec(RET_V_WIDTH),
        tok_spec(MEM_WIDTH),
        tok_spec(CONV_WIDTH),
        pl.BlockSpec((None, ns, RET_HEADS, RET_DK, RET_DV), lambda i: (layer, i, 0, 0, 0)),
    ]
    out_shape = [
        jax.ShapeDtypeStruct((n, D_MODEL), F32),
        jax.ShapeDtypeStruct((n_tok, CONV_WIDTH), BF16),
        jax.ShapeDtypeStruct((n_tok, RET_V_WIDTH), BF16),
        jax.ShapeDtypeStruct((n_tok, MEM_WIDTH), BF16),
        jax.ShapeDtypeStruct((n_tok, CONV_WIDTH), F32),
        jax.ShapeDtypeStruct((DEPTH, nseq, RET_HEADS, RET_DK, RET_DV), F32),
    ]
    assert len(in_specs) == FFN_INPUTS + STATE_INPUTS and len(out_specs) == 1 + STATE_OUTPUTS
    args = (z, z, w["ln1_g"], w["ln1_b"], w["w_up"], w["w_down"], w["ln2_g"], w["ln2_b"],
            tabs["cd"], proj, cst, tabs["cos"], tabs["sin"], tabs["dmask"], tabs["qdec"], tabs["kdec"],
            state_ret, kc, vc, w["conv_w"], w["ret_gn_g"])
    aliases = {}
    if ret_stack is not None:
        aliases = {0: len(out_shape) - 1}
        in_specs = [pl.BlockSpec(memory_space=pl.ANY)] + in_specs
        args = (ret_stack,) + args
    scratch = [
        pltpu.VMEM((rows, RET_V_WIDTH), F32),
        pltpu.VMEM((rows, MEM_HEADS * MEM_LEN), F32),
        pltpu.VMEM((rows, MEM_WIDTH), F32),
        pltpu.VMEM((tm, D_MODEL), F32),
    ]
    return pl.pallas_call(
        functools.partial(_ffn_state_kernel, ret_stack is not None),
        grid=(n // tm,),
        in_specs=in_specs,
        out_specs=out_specs,
        out_shape=out_shape,
        scratch_shapes=scratch,
        input_output_aliases=aliases,
        compiler_params=pltpu.CompilerParams(
            dimension_semantics=("arbitrary",), vmem_limit_bytes=VMEM_LIMIT_BYTES),
        name="ffn_state",
    )(*args)


def _sample_proj_kernel(x_ref, w_ref, o_ref):
    o_ref[...] = _dot(x_ref[...].astype(BF16), w_ref[...])


def _sample_proj(x, w_in_b):
    n = x.shape[0]
    tn = SAMPLE_PROJ_COLS
    return pl.pallas_call(
        _sample_proj_kernel,
        grid=(IN_WIDTH // tn,),
        in_specs=[
            pl.BlockSpec((n, D_MODEL), lambda j: (0, 0)),
            pl.BlockSpec((D_MODEL, tn), lambda j: (0, j)),
        ],
        out_specs=pl.BlockSpec((n, tn), lambda j: (0, j)),
        out_shape=jax.ShapeDtypeStruct((n, IN_WIDTH), F32),
        compiler_params=pltpu.CompilerParams(
            dimension_semantics=("arbitrary",), vmem_limit_bytes=VMEM_LIMIT_BYTES),
        name="sample_proj",
    )(x, w_in_b)


def _sample_state_stages(cd_ref, proj_ref, cst_ref, cos_ref, sin_ref, dmask_ref, qdec_ref, kdec_ref,
                         sret_ref, kc_ref, vc_ref, cw_ref, gn_ref,
                         a_ref, b_ref, m_ref, u_ref, nret_ref,
                         o_scr, p_scr, m_scr):
    ns = sret_ref.shape[0]
    t_len = proj_ref.shape[0] // ns
    rows = [slice(s * t_len, (s + 1) * t_len) for s in range(ns)]
    kcols = [slice(h * RET_DK, (h + 1) * RET_DK) for h in range(RET_HEADS)]
    vcols = [slice(h * RET_DV, (h + 1) * RET_DV) for h in range(RET_HEADS)]
    mcols = [slice(h * MEM_HEAD_DIM, (h + 1) * MEM_HEAD_DIM) for h in range(MEM_HEADS)]
    pcols = [slice(h * MEM_LEN, (h + 1) * MEM_LEN) for h in range(MEM_HEADS)]
    head_rows = [pl.ds(h, MEM_LEN, stride=MEM_HEADS) for h in range(MEM_HEADS)]

    q, kd, masked = [], [], []

    def conv_and_scores():
        cb = proj_ref[:, OFF_CB:OFF_CB + CONV_WIDTH]
        u = proj_ref[:, OFF_CC:OFF_CC + CONV_WIDTH] * proj_ref[:, OFF_CH:OFF_CH + CONV_WIDTH]
        tok = lax.broadcasted_iota(jnp.int32, u.shape, 0) & (t_len - 1)
        st2 = cst_ref[...]
        st1 = pltpu.roll(st2, st2.shape[0] - 1, axis=0)
        u1 = jnp.where(tok == 0, st1, pltpu.roll(u, 1, axis=0))
        u2 = jnp.where(tok < 2, st2, pltpu.roll(u, 2, axis=0))
        conv_y = u2 * cw_ref[0:1, :] + u1 * cw_ref[1:2, :] + u * cw_ref[2:3, :]
        u_ref[...] = u
        a_ref[...] = (cb * conv_y).astype(BF16)

        cos = cos_ref[...]
        sin = sin_ref[...]
        for h in range(RET_HEADS):
            qh = _rotary(proj_ref[:, OFF_RQ + h * RET_DK:OFF_RQ + (h + 1) * RET_DK], cos, sin)
            kh = _rotary(proj_ref[:, OFF_RK + h * RET_DK:OFF_RK + (h + 1) * RET_DK], cos, sin) * (RET_DK ** -0.5)
            q.append(qh)
            kd.append(kh * kdec_ref[h])
            masked.append((_dot_nt(qh.astype(BF16), kh.astype(BF16)) * dmask_ref[h]).astype(BF16))

    def intra_chunk():
        for h in range(RET_HEADS):
            vb = proj_ref[:, OFF_RV + h * RET_DV:OFF_RV + (h + 1) * RET_DV].astype(BF16)
            o_scr[:, vcols[h]] = _dot(masked[h], vb)

    def cross_chunk_and_state(seqs):
        for s in seqs:
            for h in range(RET_HEADS):
                s_prev = sret_ref[s, h]
                inter = _dot(q[h][rows[s]].astype(BF16), s_prev.astype(BF16))
                o_scr[rows[s], vcols[h]] += inter * qdec_ref[h]
                vb = proj_ref[rows[s], OFF_RV + h * RET_DV:OFF_RV + (h + 1) * RET_DV].astype(BF16)
                nret_ref[s, h] = s_prev * cd_ref[h] + _dot_tn(kd[h][rows[s]].astype(BF16), vb)

    def attention_scores():
        for s in range(ns):
            for h in range(MEM_HEADS):
                mq = proj_ref[rows[s], OFF_MQ + h * MEM_HEAD_DIM:OFF_MQ + (h + 1) * MEM_HEAD_DIM].astype(BF16)
                p_scr[rows[s], pcols[h]] = (_dot_nt(mq, kc_ref[s, head_rows[h], :].astype(BF16))
                                            * (MEM_HEAD_DIM ** -0.5))

    def attention_values():
        for h in range(MEM_HEADS):
            p_scr[:, pcols[h]] = _softmax_rows(p_scr[:, pcols[h]])
        for s in range(ns):
            for h in range(MEM_HEADS):
                p = p_scr[rows[s], pcols[h]].astype(BF16)
                m_scr[rows[s], mcols[h]] = _dot(p, vc_ref[s, head_rows[h], :].astype(BF16))
        m_ref[...] = m_scr[...].astype(BF16)

    def norm_and_gate():
        for h in range(RET_HEADS):
            rg = proj_ref[:, OFF_RG + h * RET_DV:OFF_RG + (h + 1) * RET_DV]
            gated = jax.nn.silu(rg) * (_group_norm(o_scr[:, vcols[h]]) * gn_ref[:, vcols[h]])
            b_ref[:, vcols[h]] = gated.astype(BF16)

    first, second = list(range(ns // 2)), list(range(ns // 2, ns))
    return [conv_and_scores, intra_chunk,
            functools.partial(cross_chunk_and_state, first), functools.partial(cross_chunk_and_state, second),
            attention_scores, attention_values, norm_and_gate]


def _sample_post_kernel(x_ref, a_ref, b_ref, m_ref, g0_ref, g1_ref, g2_ref,
                        bg_ref, wco_ref, wro_ref, wmo_ref, wo_ref, l1g_ref, l1b_ref,
                        wup_ref, wdn_ref, l2g_ref, l2b_ref, y_ref, x1_ref):
    gate_refs = (g0_ref, g1_ref, g2_ref)
    _merge_out_ln(x_ref, (a_ref, b_ref, m_ref), lambda i: gate_refs[i][...], bg_ref,
                  (wco_ref, wro_ref, wmo_ref), wo_ref, (l1g_ref, l1b_ref), x1_ref)
    y_ref[...] = _ffn_ln(x1_ref[...], wup_ref, wdn_ref, l2g_ref, l2b_ref)


def _sample_post(layer, x, a_pre, b_pre, m_pre, proj, w):
    n = x.shape[0]
    tm = SAMPLE_POST_TILE
    tok = lambda width: pl.BlockSpec((tm, width), lambda i: (i, 0))
    gate = lambda k: pl.BlockSpec((tm, D_MODEL), lambda i: (i, OFF_GL // D_MODEL + k))
    return pl.pallas_call(
        _sample_post_kernel,
        grid=(n // tm,),
        in_specs=[
            tok(D_MODEL), tok(CONV_WIDTH), tok(RET_V_WIDTH), tok(MEM_WIDTH), gate(0), gate(1), gate(2),
            _layer_const((1, N_BRANCH * D_MODEL), layer),
            _const((CONV_WIDTH, D_MODEL)),
            _const((RET_V_WIDTH, D_MODEL)),
            _const((MEM_WIDTH, D_MODEL)),
            _const((D_MODEL, D_MODEL)),
            _layer_const((1, D_MODEL), layer),
            _layer_const((1, D_MODEL), layer),
            _const((D_MODEL, D_FF)),
            _const((D_FF, D_MODEL)),
            _layer_const((1, D_MODEL), layer),
            _layer_const((1, D_MODEL), layer),
        ],
        out_specs=tok(D_MODEL),
        out_shape=jax.ShapeDtypeStruct((n, D_MODEL), F32),
        scratch_shapes=[pltpu.VMEM((tm, D_MODEL), F32)],
        compiler_params=pltpu.CompilerParams(
            dimension_semantics=("arbitrary",), vmem_limit_bytes=VMEM_LIMIT_BYTES),
        name="sample_post",
    )(x, a_pre, b_pre, m_pre, proj, proj, proj,
      w["b_gate"], w["w_conv_out"], w["w_ret_out"], w["w_mem_out"], w["w_out"], w["ln1_g"], w["ln1_b"],
      w["w_up"], w["w_down"], w["ln2_g"], w["ln2_b"])


def _rotary_tables(pos, reps):
    half = RET_DK // 2
    inv = ROPE_BASE ** (-jnp.arange(half, dtype=F32) / half)
    ang = pos.astype(F32)[:, None] * inv[None, :]
    cos = jnp.cos(ang)
    sin = jnp.sin(ang)
    cos_full = jnp.concatenate([cos, cos], axis=-1)
    sin_signed = jnp.concatenate([-sin, sin], axis=-1)
    return jnp.tile(cos_full, (reps, 1)), jnp.tile(sin_signed, (reps, 1))


def _decay_tables(c):
    h = RET_HEADS
    log_g = jnp.log1p(-jnp.exp2(-5.0 - jnp.arange(h, dtype=F32)))
    idx = jnp.arange(c, dtype=F32)
    rel = idx[:, None] - idx[None, :]
    causal = rel >= 0
    dmask = jnp.where(causal[None], jnp.exp(jnp.where(causal, rel, 0.0)[None] * log_g[:, None, None]), 0.0)
    q_decay = jnp.exp((idx + 1.0)[None, :] * log_g[:, None])
    k_decay = jnp.exp((c - 1.0 - idx)[None, :] * log_g[:, None])
    chunk_decay = jnp.exp(c * log_g)
    return {
        "dmask": dmask,
        "qdec": jnp.broadcast_to(q_decay[:, :, None], (h, c, RET_DV)),
        "kdec": jnp.broadcast_to(k_decay[:, :, None], (h, c, RET_DK)),
        "cd": chunk_decay,
    }


def kernel(x_prompt, x_sample, cache_mem_k, cache_mem_v, state_conv, state_ret, mem_prompt,
           w_in, b_gate, conv_w, ret_gn_g, w_conv_out, w_ret_out, w_mem_out, w_out, w_mem_kv,
           ln1_g, ln1_b, w_up, w_down, ln2_g, ln2_b):
    bp, seq, _ = x_prompt.shape
    nseq, t_len, _ = x_sample.shape
    assert seq % PROMPT_TILE == 0 and PROMPT_TILE % MIXER_CHUNK == 0
    assert t_len == 8 and nseq % SAMPLE_SEQS == 0 and t_len % RET_CHUNK != 0

    mixer_w = {"w_in": w_in, "w_conv_out": w_conv_out, "w_ret_out": w_ret_out, "w_mem_out": w_mem_out, "w_out": w_out}
    mlp_w = {"w_up": w_up, "w_down": w_down}
    small = {name: a.reshape(DEPTH, 1, a.shape[-1]) for name, a in
             {"b_gate": b_gate, "ret_gn_g": ret_gn_g, "ln1_g": ln1_g, "ln1_b": ln1_b,
              "ln2_g": ln2_g, "ln2_b": ln2_b}.items()}

    tabs_p = _decay_tables(MIXER_CHUNK)
    tabs_p["cos"], tabs_p["sin"] = _rotary_tables(jnp.arange(seq, dtype=jnp.int32), 1)
    tabs_s = _decay_tables(t_len)
    eye = jnp.eye(SAMPLE_SEQS, dtype=F32)
    tabs_s["dmask"] = (eye[None, :, None, :, None] * tabs_s["dmask"][:, None, :, None, :]).reshape(
        RET_HEADS, SAMPLE_SEQS * t_len, SAMPLE_SEQS * t_len)
    tabs_s["kdec"] = jnp.tile(tabs_s["kdec"], (1, SAMPLE_SEQS, 1))
    tabs_s["cos"], tabs_s["sin"] = _rotary_tables(PAST_LEN + jnp.arange(t_len, dtype=jnp.int32), SAMPLE_SEQS)

    mk_f, mv_f, mk_b, mv_b, *cast = _mem_kv(mem_prompt, w_mem_kv, list(mixer_w.values()))
    mixer_b = dict(zip(mixer_w, cast))
    mk_b = mk_b.reshape(DEPTH, bp, MEM_LEN, MEM_WIDTH)
    mv_b = mv_b.reshape(DEPTH, bp, MEM_LEN, MEM_WIDTH)

    kc = cache_mem_k.reshape(DEPTH, nseq, MEM_LEN * MEM_HEADS, MEM_HEAD_DIM)
    vc = cache_mem_v.reshape(DEPTH, nseq, MEM_LEN * MEM_HEADS, MEM_HEAD_DIM)
    cst = jnp.concatenate(
        [state_conv, jnp.zeros((DEPTH, nseq, t_len - (CONV_K - 1), CONV_WIDTH), state_conv.dtype)], axis=2
    ).reshape(DEPTH, nseq * t_len, CONV_WIDTH)

    xp = x_prompt
    xs = x_sample.reshape(nseq * t_len, D_MODEL)
    cp_list, cs_list = [], []
    ret_stack = ret_stack_p = None
    for l in range(DEPTH):
        w = dict(mixer_b, conv_w=conv_w, **small)
        to_cast = [(a, l) for a in mlp_w.values()]
        if l + 1 < DEPTH:
            to_cast += [(a, l + 1) for a in mixer_w.values()]
        zp, ctail_p, ret_stack_p, *cast = _prompt_mixer(l, xp, tabs_p, mk_b, mv_b, w, to_cast, ret_stack_p)
        w.update(zip(mlp_w, cast[:len(mlp_w)]))
        mixer_b = dict(zip(mixer_w, cast[len(mlp_w):]))
        cp_list.append(ctail_p[:, 8 - (CONV_K - 1):, :])

        proj = _sample_proj(xs, w["w_in"])
        xp, a_pre, b_pre, m_pre, u_s, ret_stack = _ffn_and_sample_state(
            l, zp.reshape(bp * seq, D_MODEL), w, proj, cst, tabs_s, state_ret, kc, vc, ret_stack)
        xp = xp.reshape(bp, seq, D_MODEL)
        xs = _sample_post(l, xs, a_pre, b_pre, m_pre, proj, w)
        cs_list.append(u_s.reshape(nseq, t_len, CONV_WIDTH)[:, t_len - (CONV_K - 1):, :])

    mem_shape = (DEPTH, bp, MEM_LEN, MEM_HEADS, MEM_HEAD_DIM)
    return (xp, xs.reshape(nseq, t_len, D_MODEL), mk_f.reshape(mem_shape), mv_f.reshape(mem_shape),
            jnp.stack(cp_list), ret_stack_p, jnp.stack(cs_list), ret_stack)
```

```python
import functools

import jax
import jax.numpy as jnp
from jax import lax
from jax.experimental import pallas as pl
from jax.experimental.pallas import tpu as pltpu

D_MODEL = 1024
DEPTH = 4
CONV_WIDTH = 512
CONV_K = 3
RET_HEADS = 4
RET_DK = 128
RET_DV = 256
RET_QK_WIDTH = RET_HEADS * RET_DK
RET_V_WIDTH = RET_HEADS * RET_DV
RET_CHUNK = 128
MEM_LEN = 256
MEM_HEADS = 4
MEM_HEAD_DIM = 128
MEM_WIDTH = MEM_HEADS * MEM_HEAD_DIM
N_BRANCH = 3
D_FF = 4 * D_MODEL
ROPE_BASE = 10000.0
LN_EPS = 1e-5
GN_EPS = 1e-6
PAST_LEN = 16384
DEEPNORM_ALPHA = (2 * DEPTH) ** 0.25

OFF_CB = 0
OFF_CC = OFF_CB + CONV_WIDTH
OFF_CH = OFF_CC + CONV_WIDTH
OFF_RQ = OFF_CH + CONV_WIDTH
OFF_RK = OFF_RQ + RET_QK_WIDTH
OFF_RV = OFF_RK + RET_QK_WIDTH
OFF_RG = OFF_RV + RET_V_WIDTH
OFF_MQ = OFF_RG + RET_V_WIDTH
OFF_GL = OFF_MQ + MEM_WIDTH
IN_WIDTH = OFF_GL + N_BRANCH * D_MODEL

V7X_VMEM_BYTES = 64 * 1024 * 1024
VMEM_LIMIT_BYTES = V7X_VMEM_BYTES - 6 * 1024 * 1024

PROMPT_TILE = 512
MIXER_CHUNK = 256
FFN_TILE = 512
FFN_CHUNK = 1024
SAMPLE_SEQS = 4
SAMPLE_POST_TILE = 512
SAMPLE_PROJ_COLS = 2048

BF16 = jnp.bfloat16
F32 = jnp.float32


def _dot(a, b):
    return jnp.dot(a, b, preferred_element_type=F32)


def _dot_nt(a, b):
    return lax.dot_general(a, b, (((1,), (1,)), ((), ())), preferred_element_type=F32)


def _dot_tn(a, b):
    return lax.dot_general(a, b, (((0,), (0,)), ((), ())), preferred_element_type=F32)


def _layer_norm(z, g, b):
    mu = jnp.mean(z, axis=-1, keepdims=True)
    zc = z - mu
    var = jnp.mean(zc * zc, axis=-1, keepdims=True)
    return zc * lax.rsqrt(var + LN_EPS) * g + b


def _group_norm(o):
    mu = jnp.mean(o, axis=-1, keepdims=True)
    oc = o - mu
    var = jnp.mean(oc * oc, axis=-1, keepdims=True)
    return oc * lax.rsqrt(var + GN_EPS)


def _rotary(xh, cos, sin_signed):
    return xh * cos + pltpu.roll(xh, RET_DK // 2, axis=1) * sin_signed


def _softmax_rows(s):
    m = jnp.max(s, axis=-1, keepdims=True)
    e = jnp.exp(s - m)
    return e / jnp.sum(e, axis=-1, keepdims=True)


def _merge_out_ln(x_ref, pre_refs, gate_logits, bg_ref, w_refs, wo_ref, ln_refs, y_ref):
    merged = None
    for i, (pre_ref, w_ref) in enumerate(zip(pre_refs, w_refs)):
        gate = jax.nn.sigmoid(gate_logits(i) + bg_ref[:, i * D_MODEL:(i + 1) * D_MODEL])
        term = gate * _dot(pre_ref[...], w_ref[...])
        merged = term if merged is None else merged + term
    merged = merged.astype(BF16)
    half = merged.shape[0] // 2
    for rows in (slice(0, half), slice(half, 2 * half)):
        z = DEEPNORM_ALPHA * x_ref[rows, :] + _dot(merged[rows], wo_ref[...])
        y_ref[rows, :] = z if ln_refs is None else _layer_norm(z, ln_refs[0][...], ln_refs[1][...])


def _ffn_ln(x, wup_ref, wdn_ref, g_ref, b_ref):
    xb = x.astype(BF16)
    acc = None
    for c in range(D_FF // FFN_CHUNK):
        cols = slice(c * FFN_CHUNK, (c + 1) * FFN_CHUNK)
        h = jnp.maximum(_dot(xb, wup_ref[:, cols]), 0.0)
        part = _dot((h * h).astype(BF16), wdn_ref[cols, :])
        acc = part if acc is None else acc + part
    return _layer_norm(DEEPNORM_ALPHA * x + acc, g_ref[...], b_ref[...])


MEM_KV_ROWS = 512
MEM_KV_OUTPUTS = 4


def _mem_kv_kernel(n_cast, mem_ref, w_ref, *refs):
    cast_src = refs[:n_cast]
    k_ref, v_ref, kb_ref, vb_ref = refs[n_cast:n_cast + MEM_KV_OUTPUTS]
    cast_dst = refs[n_cast + MEM_KV_OUTPUTS:n_cast + MEM_KV_OUTPUTS + n_cast]
    wb_ref = refs[-1]
    j = pl.program_id(1)

    for src_ref, dst_ref in zip(cast_src, cast_dst):
        dst_ref[...] = src_ref[...].astype(BF16)

    @pl.when(j == 0)
    def _():
        wb_ref[...] = w_ref[...].astype(BF16)

    rows = pl.ds(pl.multiple_of(j * MEM_KV_ROWS, MEM_KV_ROWS), MEM_KV_ROWS)
    kv = _dot(mem_ref[rows, :].astype(BF16), wb_ref[...])
    kb_ref[...] = kv[:, :MEM_WIDTH].astype(BF16)
    vb_ref[...] = kv[:, MEM_WIDTH:].astype(BF16)
    for h in range(MEM_HEADS):
        head_rows = pl.ds(h, MEM_KV_ROWS, stride=MEM_HEADS)
        k_ref[head_rows, :] = kv[:, h * MEM_HEAD_DIM:(h + 1) * MEM_HEAD_DIM]
        v_ref[head_rows, :] = kv[:, MEM_WIDTH + h * MEM_HEAD_DIM:MEM_WIDTH + (h + 1) * MEM_HEAD_DIM]


def _mem_kv(mem_prompt, w_mem_kv, to_cast):
    bp = mem_prompt.shape[0]
    n = bp * MEM_LEN
    nsub = n // MEM_KV_ROWS
    steps = DEPTH * nsub
    assert nsub * MEM_KV_ROWS == n
    cast_in_specs, cast_out_specs, cast_out_shapes = [], [], []
    for a in to_cast:
        _, rows, cols = a.shape
        slab = rows // steps
        assert slab * steps == rows and slab % 16 == 0
        cast_in_specs.append(pl.BlockSpec((None, slab, cols), lambda l, j: (0, l * nsub + j, 0)))
        cast_out_specs.append(pl.BlockSpec((slab, cols), lambda l, j: (l * nsub + j, 0)))
        cast_out_shapes.append(jax.ShapeDtypeStruct((rows, cols), BF16))
    out_f = jax.ShapeDtypeStruct((DEPTH, n * MEM_HEADS, MEM_HEAD_DIM), F32)
    out_b = jax.ShapeDtypeStruct((DEPTH, n, MEM_WIDTH), BF16)
    f_spec = pl.BlockSpec((None, MEM_KV_ROWS * MEM_HEADS, MEM_HEAD_DIM), lambda l, j: (l, j, 0))
    b_spec = pl.BlockSpec((None, MEM_KV_ROWS, MEM_WIDTH), lambda l, j: (l, j, 0))
    return pl.pallas_call(
        functools.partial(_mem_kv_kernel, len(to_cast)),
        grid=(DEPTH, nsub),
        in_specs=[
            _const((n, D_MODEL)),
            pl.BlockSpec((None, D_MODEL, 2 * MEM_WIDTH), lambda l, j: (l, 0, 0)),
        ] + cast_in_specs,
        out_specs=[f_spec, f_spec, b_spec, b_spec] + cast_out_specs,
        out_shape=[out_f, out_f, out_b, out_b] + cast_out_shapes,
        scratch_shapes=[pltpu.VMEM((D_MODEL, 2 * MEM_WIDTH), BF16)],
        compiler_params=pltpu.CompilerParams(
            dimension_semantics=("arbitrary", "arbitrary"), vmem_limit_bytes=VMEM_LIMIT_BYTES),
        name="mem_kv",
    )(mem_prompt.reshape(n, D_MODEL), w_mem_kv, *to_cast)


MIXER_INPUTS = 17
MIXER_OUTPUTS = 3


def _prompt_mixer_kernel(aliased, n_cast, *refs):
    if aliased:
        refs = refs[1:]
    (cd_ref, x_ref, cos_ref, sin_ref, dmask_ref, qdec_ref, kdec_ref, mk_ref, mv_ref,
     win_ref, bg_ref, cw_ref, gn_ref, wco_ref, wro_ref, wmo_ref, wo_ref) = refs[:MIXER_INPUTS]
    cast_src = refs[MIXER_INPUTS:MIXER_INPUTS + n_cast]
    outs = refs[MIXER_INPUTS + n_cast:]
    y_ref, ctail_ref, ret_ref = outs[:MIXER_OUTPUTS]
    cast_dst = outs[MIXER_OUTPUTS:MIXER_OUTPUTS + n_cast]
    (xb_ref, q_ref, k_ref, kd_ref, vb_ref, msk_ref, kv_ref,
     abuf_ref, bbuf_ref, mbuf_ref) = outs[MIXER_OUTPUTS + n_cast:]

    for src_ref, dst_ref in zip(cast_src, cast_dst):
        dst_ref[...] = src_ref[...].astype(BF16)

    t = pl.program_id(1)
    tq = x_ref.shape[0]
    chunk = dmask_ref.shape[1]
    chunks = [slice(j * chunk, (j + 1) * chunk) for j in range(tq // chunk)]
    scols = [slice(h * chunk, (h + 1) * chunk) for h in range(RET_HEADS)]
    kcols = [slice(h * RET_DK, (h + 1) * RET_DK) for h in range(RET_HEADS)]
    vcols = [slice(h * RET_DV, (h + 1) * RET_DV) for h in range(RET_HEADS)]
    mcols = [slice(h * MEM_HEAD_DIM, (h + 1) * MEM_HEAD_DIM) for h in range(MEM_HEADS)]

    @pl.when(t == 0)
    def _():
        ret_ref[...] = jnp.zeros_like(ret_ref)
        ctail_ref[...] = jnp.zeros_like(ctail_ref)

    xb_ref[...] = x_ref[...].astype(BF16)

    def proj(lo, width):
        return _dot(xb_ref[...], win_ref[:, lo:lo + width])

    pq = proj(OFF_RQ, RET_QK_WIDTH)
    pk = proj(OFF_RK, RET_QK_WIDTH)
    vb_ref[...] = proj(OFF_RV, RET_V_WIDTH).astype(BF16)
    cos = cos_ref[...]
    sin = sin_ref[...]
    for h in range(RET_HEADS):
        q_ref[:, kcols[h]] = _rotary(pq[:, kcols[h]], cos, sin).astype(BF16)
        kh = _rotary(pk[:, kcols[h]], cos, sin) * (RET_DK ** -0.5)
        k_ref[:, kcols[h]] = kh.astype(BF16)
        for r in chunks:
            kd_ref[r, kcols[h]] = (kh[r] * kdec_ref[h]).astype(BF16)

    pm = proj(OFF_MQ, MEM_WIDTH).astype(BF16)
    mem_s = [_dot_nt(pm[:, mcols[h]], mk_ref[:, mcols[h]]) * (MEM_HEAD_DIM ** -0.5) for h in range(MEM_HEADS)]

    cb = proj(OFF_CB, CONV_WIDTH)
    u = proj(OFF_CC, CONV_WIDTH) * proj(OFF_CH, CONV_WIDTH)
    row = lax.broadcasted_iota(jnp.int32, u.shape, 0)
    prev1 = ctail_ref[7:8, :]
    prev2 = ctail_ref[6:7, :]
    u1 = jnp.where(row == 0, prev1, pltpu.roll(u, 1, axis=0))
    u2 = jnp.where(row == 0, prev2, jnp.where(row == 1, prev1, pltpu.roll(u, 2, axis=0)))
    conv_y = u2 * cw_ref[0:1, :] + u1 * cw_ref[1:2, :] + u * cw_ref[2:3, :]
    ctail_ref[...] = u[tq - 8:, :]
    abuf_ref[...] = (cb * conv_y).astype(BF16)

    for h in range(MEM_HEADS):
        p = _softmax_rows(mem_s[h])
        mbuf_ref[:, mcols[h]] = _dot(p.astype(BF16), mv_ref[:, mcols[h]]).astype(BF16)

    for h in range(RET_HEADS):
        for r in chunks:
            msk_ref[r, scols[h]] = (_dot_nt(q_ref[r, kcols[h]], k_ref[r, kcols[h]]) * dmask_ref[h]).astype(BF16)

    for h in range(RET_HEADS):
        for j, r in enumerate(chunks):
            kv_ref[h, j] = _dot_tn(kd_ref[r, kcols[h]], vb_ref[r, vcols[h]])

    for h in range(RET_HEADS):
        pg = proj(OFF_RG + h * RET_DV, RET_DV)
        s_cur = ret_ref[h]
        for j, r in enumerate(chunks):
            intra = _dot(msk_ref[r, scols[h]], vb_ref[r, vcols[h]])
            o = intra + _dot(q_ref[r, kcols[h]], s_cur.astype(BF16)) * qdec_ref[h]
            gated = jax.nn.silu(pg[r]) * (_group_norm(o) * gn_ref[:, vcols[h]])
            bbuf_ref[r, vcols[h]] = gated.astype(BF16)
            s_cur = s_cur * cd_ref[h] + kv_ref[h, j]
        ret_ref[h] = s_cur

    _merge_out_ln(x_ref, (abuf_ref, bbuf_ref, mbuf_ref), lambda i: proj(OFF_GL + i * D_MODEL, D_MODEL), bg_ref,
                  (wco_ref, wro_ref, wmo_ref), wo_ref, None, y_ref)


def _const(shape):
    nd = len(shape)
    return pl.BlockSpec(tuple(shape), lambda *_: (0,) * nd, pipeline_mode=pl.Buffered(1))


def _layer_const(shape, layer):
    nd = len(shape)
    return pl.BlockSpec((None,) + tuple(shape), lambda *_: (layer,) + (0,) * nd, pipeline_mode=pl.Buffered(1))


def _prompt_mixer(layer, x, tabs, mk_b, mv_b, w, to_cast, ret_stack):
    bp, seq, _ = x.shape
    tq = PROMPT_TILE
    grid = (bp, seq // tq)
    steps = bp * (seq // tq)
    steps_per_seq = seq // tq
    cast_in_specs, cast_out_specs, cast_out_shapes = [], [], []
    for a, src_layer in to_cast:
        _, rows, cols = a.shape
        slab = rows // steps
        assert slab * steps == rows and slab % 16 == 0
        cast_in_specs.append(pl.BlockSpec(
            (None, slab, cols), lambda b, t, src_layer=src_layer: (src_layer, b * steps_per_seq + t, 0)))
        cast_out_specs.append(pl.BlockSpec((slab, cols), lambda b, t: (b * steps_per_seq + t, 0)))
        cast_out_shapes.append(jax.ShapeDtypeStruct((rows, cols), BF16))
    in_specs = [
        pl.BlockSpec(memory_space=pltpu.SMEM),
        pl.BlockSpec((None, tq, D_MODEL), lambda b, t: (b, t, 0)),
        pl.BlockSpec((tq, RET_DK), lambda b, t: (t, 0)),
        pl.BlockSpec((tq, RET_DK), lambda b, t: (t, 0)),
        _const((RET_HEADS, MIXER_CHUNK, MIXER_CHUNK)),
        _const((RET_HEADS, MIXER_CHUNK, RET_DV)),
        _const((RET_HEADS, MIXER_CHUNK, RET_DK)),
        pl.BlockSpec((None, None, MEM_LEN, MEM_WIDTH), lambda b, t: (layer, b, 0, 0)),
        pl.BlockSpec((None, None, MEM_LEN, MEM_WIDTH), lambda b, t: (layer, b, 0, 0)),
        _const((D_MODEL, IN_WIDTH)),
        _layer_const((1, N_BRANCH * D_MODEL), layer),
        _layer_const((CONV_K, CONV_WIDTH), layer),
        _layer_const((1, RET_V_WIDTH), layer),
        _const((CONV_WIDTH, D_MODEL)),
        _const((RET_V_WIDTH, D_MODEL)),
        _const((MEM_WIDTH, D_MODEL)),
        _const((D_MODEL, D_MODEL)),
    ]
    out_specs = [
        pl.BlockSpec((None, tq, D_MODEL), lambda b, t: (b, t, 0)),
        pl.BlockSpec((None, 8, CONV_WIDTH), lambda b, t: (b, 0, 0)),
        pl.BlockSpec((None, None, RET_HEADS, RET_DK, RET_DV), lambda b, t: (layer, b, 0, 0, 0)),
    ]
    out_shape = [
        jax.ShapeDtypeStruct((bp, seq, D_MODEL), F32),
        jax.ShapeDtypeStruct((bp, 8, CONV_WIDTH), F32),
        jax.ShapeDtypeStruct((DEPTH, bp, RET_HEADS, RET_DK, RET_DV), F32),
    ]
    assert len(in_specs) == MIXER_INPUTS and len(out_specs) == MIXER_OUTPUTS
    args = (tabs["cd"], x, tabs["cos"], tabs["sin"], tabs["dmask"], tabs["qdec"], tabs["kdec"], mk_b, mv_b,
            w["w_in"], w["b_gate"], w["conv_w"], w["ret_gn_g"], w["w_conv_out"], w["w_ret_out"], w["w_mem_out"],
            w["w_out"], *[a for a, _ in to_cast])
    in_specs = in_specs + cast_in_specs
    aliases = {}
    if ret_stack is not None:
        aliases = {0: MIXER_OUTPUTS - 1}
        in_specs = [pl.BlockSpec(memory_space=pl.ANY)] + in_specs
        args = (ret_stack,) + args
    return pl.pallas_call(
        functools.partial(_prompt_mixer_kernel, ret_stack is not None, len(to_cast)),
        grid=grid,
        in_specs=in_specs,
        out_specs=out_specs + cast_out_specs,
        out_shape=out_shape + cast_out_shapes,
        scratch_shapes=[
            pltpu.VMEM((tq, D_MODEL), BF16),
            pltpu.VMEM((tq, RET_QK_WIDTH), BF16),
            pltpu.VMEM((tq, RET_QK_WIDTH), BF16),
            pltpu.VMEM((tq, RET_QK_WIDTH), BF16),
            pltpu.VMEM((tq, RET_V_WIDTH), BF16),
            pltpu.VMEM((tq, RET_HEADS * MIXER_CHUNK), BF16),
            pltpu.VMEM((RET_HEADS, tq // MIXER_CHUNK, RET_DK, RET_DV), F32),
            pltpu.VMEM((tq, CONV_WIDTH), BF16),
            pltpu.VMEM((tq, RET_V_WIDTH), BF16),
            pltpu.VMEM((tq, MEM_WIDTH), BF16),
        ],
        input_output_aliases=aliases,
        compiler_params=pltpu.CompilerParams(
            dimension_semantics=("arbitrary", "arbitrary"), vmem_limit_bytes=VMEM_LIMIT_BYTES),
        name="prompt_mixer",
    )(*args)


FFN_INPUTS = 8
STATE_INPUTS = 13
STATE_OUTPUTS = 5


def _ffn_state_kernel(aliased, *refs):
    if aliased:
        refs = refs[1:]
    z_first_ref, z_next_ref, l1g_ref, l1b_ref, wup_ref, wdn_ref, g_ref, b_ref = refs[:FFN_INPUTS]
    state_in = refs[FFN_INPUTS:FFN_INPUTS + STATE_INPUTS]
    y_ref = refs[FFN_INPUTS + STATE_INPUTS]
    rest = refs[FFN_INPUTS + STATE_INPUTS + 1:]
    x1_scr, next_scr = rest[-2:]
    stages = _sample_state_stages(*state_in, *rest[:-2])
    n_chunks = D_FF // FFN_CHUNK
    assert len(stages) == 2 * n_chunks - 1

    @pl.when(pl.program_id(0) == 0)
    def _():
        x1_scr[...] = _layer_norm(z_first_ref[...], l1g_ref[...], l1b_ref[...])

    next_scr[...] = _layer_norm(z_next_ref[...], l1g_ref[...], l1b_ref[...])
    xb = x1_scr[...].astype(BF16)
    acc = None
    for c in range(n_chunks):
        cols = slice(c * FFN_CHUNK, (c + 1) * FFN_CHUNK)
        h = jnp.maximum(_dot(xb, wup_ref[:, cols]), 0.0)
        stages[2 * c]()
        part = _dot((h * h).astype(BF16), wdn_ref[cols, :])
        if c + 1 < n_chunks:
            stages[2 * c + 1]()
        acc = part if acc is None else acc + part
    y_ref[...] = _layer_norm(DEEPNORM_ALPHA * x1_scr[...] + acc, g_ref[...], b_ref[...])
    x1_scr[...] = next_scr[...]


def _ffn_and_sample_state(layer, z, w, proj, cst, tabs, state_ret, kc, vc, ret_stack):
    n = z.shape[0]
    tm = FFN_TILE
    ns = SAMPLE_SEQS
    n_tok = proj.shape[0]
    nseq = state_ret.shape[1]
    t_len = n_tok // nseq
    rows = ns * t_len
    n_tiles = n // tm
    assert n_tiles == nseq // ns
    tok_spec = lambda width: pl.BlockSpec((rows, width), lambda i: (i, 0))
    in_specs = [
        pl.BlockSpec((tm, D_MODEL), lambda i: (0, 0)),
        pl.BlockSpec((tm, D_MODEL), lambda i: (jnp.minimum(i + 1, n_tiles - 1), 0)),
        _layer_const((1, D_MODEL), layer),
        _layer_const((1, D_MODEL), layer),
        _const((D_MODEL, D_FF)),
        _const((D_FF, D_MODEL)),
        _layer_const((1, D_MODEL), layer),
        _layer_const((1, D_MODEL), layer),
        pl.BlockSpec(memory_space=pltpu.SMEM),
        tok_spec(OFF_GL),
        pl.BlockSpec((None, rows, CONV_WIDTH), lambda i: (layer, i, 0)),
        _const((rows, RET_DK)),
        _const((rows, RET_DK)),
        _const((RET_HEADS, rows, rows)),
        _const((RET_HEADS, t_len, RET_DV)),
        _const((RET_HEADS, rows, RET_DK)),
        pl.BlockSpec((None, ns, RET_HEADS, RET_DK, RET_DV), lambda i: (layer, i, 0, 0, 0)),
        pl.BlockSpec((None, ns, MEM_LEN * MEM_HEADS, MEM_HEAD_DIM), lambda i: (layer, i, 0, 0)),
        pl.BlockSpec((None, ns, MEM_LEN * MEM_HEADS, MEM_HEAD_DIM), lambda i: (layer, i, 0, 0)),
        _layer_const((CONV_K, CONV_WIDTH), layer),
        _layer_const((1, RET_V_WIDTH), layer),
    ]
    out_specs = [
        pl.BlockSpec((tm, D_MODEL), lambda i: (i, 0)),
        tok_spec(CONV_WIDTH),
        tok_spec(RET_V_WIDTH),
        tok_spec(MEM_WIDTH),
        tok_spec(CONV_WIDTH),
        pl.BlockSpec((None, ns, RET_HEADS, RET_DK, RET_DV), lambda i: (layer, i, 0, 0, 0)),
    ]
    out_shape = [
        jax.ShapeDtypeStruct((n, D_MODEL), F32),
        jax.ShapeDtypeStruct((n_tok, CONV_WIDTH), BF16),
        jax.ShapeDtypeStruct((n_tok, RET_V_WIDTH), BF16),
        jax.ShapeDtypeStruct((n_tok, MEM_WIDTH), BF16),
        jax.ShapeDtypeStruct((n_tok, CONV_WIDTH), F32),
        jax.ShapeDtypeStruct((DEPTH, nseq, RET_HEADS, RET_DK, RET_DV), F32),
    ]
    assert len(in_specs) == FFN_INPUTS + STATE_INPUTS and len(out_specs) == 1 + STATE_OUTPUTS
    args = (z, z, w["ln1_g"], w["ln1_b"], w["w_up"], w["w_down"], w["ln2_g"], w["ln2_b"],
            tabs["cd"], proj, cst, tabs["cos"], tabs["sin"], tabs["dmask"], tabs["qdec"], tabs["kdec"],
            state_ret, kc, vc, w["conv_w"], w["ret_gn_g"])
    aliases = {}
    if ret_stack is not None:
        aliases = {0: len(out_shape) - 1}
        in_specs = [pl.BlockSpec(memory_space=pl.ANY)] + in_specs
        args = (ret_stack,) + args
    scratch = [
        pltpu.VMEM((rows, RET_V_WIDTH), F32),
        pltpu.VMEM((rows, MEM_HEADS * MEM_LEN), F32),
        pltpu.VMEM((rows, MEM_WIDTH), F32),
        pltpu.VMEM((tm, D_MODEL), F32),
        pltpu.VMEM((tm, D_MODEL), F32),
    ]
    return pl.pallas_call(
        functools.partial(_ffn_state_kernel, ret_stack is not None),
        grid=(n // tm,),
        in_specs=in_specs,
        out_specs=out_specs,
        out_shape=out_shape,
        scratch_shapes=scratch,
        input_output_aliases=aliases,
        compiler_params=pltpu.CompilerParams(
            dimension_semantics=("arbitrary",), vmem_limit_bytes=VMEM_LIMIT_BYTES),
        name="ffn_state",
    )(*args)


def _sample_proj_kernel(x_ref, w_ref, o_ref):
    o_ref[...] = _dot(x_ref[...].astype(BF16), w_ref[...])


def _sample_proj(x, w_in_b):
    n = x.shape[0]
    tn = SAMPLE_PROJ_COLS
    return pl.pallas_call(
        _sample_proj_kernel,
        grid=(IN_WIDTH // tn,),
        in_specs=[
            pl.BlockSpec((n, D_MODEL), lambda j: (0, 0)),
            pl.BlockSpec((D_MODEL, tn), lambda j: (0, j)),
        ],
        out_specs=pl.BlockSpec((n, tn), lambda j: (0, j)),
        out_shape=jax.ShapeDtypeStruct((n, IN_WIDTH), F32),
        compiler_params=pltpu.CompilerParams(
            dimension_semantics=("arbitrary",), vmem_limit_bytes=VMEM_LIMIT_BYTES),
        name="sample_proj",
    )(x, w_in_b)


def _sample_state_stages(cd_ref, proj_ref, cst_ref, cos_ref, sin_ref, dmask_ref, qdec_ref, kdec_ref,
                         sret_ref, kc_ref, vc_ref, cw_ref, gn_ref,
                         a_ref, b_ref, m_ref, u_ref, nret_ref,
                         o_scr, p_scr, m_scr):
    ns = sret_ref.shape[0]
    t_len = proj_ref.shape[0] // ns
    rows = [slice(s * t_len, (s + 1) * t_len) for s in range(ns)]
    kcols = [slice(h * RET_DK, (h + 1) * RET_DK) for h in range(RET_HEADS)]
    vcols = [slice(h * RET_DV, (h + 1) * RET_DV) for h in range(RET_HEADS)]
    mcols = [slice(h * MEM_HEAD_DIM, (h + 1) * MEM_HEAD_DIM) for h in range(MEM_HEADS)]
    pcols = [slice(h * MEM_LEN, (h + 1) * MEM_LEN) for h in range(MEM_HEADS)]
    head_rows = [pl.ds(h, MEM_LEN, stride=MEM_HEADS) for h in range(MEM_HEADS)]

    q, kd, masked = [], [], []

    def conv_and_scores():
        cb = proj_ref[:, OFF_CB:OFF_CB + CONV_WIDTH]
        u = proj_ref[:, OFF_CC:OFF_CC + CONV_WIDTH] * proj_ref[:, OFF_CH:OFF_CH + CONV_WIDTH]
        tok = lax.broadcasted_iota(jnp.int32, u.shape, 0) & (t_len - 1)
        st2 = cst_ref[...]
        st1 = pltpu.roll(st2, st2.shape[0] - 1, axis=0)
        u1 = jnp.where(tok == 0, st1, pltpu.roll(u, 1, axis=0))
        u2 = jnp.where(tok < 2, st2, pltpu.roll(u, 2, axis=0))
        conv_y = u2 * cw_ref[0:1, :] + u1 * cw_ref[1:2, :] + u * cw_ref[2:3, :]
        u_ref[...] = u
        a_ref[...] = (cb * conv_y).astype(BF16)

        cos = cos_ref[...]
        sin = sin_ref[...]
        for h in range(RET_HEADS):
            qh = _rotary(proj_ref[:, OFF_RQ + h * RET_DK:OFF_RQ + (h + 1) * RET_DK], cos, sin)
            kh = _rotary(proj_ref[:, OFF_RK + h * RET_DK:OFF_RK + (h + 1) * RET_DK], cos, sin) * (RET_DK ** -0.5)
            q.append(qh)
            kd.append(kh * kdec_ref[h])
            masked.append((_dot_nt(qh.astype(BF16), kh.astype(BF16)) * dmask_ref[h]).astype(BF16))

    def intra_chunk():
        for h in range(RET_HEADS):
            vb = proj_ref[:, OFF_RV + h * RET_DV:OFF_RV + (h + 1) * RET_DV].astype(BF16)
            o_scr[:, vcols[h]] = _dot(masked[h], vb)

    def cross_chunk_and_state(seqs):
        for s in seqs:
            for h in range(RET_HEADS):
                s_prev = sret_ref[s, h]
                inter = _dot(q[h][rows[s]].astype(BF16), s_prev.astype(BF16))
                o_scr[rows[s], vcols[h]] += inter * qdec_ref[h]
                vb = proj_ref[rows[s], OFF_RV + h * RET_DV:OFF_RV + (h + 1) * RET_DV].astype(BF16)
                nret_ref[s, h] = s_prev * cd_ref[h] + _dot_tn(kd[h][rows[s]].astype(BF16), vb)

    def attention_scores():
        for s in range(ns):
            for h in range(MEM_HEADS):
                mq = proj_ref[rows[s], OFF_MQ + h * MEM_HEAD_DIM:OFF_MQ + (h + 1) * MEM_HEAD_DIM].astype(BF16)
                p_scr[rows[s], pcols[h]] = (_dot_nt(mq, kc_ref[s, head_rows[h], :].astype(BF16))
                                            * (MEM_HEAD_DIM ** -0.5))

    def attention_values():
        for h in range(MEM_HEADS):
            p_scr[:, pcols[h]] = _softmax_rows(p_scr[:, pcols[h]])
        for s in range(ns):
            for h in range(MEM_HEADS):
                p = p_scr[rows[s], pcols[h]].astype(BF16)
                m_scr[rows[s], mcols[h]] = _dot(p, vc_ref[s, head_rows[h], :].astype(BF16))
        m_ref[...] = m_scr[...].astype(BF16)

    def norm_and_gate():
        for h in range(RET_HEADS):
            rg = proj_ref[:, OFF_RG + h * RET_DV:OFF_RG + (h + 1) * RET_DV]
            gated = jax.nn.silu(rg) * (_group_norm(o_scr[:, vcols[h]]) * gn_ref[:, vcols[h]])
            b_ref[:, vcols[h]] = gated.astype(BF16)

    first, second = list(range(ns // 2)), list(range(ns // 2, ns))
    return [conv_and_scores, intra_chunk,
            functools.partial(cross_chunk_and_state, first), functools.partial(cross_chunk_and_state, second),
            attention_scores, attention_values, norm_and_gate]


def _sample_post_kernel(x_ref, a_ref, b_ref, m_ref, g0_ref, g1_ref, g2_ref,
                        bg_ref, wco_ref, wro_ref, wmo_ref, wo_ref, l1g_ref, l1b_ref,
                        wup_ref, wdn_ref, l2g_ref, l2b_ref, y_ref, x1_ref):
    gate_refs = (g0_ref, g1_ref, g2_ref)
    _merge_out_ln(x_ref, (a_ref, b_ref, m_ref), lambda i: gate_refs[i][...], bg_ref,
                  (wco_ref, wro_ref, wmo_ref), wo_ref, (l1g_ref, l1b_ref), x1_ref)
    y_ref[...] = _ffn_ln(x1_ref[...], wup_ref, wdn_ref, l2g_ref, l2b_ref)


def _sample_post(layer, x, a_pre, b_pre, m_pre, proj, w):
    n = x.shape[0]
    tm = SAMPLE_POST_TILE
    tok = lambda width: pl.BlockSpec((tm, width), lambda i: (i, 0))
    gate = lambda k: pl.BlockSpec((tm, D_MODEL), lambda i: (i, OFF_GL // D_MODEL + k))
    return pl.pallas_call(
        _sample_post_kernel,
        grid=(n // tm,),
        in_specs=[
            tok(D_MODEL), tok(CONV_WIDTH), tok(RET_V_WIDTH), tok(MEM_WIDTH), gate(0), gate(1), gate(2),
            _layer_const((1, N_BRANCH * D_MODEL), layer),
            _const((CONV_WIDTH, D_MODEL)),
            _const((RET_V_WIDTH, D_MODEL)),
            _const((MEM_WIDTH, D_MODEL)),
            _const((D_MODEL, D_MODEL)),
            _layer_const((1, D_MODEL), layer),
            _layer_const((1, D_MODEL), layer),
            _const((D_MODEL, D_FF)),
            _const((D_FF, D_MODEL)),
            _layer_const((1, D_MODEL), layer),
            _layer_const((1, D_MODEL), layer),
        ],
        out_specs=tok(D_MODEL),
        out_shape=jax.ShapeDtypeStruct((n, D_MODEL), F32),
        scratch_shapes=[pltpu.VMEM((tm, D_MODEL), F32)],
        compiler_params=pltpu.CompilerParams(
            dimension_semantics=("arbitrary",), vmem_limit_bytes=VMEM_LIMIT_BYTES),
        name="sample_post",
    )(x, a_pre, b_pre, m_pre, proj, proj, proj,
      w["b_gate"], w["w_conv_out"], w["w_ret_out"], w["w_mem_out"], w["w_out"], w["ln1_g"], w["ln1_b"],
      w["w_up"], w["w_down"], w["ln2_g"], w["ln2_b"])


def _rotary_tables(pos, reps):
    half = RET_DK // 2
    inv = ROPE_BASE ** (-jnp.arange(half, dtype=F32) / half)
    ang = pos.astype(F32)[:, None] * inv[None, :]
    cos = jnp.cos(ang)
    sin = jnp.sin(ang)
    cos_full = jnp.concatenate([cos, cos], axis=-1)
    sin_signed = jnp.concatenate([-sin, sin], axis=-1)
    return jnp.tile(cos_full, (reps, 1)), jnp.tile(sin_signed, (reps, 1))


def _decay_tables(c):
    h = RET_HEADS
    log_g = jnp.log1p(-jnp.exp2(-5.0 - jnp.arange(h, dtype=F32)))
    idx = jnp.arange(c, dtype=F32)
    rel = idx[:, None] - idx[None, :]
    causal = rel >= 0
    dmask = jnp.where(causal[None], jnp.exp(jnp.where(causal, rel, 0.0)[None] * log_g[:, None, None]), 0.0)
    q_decay = jnp.exp((idx + 1.0)[None, :] * log_g[:, None])
    k_decay = jnp.exp((c - 1.0 - idx)[None, :] * log_g[:, None])
    chunk_decay = jnp.exp(c * log_g)
    return {
        "dmask": dmask,
        "qdec": jnp.broadcast_to(q_decay[:, :, None], (h, c, RET_DV)),
        "kdec": jnp.broadcast_to(k_decay[:, :, None], (h, c, RET_DK)),
        "cd": chunk_decay,
    }


def kernel(x_prompt, x_sample, cache_mem_k, cache_mem_v, state_conv, state_ret, mem_prompt,
           w_in, b_gate, conv_w, ret_gn_g, w_conv_out, w_ret_out, w_mem_out, w_out, w_mem_kv,
           ln1_g, ln1_b, w_up, w_down, ln2_g, ln2_b):
    bp, seq, _ = x_prompt.shape
    nseq, t_len, _ = x_sample.shape
    assert seq % PROMPT_TILE == 0 and PROMPT_TILE % MIXER_CHUNK == 0
    assert t_len == 8 and nseq % SAMPLE_SEQS == 0 and t_len % RET_CHUNK != 0

    mixer_w = {"w_in": w_in, "w_conv_out": w_conv_out, "w_ret_out": w_ret_out, "w_mem_out": w_mem_out, "w_out": w_out}
    mlp_w = {"w_up": w_up, "w_down": w_down}
    small = {name: a.reshape(DEPTH, 1, a.shape[-1]) for name, a in
             {"b_gate": b_gate, "ret_gn_g": ret_gn_g, "ln1_g": ln1_g, "ln1_b": ln1_b,
              "ln2_g": ln2_g, "ln2_b": ln2_b}.items()}

    tabs_p = _decay_tables(MIXER_CHUNK)
    tabs_p["cos"], tabs_p["sin"] = _rotary_tables(jnp.arange(seq, dtype=jnp.int32), 1)
    tabs_s = _decay_tables(t_len)
    eye = jnp.eye(SAMPLE_SEQS, dtype=F32)
    tabs_s["dmask"] = (eye[None, :, None, :, None] * tabs_s["dmask"][:, None, :, None, :]).reshape(
        RET_HEADS, SAMPLE_SEQS * t_len, SAMPLE_SEQS * t_len)
    tabs_s["kdec"] = jnp.tile(tabs_s["kdec"], (1, SAMPLE_SEQS, 1))
    tabs_s["cos"], tabs_s["sin"] = _rotary_tables(PAST_LEN + jnp.arange(t_len, dtype=jnp.int32), SAMPLE_SEQS)

    mk_f, mv_f, mk_b, mv_b, *cast = _mem_kv(mem_prompt, w_mem_kv, list(mixer_w.values()))
    mixer_b = dict(zip(mixer_w, cast))
    mk_b = mk_b.reshape(DEPTH, bp, MEM_LEN, MEM_WIDTH)
    mv_b = mv_b.reshape(DEPTH, bp, MEM_LEN, MEM_WIDTH)

    kc = cache_mem_k.reshape(DEPTH, nseq, MEM_LEN * MEM_HEADS, MEM_HEAD_DIM)
    vc = cache_mem_v.reshape(DEPTH, nseq, MEM_LEN * MEM_HEADS, MEM_HEAD_DIM)
    cst = jnp.concatenate(
        [state_conv, jnp.zeros((DEPTH, nseq, t_len - (CONV_K - 1), CONV_WIDTH), state_conv.dtype)], axis=2
    ).reshape(DEPTH, nseq * t_len, CONV_WIDTH)

    xp = x_prompt
    xs = x_sample.reshape(nseq * t_len, D_MODEL)
    cp_list, cs_list = [], []
    ret_stack = ret_stack_p = None
    for l in range(DEPTH):
        w = dict(mixer_b, conv_w=conv_w, **small)
        to_cast = [(a, l) for a in mlp_w.values()]
        if l + 1 < DEPTH:
            to_cast += [(a, l + 1) for a in mixer_w.values()]
        zp, ctail_p, ret_stack_p, *cast = _prompt_mixer(l, xp, tabs_p, mk_b, mv_b, w, to_cast, ret_stack_p)
        w.update(zip(mlp_w, cast[:len(mlp_w)]))
        mixer_b = dict(zip(mixer_w, cast[len(mlp_w):]))
        cp_list.append(ctail_p[:, 8 - (CONV_K - 1):, :])

        proj = _sample_proj(xs, w["w_in"])
        xp, a_pre, b_pre, m_pre, u_s, ret_stack = _ffn_and_sample_state(
            l, zp.reshape(bp * seq, D_MODEL), w, proj, cst, tabs_s, state_ret, kc, vc, ret_stack)
        xp = xp.reshape(bp, seq, D_MODEL)
        xs = _sample_post(l, xs, a_pre, b_pre, m_pre, proj, w)
        cs_list.append(u_s.reshape(nseq, t_len, CONV_WIDTH)[:, t_len - (CONV_K - 1):, :])

    mem_shape = (DEPTH, bp, MEM_LEN, MEM_HEADS, MEM_HEAD_DIM)
    return (xp, xs.reshape(nseq, t_len, D_MODEL), mk_f.reshape(mem_shape), mv_f.reshape(mem_shape),
            jnp.stack(cp_list), ret_stack_p, jnp.stack(cs_list), ret_stack)
```

```python
import functools

import jax
import jax.numpy as jnp
from jax import lax
from jax.experimental import pallas as pl
from jax.experimental.pallas import tpu as pltpu

D_MODEL = 1024
DEPTH = 4
CONV_WIDTH = 512
CONV_K = 3
RET_HEADS = 4
RET_DK = 128
RET_DV = 256
RET_QK_WIDTH = RET_HEADS * RET_DK
RET_V_WIDTH = RET_HEADS * RET_DV
RET_CHUNK = 128
MEM_LEN = 256
MEM_HEADS = 4
MEM_HEAD_DIM = 128
MEM_WIDTH = MEM_HEADS * MEM_HEAD_DIM
N_BRANCH = 3
D_FF = 4 * D_MODEL
ROPE_BASE = 10000.0
LN_EPS = 1e-5
GN_EPS = 1e-6
PAST_LEN = 16384
DEEPNORM_ALPHA = (2 * DEPTH) ** 0.25

OFF_CB = 0
OFF_CC = OFF_CB + CONV_WIDTH
OFF_CH = OFF_CC + CONV_WIDTH
OFF_RQ = OFF_CH + CONV_WIDTH
OFF_RK = OFF_RQ + RET_QK_WIDTH
OFF_RV = OFF_RK + RET_QK_WIDTH
OFF_RG = OFF_RV + RET_V_WIDTH
OFF_MQ = OFF_RG + RET_V_WIDTH
OFF_GL = OFF_MQ + MEM_WIDTH
IN_WIDTH = OFF_GL + N_BRANCH * D_MODEL

V7X_VMEM_BYTES = 64 * 1024 * 1024
VMEM_LIMIT_BYTES = V7X_VMEM_BYTES - 6 * 1024 * 1024

PROMPT_TILE = 512
MIXER_CHUNK = 256
FFN_TILE = 512
FFN_CHUNK = 1024
SAMPLE_SEQS = 4
SAMPLE_POST_TILE = 512
SAMPLE_PROJ_COLS = 2048

BF16 = jnp.bfloat16
F32 = jnp.float32


def _dot(a, b):
    return jnp.dot(a, b, preferred_element_type=F32)


def _dot_nt(a, b):
    return lax.dot_general(a, b, (((1,), (1,)), ((), ())), preferred_element_type=F32)


def _dot_tn(a, b):
    return lax.dot_general(a, b, (((0,), (0,)), ((), ())), preferred_element_type=F32)


def _layer_norm(z, g, b):
    mu = jnp.mean(z, axis=-1, keepdims=True)
    zc = z - mu
    var = jnp.mean(zc * zc, axis=-1, keepdims=True)
    return zc * lax.rsqrt(var + LN_EPS) * g + b


def _group_norm(o):
    mu = jnp.mean(o, axis=-1, keepdims=True)
    oc = o - mu
    var = jnp.mean(oc * oc, axis=-1, keepdims=True)
    return oc * lax.rsqrt(var + GN_EPS)


def _rotary(xh, cos, sin_signed):
    return xh * cos + pltpu.roll(xh, RET_DK // 2, axis=1) * sin_signed


def _softmax_rows(s):
    m = jnp.max(s, axis=-1, keepdims=True)
    e = jnp.exp(s - m)
    return e / jnp.sum(e, axis=-1, keepdims=True)


def _merge_out_ln(x_ref, pre_refs, gate_logits, bg_ref, w_refs, wo_ref, ln_refs, y_ref):
    merged = None
    for i, (pre_ref, w_ref) in enumerate(zip(pre_refs, w_refs)):
        gate = jax.nn.sigmoid(gate_logits(i) + bg_ref[:, i * D_MODEL:(i + 1) * D_MODEL])
        term = gate * _dot(pre_ref[...], w_ref[...])
        merged = term if merged is None else merged + term
    merged = merged.astype(BF16)
    half = merged.shape[0] // 2
    for rows in (slice(0, half), slice(half, 2 * half)):
        z = DEEPNORM_ALPHA * x_ref[rows, :] + _dot(merged[rows], wo_ref[...])
        y_ref[rows, :] = z if ln_refs is None else _layer_norm(z, ln_refs[0][...], ln_refs[1][...])


def _ffn_ln(x, wup_ref, wdn_ref, g_ref, b_ref):
    xb = x.astype(BF16)
    acc = None
    for c in range(D_FF // FFN_CHUNK):
        cols = slice(c * FFN_CHUNK, (c + 1) * FFN_CHUNK)
        h = jnp.maximum(_dot(xb, wup_ref[:, cols]), 0.0)
        part = _dot((h * h).astype(BF16), wdn_ref[cols, :])
        acc = part if acc is None else acc + part
    return _layer_norm(DEEPNORM_ALPHA * x + acc, g_ref[...], b_ref[...])


MEM_KV_ROWS = 512
MEM_KV_OUTPUTS = 4


def _mem_kv_kernel(n_cast, mem_ref, w_ref, *refs):
    cast_src = refs[:n_cast]
    k_ref, v_ref, kb_ref, vb_ref = refs[n_cast:n_cast + MEM_KV_OUTPUTS]
    cast_dst = refs[n_cast + MEM_KV_OUTPUTS:n_cast + MEM_KV_OUTPUTS + n_cast]
    wb_ref = refs[-1]
    j = pl.program_id(1)

    for src_ref, dst_ref in zip(cast_src, cast_dst):
        dst_ref[...] = src_ref[...].astype(BF16)

    @pl.when(j == 0)
    def _():
        wb_ref[...] = w_ref[...].astype(BF16)

    rows = pl.ds(pl.multiple_of(j * MEM_KV_ROWS, MEM_KV_ROWS), MEM_KV_ROWS)
    kv = _dot(mem_ref[rows, :].astype(BF16), wb_ref[...])
    kb_ref[...] = kv[:, :MEM_WIDTH].astype(BF16)
    vb_ref[...] = kv[:, MEM_WIDTH:].astype(BF16)
    for h in range(MEM_HEADS):
        head_rows = pl.ds(h, MEM_KV_ROWS, stride=MEM_HEADS)
        k_ref[head_rows, :] = kv[:, h * MEM_HEAD_DIM:(h + 1) * MEM_HEAD_DIM]
        v_ref[head_rows, :] = kv[:, MEM_WIDTH + h * MEM_HEAD_DIM:MEM_WIDTH + (h + 1) * MEM_HEAD_DIM]


def _mem_kv(mem_prompt, w_mem_kv, to_cast):
    bp = mem_prompt.shape[0]
    n = bp * MEM_LEN
    nsub = n // MEM_KV_ROWS
    steps = DEPTH * nsub
    assert nsub * MEM_KV_ROWS == n
    cast_in_specs, cast_out_specs, cast_out_shapes = [], [], []
    for a in to_cast:
        _, rows, cols = a.shape
        slab = rows // steps
        assert slab * steps == rows and slab % 16 == 0
        cast_in_specs.append(pl.BlockSpec((None, slab, cols), lambda l, j: (0, l * nsub + j, 0)))
        cast_out_specs.append(pl.BlockSpec((slab, cols), lambda l, j: (l * nsub + j, 0)))
        cast_out_shapes.append(jax.ShapeDtypeStruct((rows, cols), BF16))
    out_f = jax.ShapeDtypeStruct((DEPTH, n * MEM_HEADS, MEM_HEAD_DIM), F32)
    out_b = jax.ShapeDtypeStruct((DEPTH, n, MEM_WIDTH), BF16)
    f_spec = pl.BlockSpec((None, MEM_KV_ROWS * MEM_HEADS, MEM_HEAD_DIM), lambda l, j: (l, j, 0))
    b_spec = pl.BlockSpec((None, MEM_KV_ROWS, MEM_WIDTH), lambda l, j: (l, j, 0))
    return pl.pallas_call(
        functools.partial(_mem_kv_kernel, len(to_cast)),
        grid=(DEPTH, nsub),
        in_specs=[
            _const((n, D_MODEL)),
            pl.BlockSpec((None, D_MODEL, 2 * MEM_WIDTH), lambda l, j: (l, 0, 0)),
        ] + cast_in_specs,
        out_specs=[f_spec, f_spec, b_spec, b_spec] + cast_out_specs,
        out_shape=[out_f, out_f, out_b, out_b] + cast_out_shapes,
        scratch_shapes=[pltpu.VMEM((D_MODEL, 2 * MEM_WIDTH), BF16)],
        compiler_params=pltpu.CompilerParams(
            dimension_semantics=("arbitrary", "arbitrary"), vmem_limit_bytes=VMEM_LIMIT_BYTES),
        name="mem_kv",
    )(mem_prompt.reshape(n, D_MODEL), w_mem_kv, *to_cast)


MIXER_INPUTS = 17
MIXER_OUTPUTS = 3


def _prompt_mixer_kernel(aliased, n_cast, *refs):
    if aliased:
        refs = refs[1:]
    (cd_ref, x_ref, cos_ref, sin_ref, dmask_ref, qdec_ref, kdec_ref, mk_ref, mv_ref,
     win_ref, bg_ref, cw_ref, gn_ref, wco_ref, wro_ref, wmo_ref, wo_ref) = refs[:MIXER_INPUTS]
    cast_src = refs[MIXER_INPUTS:MIXER_INPUTS + n_cast]
    outs = refs[MIXER_INPUTS + n_cast:]
    y_ref, ctail_ref, ret_ref = outs[:MIXER_OUTPUTS]
    cast_dst = outs[MIXER_OUTPUTS:MIXER_OUTPUTS + n_cast]
    (xb_ref, q_ref, k_ref, kd_ref, vb_ref, msk_ref, kv_ref,
     abuf_ref, bbuf_ref, mbuf_ref) = outs[MIXER_OUTPUTS + n_cast:]

    for src_ref, dst_ref in zip(cast_src, cast_dst):
        dst_ref[...] = src_ref[...].astype(BF16)

    t = pl.program_id(1)
    tq = x_ref.shape[0]
    chunk = dmask_ref.shape[1]
    chunks = [slice(j * chunk, (j + 1) * chunk) for j in range(tq // chunk)]
    scols = [slice(h * chunk, (h + 1) * chunk) for h in range(RET_HEADS)]
    kcols = [slice(h * RET_DK, (h + 1) * RET_DK) for h in range(RET_HEADS)]
    vcols = [slice(h * RET_DV, (h + 1) * RET_DV) for h in range(RET_HEADS)]
    mcols = [slice(h * MEM_HEAD_DIM, (h + 1) * MEM_HEAD_DIM) for h in range(MEM_HEADS)]

    @pl.when(t == 0)
    def _():
        ret_ref[...] = jnp.zeros_like(ret_ref)
        ctail_ref[...] = jnp.zeros_like(ctail_ref)

    xb_ref[...] = x_ref[...].astype(BF16)

    def proj(lo, width):
        return _dot(xb_ref[...], win_ref[:, lo:lo + width])

    pq = proj(OFF_RQ, RET_QK_WIDTH)
    pk = proj(OFF_RK, RET_QK_WIDTH)
    vb_ref[...] = proj(OFF_RV, RET_V_WIDTH).astype(BF16)
    cos = cos_ref[...]
    sin = sin_ref[...]
    for h in range(RET_HEADS):
        q_ref[:, kcols[h]] = _rotary(pq[:, kcols[h]], cos, sin).astype(BF16)
        kh = _rotary(pk[:, kcols[h]], cos, sin) * (RET_DK ** -0.5)
        k_ref[:, kcols[h]] = kh.astype(BF16)
        for r in chunks:
            kd_ref[r, kcols[h]] = (kh[r] * kdec_ref[h]).astype(BF16)

    pm = proj(OFF_MQ, MEM_WIDTH).astype(BF16)
    mem_s = [_dot_nt(pm[:, mcols[h]], mk_ref[:, mcols[h]]) * (MEM_HEAD_DIM ** -0.5) for h in range(MEM_HEADS)]

    cb = proj(OFF_CB, CONV_WIDTH)
    u = proj(OFF_CC, CONV_WIDTH) * proj(OFF_CH, CONV_WIDTH)
    row = lax.broadcasted_iota(jnp.int32, u.shape, 0)
    prev1 = ctail_ref[7:8, :]
    prev2 = ctail_ref[6:7, :]
    u1 = jnp.where(row == 0, prev1, pltpu.roll(u, 1, axis=0))
    u2 = jnp.where(row == 0, prev2, jnp.where(row == 1, prev1, pltpu.roll(u, 2, axis=0)))
    conv_y = u2 * cw_ref[0:1, :] + u1 * cw_ref[1:2, :] + u * cw_ref[2:3, :]
    ctail_ref[...] = u[tq - 8:, :]
    abuf_ref[...] = (cb * conv_y).astype(BF16)

    for h in range(MEM_HEADS):
        p = _softmax_rows(mem_s[h])
        mbuf_ref[:, mcols[h]] = _dot(p.astype(BF16), mv_ref[:, mcols[h]]).astype(BF16)

    for h in range(RET_HEADS):
        for r in chunks:
            msk_ref[r, scols[h]] = (_dot_nt(q_ref[r, kcols[h]], k_ref[r, kcols[h]]) * dmask_ref[h]).astype(BF16)

    for h in range(RET_HEADS):
        for j, r in enumerate(chunks):
            kv_ref[h, j] = _dot_tn(kd_ref[r, kcols[h]], vb_ref[r, vcols[h]])

    for h in range(RET_HEADS):
        pg = proj(OFF_RG + h * RET_DV, RET_DV)
        s_cur = ret_ref[h]
        for j, r in enumerate(chunks):
            intra = _dot(msk_ref[r, scols[h]], vb_ref[r, vcols[h]])
            o = intra + _dot(q_ref[r, kcols[h]], s_cur.astype(BF16)) * qdec_ref[h]
            gated = jax.nn.silu(pg[r]) * (_group_norm(o) * gn_ref[:, vcols[h]])
            bbuf_ref[r, vcols[h]] = gated.astype(BF16)
            s_cur = s_cur * cd_ref[h] + kv_ref[h, j]
        ret_ref[h] = s_cur

    _merge_out_ln(x_ref, (abuf_ref, bbuf_ref, mbuf_ref), lambda i: proj(OFF_GL + i * D_MODEL, D_MODEL), bg_ref,
                  (wco_ref, wro_ref, wmo_ref), wo_ref, None, y_ref)


def _const(shape):
    nd = len(shape)
    return pl.BlockSpec(tuple(shape), lambda *_: (0,) * nd, pipeline_mode=pl.Buffered(1))


def _layer_const(shape, layer):
    nd = len(shape)
    return pl.BlockSpec((None,) + tuple(shape), lambda *_: (layer,) + (0,) * nd, pipeline_mode=pl.Buffered(1))


def _prompt_mixer(layer, x, tabs, mk_b, mv_b, w, to_cast, ret_stack):
    bp, seq, _ = x.shape
    tq = PROMPT_TILE
    grid = (bp, seq // tq)
    steps = bp * (seq // tq)
    steps_per_seq = seq // tq
    cast_in_specs, cast_out_specs, cast_out_shapes = [], [], []
    for a, src_layer in to_cast:
        _, rows, cols = a.shape
        slab = rows // steps
        assert slab * steps == rows and slab % 16 == 0
        cast_in_specs.append(pl.BlockSpec(
            (None, slab, cols), lambda b, t, src_layer=src_layer: (src_layer, b * steps_per_seq + t, 0)))
        cast_out_specs.append(pl.BlockSpec((slab, cols), lambda b, t: (b * steps_per_seq + t, 0)))
        cast_out_shapes.append(jax.ShapeDtypeStruct((rows, cols), BF16))
    in_specs = [
        pl.BlockSpec(memory_space=pltpu.SMEM),
        pl.BlockSpec((None, tq, D_MODEL), lambda b, t: (b, t, 0)),
        pl.BlockSpec((tq, RET_DK), lambda b, t: (t, 0)),
        pl.BlockSpec((tq, RET_DK), lambda b, t: (t, 0)),
        _const((RET_HEADS, MIXER_CHUNK, MIXER_CHUNK)),
        _const((RET_HEADS, MIXER_CHUNK, RET_DV)),
        _const((RET_HEADS, MIXER_CHUNK, RET_DK)),
        pl.BlockSpec((None, None, MEM_LEN, MEM_WIDTH), lambda b, t: (layer, b, 0, 0)),
        pl.BlockSpec((None, None, MEM_LEN, MEM_WIDTH), lambda b, t: (layer, b, 0, 0)),
        _const((D_MODEL, IN_WIDTH)),
        _layer_const((1, N_BRANCH * D_MODEL), layer),
        _layer_const((CONV_K, CONV_WIDTH), layer),
        _layer_const((1, RET_V_WIDTH), layer),
        _const((CONV_WIDTH, D_MODEL)),
        _const((RET_V_WIDTH, D_MODEL)),
        _const((MEM_WIDTH, D_MODEL)),
        _const((D_MODEL, D_MODEL)),
    ]
    out_specs = [
        pl.BlockSpec((None, tq, D_MODEL), lambda b, t: (b, t, 0)),
        pl.BlockSpec((None, 8, CONV_WIDTH), lambda b, t: (b, 0, 0)),
        pl.BlockSpec((None, None, RET_HEADS, RET_DK, RET_DV), lambda b, t: (layer, b, 0, 0, 0)),
    ]
    out_shape = [
        jax.ShapeDtypeStruct((bp, seq, D_MODEL), F32),
        jax.ShapeDtypeStruct((bp, 8, CONV_WIDTH), F32),
        jax.ShapeDtypeStruct((DEPTH, bp, RET_HEADS, RET_DK, RET_DV), F32),
    ]
    assert len(in_specs) == MIXER_INPUTS and len(out_specs) == MIXER_OUTPUTS
    args = (tabs["cd"], x, tabs["cos"], tabs["sin"], tabs["dmask"], tabs["qdec"], tabs["kdec"], mk_b, mv_b,
            w["w_in"], w["b_gate"], w["conv_w"], w["ret_gn_g"], w["w_conv_out"], w["w_ret_out"], w["w_mem_out"],
            w["w_out"], *[a for a, _ in to_cast])
    in_specs = in_specs + cast_in_specs
    aliases = {}
    if ret_stack is not None:
        aliases = {0: MIXER_OUTPUTS - 1}
        in_specs = [pl.BlockSpec(memory_space=pl.ANY)] + in_specs
        args = (ret_stack,) + args
    return pl.pallas_call(
        functools.partial(_prompt_mixer_kernel, ret_stack is not None, len(to_cast)),
        grid=grid,
        in_specs=in_specs,
        out_specs=out_specs + cast_out_specs,
        out_shape=out_shape + cast_out_shapes,
        scratch_shapes=[
            pltpu.VMEM((tq, D_MODEL), BF16),
            pltpu.VMEM((tq, RET_QK_WIDTH), BF16),
            pltpu.VMEM((tq, RET_QK_WIDTH), BF16),
            pltpu.VMEM((tq, RET_QK_WIDTH), BF16),
            pltpu.VMEM((tq, RET_V_WIDTH), BF16),
            pltpu.VMEM((tq, RET_HEADS * MIXER_CHUNK), BF16),
            pltpu.VMEM((RET_HEADS, tq // MIXER_CHUNK, RET_DK, RET_DV), F32),
            pltpu.VMEM((tq, CONV_WIDTH), BF16),
            pltpu.VMEM((tq, RET_V_WIDTH), BF16),
            pltpu.VMEM((tq, MEM_WIDTH), BF16),
        ],
        input_output_aliases=aliases,
        compiler_params=pltpu.CompilerParams(
            dimension_semantics=("arbitrary", "arbitrary"), vmem_limit_bytes=VMEM_LIMIT_BYTES),
        name="prompt_mixer",
    )(*args)


FFN_INPUTS = 8
STATE_INPUTS = 13
STATE_OUTPUTS = 5


def _ffn_state_kernel(aliased, *refs):
    if aliased:
        refs = refs[1:]
    z_first_ref, z_next_ref, l1g_ref, l1b_ref, wup_ref, wdn_ref, g_ref, b_ref = refs[:FFN_INPUTS]
    state_in = refs[FFN_INPUTS:FFN_INPUTS + STATE_INPUTS]
    y_ref = refs[FFN_INPUTS + STATE_INPUTS]
    rest = refs[FFN_INPUTS + STATE_INPUTS + 1:]
    x1_scr, next_scr = rest[-2:]
    stages = _sample_state_stages(*state_in, *rest[:-2])
    n_chunks = D_FF // FFN_CHUNK
    assert len(stages) == 2 * n_chunks - 1

    @pl.when(pl.program_id(0) == 0)
    def _():
        x1_scr[...] = _layer_norm(z_first_ref[...], l1g_ref[...], l1b_ref[...])

    next_scr[...] = _layer_norm(z_next_ref[...], l1g_ref[...], l1b_ref[...])
    xb = x1_scr[...].astype(BF16)
    acc = None
    for c in range(n_chunks):
        cols = slice(c * FFN_CHUNK, (c + 1) * FFN_CHUNK)
        h = jnp.maximum(_dot(xb, wup_ref[:, cols]), 0.0)
        stages[2 * c]()
        part = _dot((h * h).astype(BF16), wdn_ref[cols, :])
        if c + 1 < n_chunks:
            stages[2 * c + 1]()
        acc = part if acc is None else acc + part
    y_ref[...] = _layer_norm(DEEPNORM_ALPHA * x1_scr[...] + acc, g_ref[...], b_ref[...])
    x1_scr[...] = next_scr[...]


def _ffn_and_sample_state(layer, z, w, proj, cst, tabs, state_ret, kc, vc, ret_stack):
    n = z.shape[0]
    tm = FFN_TILE
    ns = SAMPLE_SEQS
    n_tok = proj.shape[0]
    nseq = state_ret.shape[1]
    t_len = n_tok // nseq
    rows = ns * t_len
    n_tiles = n // tm
    assert n_tiles == nseq // ns
    tok_spec = lambda width: pl.BlockSpec((rows, width), lambda i: (i, 0))
    in_specs = [
        pl.BlockSpec((tm, D_MODEL), lambda i: (0, 0)),
        pl.BlockSpec((tm, D_MODEL), lambda i: (jnp.minimum(i + 1, n_tiles - 1), 0)),
        _layer_const((1, D_MODEL), layer),
        _layer_const((1, D_MODEL), layer),
        _const((D_MODEL, D_FF)),
        _const((D_FF, D_MODEL)),
        _layer_const((1, D_MODEL), layer),
        _layer_const((1, D_MODEL), layer),
        pl.BlockSpec(memory_space=pltpu.SMEM),
        tok_spec(OFF_GL),
        pl.BlockSpec((None, rows, CONV_WIDTH), lambda i: (layer, i, 0)),
        _const((rows, RET_DK)),
        _const((rows, RET_DK)),
        _const((RET_HEADS, rows, rows)),
        _const((RET_HEADS, t_len, RET_DV)),
        _const((RET_HEADS, rows, RET_DK)),
        pl.BlockSpec((None, ns, RET_HEADS, RET_DK, RET_DV), lambda i: (layer, i, 0, 0, 0)),
        pl.BlockSpec((None, ns, MEM_LEN * MEM_HEADS, MEM_HEAD_DIM), lambda i: (layer, i, 0, 0)),
        pl.BlockSpec((None, ns, MEM_LEN * MEM_HEADS, MEM_HEAD_DIM), lambda i: (layer, i, 0, 0)),
        _layer_const((CONV_K, CONV_WIDTH), layer),
        _layer_const((1, RET_V_WIDTH), layer),
    ]
    out_specs = [
        pl.BlockSpec((tm, D_MODEL), lambda i: (i, 0)),
        tok_spec(CONV_WIDTH),
        tok_spec(RET_V_WIDTH),
        tok_spec(MEM_WIDTH),
        tok_spec(CONV_WIDTH),
        pl.BlockSpec((None, ns, RET_HEADS, RET_DK, RET_DV), lambda i: (layer, i, 0, 0, 0)),
    ]
    out_shape = [
        jax.ShapeDtypeStruct((n, D_MODEL), F32),
        jax.ShapeDtypeStruct((n_tok, CONV_WIDTH), BF16),
        jax.ShapeDtypeStruct((n_tok, RET_V_WIDTH), BF16),
        jax.ShapeDtypeStruct((n_tok, MEM_WIDTH), BF16),
        jax.ShapeDtypeStruct((n_tok, CONV_WIDTH), F32),
        jax.ShapeDtypeStruct((DEPTH, nseq, RET_HEADS, RET_DK, RET_DV), F32),
    ]
    assert len(in_specs) == FFN_INPUTS + STATE_INPUTS and len(out_specs) == 1 + STATE_OUTPUTS
    args = (z, z, w["ln1_g"], w["ln1_b"], w["w_up"], w["w_down"], w["ln2_g"], w["ln2_b"],
            tabs["cd"], proj, cst, tabs["cos"], tabs["sin"], tabs["dmask"], tabs["qdec"], tabs["kdec"],
            state_ret, kc, vc, w["conv_w"], w["ret_gn_g"])
    aliases = {}
    if ret_stack is not None:
        aliases = {0: len(out_shape) - 1}
        in_specs = [pl.BlockSpec(memory_space=pl.ANY)] + in_specs
        args = (ret_stack,) + args
    scratch = [
        pltpu.VMEM((rows, RET_V_WIDTH), F32),
        pltpu.VMEM((rows, MEM_HEADS * MEM_LEN), F32),
        pltpu.VMEM((rows, MEM_WIDTH), F32),
        pltpu.VMEM((tm, D_MODEL), F32),
        pltpu.VMEM((tm, D_MODEL), F32),
    ]
    return pl.pallas_call(
        functools.partial(_ffn_state_kernel, ret_stack is not None),
        grid=(n // tm,),
        in_specs=in_specs,
        out_specs=out_specs,
        out_shape=out_shape,
        scratch_shapes=scratch,
        input_output_aliases=aliases,
        compiler_params=pltpu.CompilerParams(
            dimension_semantics=("arbitrary",), vmem_limit_bytes=VMEM_LIMIT_BYTES),
        name="ffn_state",
    )(*args)


def _sample_proj_kernel(x_ref, w_ref, o_ref):
    o_ref[...] = _dot(x_ref[...].astype(BF16), w_ref[...])


def _sample_proj(x, w_in_b):
    n = x.shape[0]
    tn = SAMPLE_PROJ_COLS
    return pl.pallas_call(
        _sample_proj_kernel,
        grid=(IN_WIDTH // tn,),
        in_specs=[
            pl.BlockSpec((n, D_MODEL), lambda j: (0, 0)),
            pl.BlockSpec((D_MODEL, tn), lambda j: (0, j)),
        ],
        out_specs=pl.BlockSpec((n, tn), lambda j: (0, j)),
        out_shape=jax.ShapeDtypeStruct((n, IN_WIDTH), F32),
        compiler_params=pltpu.CompilerParams(
            dimension_semantics=("arbitrary",), vmem_limit_bytes=VMEM_LIMIT_BYTES),
        name="sample_proj",
    )(x, w_in_b)


def _sample_state_stages(cd_ref, proj_ref, cst_ref, cos_ref, sin_ref, dmask_ref, qdec_ref, kdec_ref,
                         sret_ref, kc_ref, vc_ref, cw_ref, gn_ref,
                         a_ref, b_ref, m_ref, u_ref, nret_ref,
                         o_scr, p_scr, m_scr):
    ns = sret_ref.shape[0]
    t_len = proj_ref.shape[0] // ns
    rows = [slice(s * t_len, (s + 1) * t_len) for s in range(ns)]
    kcols = [slice(h * RET_DK, (h + 1) * RET_DK) for h in range(RET_HEADS)]
    vcols = [slice(h * RET_DV, (h + 1) * RET_DV) for h in range(RET_HEADS)]
    mcols = [slice(h * MEM_HEAD_DIM, (h + 1) * MEM_HEAD_DIM) for h in range(MEM_HEADS)]
    pcols = [slice(h * MEM_LEN, (h + 1) * MEM_LEN) for h in range(MEM_HEADS)]
    head_rows = [pl.ds(h, MEM_LEN, stride=MEM_HEADS) for h in range(MEM_HEADS)]

    q, kd, masked = [], [], []

    def conv_and_scores():
        cb = proj_ref[:, OFF_CB:OFF_CB + CONV_WIDTH]
        u = proj_ref[:, OFF_CC:OFF_CC + CONV_WIDTH] * proj_ref[:, OFF_CH:OFF_CH + CONV_WIDTH]
        tok = lax.broadcasted_iota(jnp.int32, u.shape, 0) & (t_len - 1)
        st2 = cst_ref[...]
        st1 = pltpu.roll(st2, st2.shape[0] - 1, axis=0)
        u1 = jnp.where(tok == 0, st1, pltpu.roll(u, 1, axis=0))
        u2 = jnp.where(tok < 2, st2, pltpu.roll(u, 2, axis=0))
        conv_y = u2 * cw_ref[0:1, :] + u1 * cw_ref[1:2, :] + u * cw_ref[2:3, :]
        u_ref[...] = u
        a_ref[...] = (cb * conv_y).astype(BF16)

        cos = cos_ref[...]
        sin = sin_ref[...]
        for h in range(RET_HEADS):
            qh = _rotary(proj_ref[:, OFF_RQ + h * RET_DK:OFF_RQ + (h + 1) * RET_DK], cos, sin)
            kh = _rotary(proj_ref[:, OFF_RK + h * RET_DK:OFF_RK + (h + 1) * RET_DK], cos, sin) * (RET_DK ** -0.5)
            q.append(qh)
            kd.append(kh * kdec_ref[h])
            masked.append((_dot_nt(qh.astype(BF16), kh.astype(BF16)) * dmask_ref[h]).astype(BF16))

    def intra_chunk():
        for h in range(RET_HEADS):
            vb = proj_ref[:, OFF_RV + h * RET_DV:OFF_RV + (h + 1) * RET_DV].astype(BF16)
            o_scr[:, vcols[h]] = _dot(masked[h], vb)

    def cross_chunk_and_state(seqs):
        for s in seqs:
            for h in range(RET_HEADS):
                s_prev = sret_ref[s, h]
                inter = _dot(q[h][rows[s]].astype(BF16), s_prev.astype(BF16))
                o_scr[rows[s], vcols[h]] += inter * qdec_ref[h]
                vb = proj_ref[rows[s], OFF_RV + h * RET_DV:OFF_RV + (h + 1) * RET_DV].astype(BF16)
                nret_ref[s, h] = s_prev * cd_ref[h] + _dot_tn(kd[h][rows[s]].astype(BF16), vb)

    def attention_scores():
        for s in range(ns):
            for h in range(MEM_HEADS):
                mq = proj_ref[rows[s], OFF_MQ + h * MEM_HEAD_DIM:OFF_MQ + (h + 1) * MEM_HEAD_DIM].astype(BF16)
                p_scr[rows[s], pcols[h]] = (_dot_nt(mq, kc_ref[s, head_rows[h], :].astype(BF16))
                                            * (MEM_HEAD_DIM ** -0.5))

    def attention_values():
        for h in range(MEM_HEADS):
            p_scr[:, pcols[h]] = _softmax_rows(p_scr[:, pcols[h]])
        for s in range(ns):
            for h in range(MEM_HEADS):
                p = p_scr[rows[s], pcols[h]].astype(BF16)
                m_scr[rows[s], mcols[h]] = _dot(p, vc_ref[s, head_rows[h], :].astype(BF16))
        m_ref[...] = m_scr[...].astype(BF16)

    def norm_and_gate():
        for h in range(RET_HEADS):
            rg = proj_ref[:, OFF_RG + h * RET_DV:OFF_RG + (h + 1) * RET_DV]
            gated = jax.nn.silu(rg) * (_group_norm(o_scr[:, vcols[h]]) * gn_ref[:, vcols[h]])
            b_ref[:, vcols[h]] = gated.astype(BF16)

    first, second = list(range(ns // 2)), list(range(ns // 2, ns))
    return [conv_and_scores, intra_chunk,
            functools.partial(cross_chunk_and_state, first), functools.partial(cross_chunk_and_state, second),
            attention_scores, attention_values, norm_and_gate]


def _sample_post_kernel(x_ref, a_ref, b_ref, m_ref, g0_ref, g1_ref, g2_ref,
                        bg_ref, wco_ref, wro_ref, wmo_ref, wo_ref, l1g_ref, l1b_ref,
                        wup_hbm, wdn_hbm, l2g_ref, l2b_ref, y_ref, x1_ref, wup_ref, wdn_ref, sem):
    first_step = pl.program_id(0) == 0

    def weight_copies():
        return (pltpu.make_async_copy(wup_hbm, wup_ref, sem.at[0]),
                pltpu.make_async_copy(wdn_hbm, wdn_ref, sem.at[1]))

    @pl.when(first_step)
    def _():
        for copy in weight_copies():
            copy.start()

    gate_refs = (g0_ref, g1_ref, g2_ref)
    _merge_out_ln(x_ref, (a_ref, b_ref, m_ref), lambda i: gate_refs[i][...], bg_ref,
                  (wco_ref, wro_ref, wmo_ref), wo_ref, (l1g_ref, l1b_ref), x1_ref)

    @pl.when(first_step)
    def _():
        for copy in weight_copies():
            copy.wait()

    y_ref[...] = _ffn_ln(x1_ref[...], wup_ref, wdn_ref, l2g_ref, l2b_ref)


def _sample_post(layer, x, a_pre, b_pre, m_pre, proj, w):
    n = x.shape[0]
    tm = SAMPLE_POST_TILE
    tok = lambda width: pl.BlockSpec((tm, width), lambda i: (i, 0))
    gate = lambda k: pl.BlockSpec((tm, D_MODEL), lambda i: (i, OFF_GL // D_MODEL + k))
    return pl.pallas_call(
        _sample_post_kernel,
        grid=(n // tm,),
        in_specs=[
            tok(D_MODEL), tok(CONV_WIDTH), tok(RET_V_WIDTH), tok(MEM_WIDTH), gate(0), gate(1), gate(2),
            _layer_const((1, N_BRANCH * D_MODEL), layer),
            _const((CONV_WIDTH, D_MODEL)),
            _const((RET_V_WIDTH, D_MODEL)),
            _const((MEM_WIDTH, D_MODEL)),
            _const((D_MODEL, D_MODEL)),
            _layer_const((1, D_MODEL), layer),
            _layer_const((1, D_MODEL), layer),
            pl.BlockSpec(memory_space=pl.ANY),
            pl.BlockSpec(memory_space=pl.ANY),
            _layer_const((1, D_MODEL), layer),
            _layer_const((1, D_MODEL), layer),
        ],
        out_specs=tok(D_MODEL),
        out_shape=jax.ShapeDtypeStruct((n, D_MODEL), F32),
        scratch_shapes=[
            pltpu.VMEM((tm, D_MODEL), F32),
            pltpu.VMEM((D_MODEL, D_FF), BF16),
            pltpu.VMEM((D_FF, D_MODEL), BF16),
            pltpu.SemaphoreType.DMA((2,)),
        ],
        compiler_params=pltpu.CompilerParams(
            dimension_semantics=("arbitrary",), vmem_limit_bytes=VMEM_LIMIT_BYTES),
        name="sample_post",
    )(x, a_pre, b_pre, m_pre, proj, proj, proj,
      w["b_gate"], w["w_conv_out"], w["w_ret_out"], w["w_mem_out"], w["w_out"], w["ln1_g"], w["ln1_b"],
      w["w_up"], w["w_down"], w["ln2_g"], w["ln2_b"])


def _rotary_tables(pos, reps):
    half = RET_DK // 2
    inv = ROPE_BASE ** (-jnp.arange(half, dtype=F32) / half)
    ang = pos.astype(F32)[:, None] * inv[None, :]
    cos = jnp.cos(ang)
    sin = jnp.sin(ang)
    cos_full = jnp.concatenate([cos, cos], axis=-1)
    sin_signed = jnp.concatenate([-sin, sin], axis=-1)
    return jnp.tile(cos_full, (reps, 1)), jnp.tile(sin_signed, (reps, 1))


def _decay_tables(c):
    h = RET_HEADS
    log_g = jnp.log1p(-jnp.exp2(-5.0 - jnp.arange(h, dtype=F32)))
    idx = jnp.arange(c, dtype=F32)
    rel = idx[:, None] - idx[None, :]
    causal = rel >= 0
    dmask = jnp.where(causal[None], jnp.exp(jnp.where(causal, rel, 0.0)[None] * log_g[:, None, None]), 0.0)
    q_decay = jnp.exp((idx + 1.0)[None, :] * log_g[:, None])
    k_decay = jnp.exp((c - 1.0 - idx)[None, :] * log_g[:, None])
    chunk_decay = jnp.exp(c * log_g)
    return {
        "dmask": dmask,
        "qdec": jnp.broadcast_to(q_decay[:, :, None], (h, c, RET_DV)),
        "kdec": jnp.broadcast_to(k_decay[:, :, None], (h, c, RET_DK)),
        "cd": chunk_decay,
    }


def kernel(x_prompt, x_sample, cache_mem_k, cache_mem_v, state_conv, state_ret, mem_prompt,
           w_in, b_gate, conv_w, ret_gn_g, w_conv_out, w_ret_out, w_mem_out, w_out, w_mem_kv,
           ln1_g, ln1_b, w_up, w_down, ln2_g, ln2_b):
    bp, seq, _ = x_prompt.shape
    nseq, t_len, _ = x_sample.shape
    assert seq % PROMPT_TILE == 0 and PROMPT_TILE % MIXER_CHUNK == 0
    assert t_len == 8 and nseq % SAMPLE_SEQS == 0 and t_len % RET_CHUNK != 0

    mixer_w = {"w_in": w_in, "w_conv_out": w_conv_out, "w_ret_out": w_ret_out, "w_mem_out": w_mem_out, "w_out": w_out}
    mlp_w = {"w_up": w_up, "w_down": w_down}
    small = {name: a.reshape(DEPTH, 1, a.shape[-1]) for name, a in
             {"b_gate": b_gate, "ret_gn_g": ret_gn_g, "ln1_g": ln1_g, "ln1_b": ln1_b,
              "ln2_g": ln2_g, "ln2_b": ln2_b}.items()}

    tabs_p = _decay_tables(MIXER_CHUNK)
    tabs_p["cos"], tabs_p["sin"] = _rotary_tables(jnp.arange(seq, dtype=jnp.int32), 1)
    tabs_s = _decay_tables(t_len)
    eye = jnp.eye(SAMPLE_SEQS, dtype=F32)
    tabs_s["dmask"] = (eye[None, :, None, :, None] * tabs_s["dmask"][:, None, :, None, :]).reshape(
        RET_HEADS, SAMPLE_SEQS * t_len, SAMPLE_SEQS * t_len)
    tabs_s["kdec"] = jnp.tile(tabs_s["kdec"], (1, SAMPLE_SEQS, 1))
    tabs_s["cos"], tabs_s["sin"] = _rotary_tables(PAST_LEN + jnp.arange(t_len, dtype=jnp.int32), SAMPLE_SEQS)

    mk_f, mv_f, mk_b, mv_b, *cast = _mem_kv(mem_prompt, w_mem_kv, list(mixer_w.values()))
    mixer_b = dict(zip(mixer_w, cast))
    mk_b = mk_b.reshape(DEPTH, bp, MEM_LEN, MEM_WIDTH)
    mv_b = mv_b.reshape(DEPTH, bp, MEM_LEN, MEM_WIDTH)

    kc = cache_mem_k.reshape(DEPTH, nseq, MEM_LEN * MEM_HEADS, MEM_HEAD_DIM)
    vc = cache_mem_v.reshape(DEPTH, nseq, MEM_LEN * MEM_HEADS, MEM_HEAD_DIM)
    cst = jnp.concatenate(
        [state_conv, jnp.zeros((DEPTH, nseq, t_len - (CONV_K - 1), CONV_WIDTH), state_conv.dtype)], axis=2
    ).reshape(DEPTH, nseq * t_len, CONV_WIDTH)

    xp = x_prompt
    xs = x_sample.reshape(nseq * t_len, D_MODEL)
    cp_list, cs_list = [], []
    ret_stack = ret_stack_p = None
    for l in range(DEPTH):
        w = dict(mixer_b, conv_w=conv_w, **small)
        to_cast = [(a, l) for a in mlp_w.values()]
        if l + 1 < DEPTH:
            to_cast += [(a, l + 1) for a in mixer_w.values()]
        zp, ctail_p, ret_stack_p, *cast = _prompt_mixer(l, xp, tabs_p, mk_b, mv_b, w, to_cast, ret_stack_p)
        w.update(zip(mlp_w, cast[:len(mlp_w)]))
        mixer_b = dict(zip(mixer_w, cast[len(mlp_w):]))
        cp_list.append(ctail_p[:, 8 - (CONV_K - 1):, :])

        proj = _sample_proj(xs, w["w_in"])
        xp, a_pre, b_pre, m_pre, u_s, ret_stack = _ffn_and_sample_state(
            l, zp.reshape(bp * seq, D_MODEL), w, proj, cst, tabs_s, state_ret, kc, vc, ret_stack)
        xp = xp.reshape(bp, seq, D_MODEL)
        xs = _sample_post(l, xs, a_pre, b_pre, m_pre, proj, w)
        cs_list.append(u_s.reshape(nseq, t_len, CONV_WIDTH)[:, t_len - (CONV_K - 1):, :])

    mem_shape = (DEPTH, bp, MEM_LEN, MEM_HEADS, MEM_HEAD_DIM)
    return (xp, xs.reshape(nseq, t_len, D_MODEL), mk_f.reshape(mem_shape), mv_f.reshape(mem_shape),
            jnp.stack(cp_list), ret_stack_p, jnp.stack(cs_list), ret_stack)
```

```python
import functools

import jax
import jax.numpy as jnp
from jax import lax
from jax.experimental import pallas as pl
from jax.experimental.pallas import tpu as pltpu

D_MODEL = 1024
DEPTH = 4
CONV_WIDTH = 512
CONV_K = 3
RET_HEADS = 4
RET_DK = 128
RET_DV = 256
RET_QK_WIDTH = RET_HEADS * RET_DK
RET_V_WIDTH = RET_HEADS * RET_DV
RET_CHUNK = 128
MEM_LEN = 256
MEM_HEADS = 4
MEM_HEAD_DIM = 128
MEM_WIDTH = MEM_HEADS * MEM_HEAD_DIM
N_BRANCH = 3
D_FF = 4 * D_MODEL
ROPE_BASE = 10000.0
LN_EPS = 1e-5
GN_EPS = 1e-6
PAST_LEN = 16384
DEEPNORM_ALPHA = (2 * DEPTH) ** 0.25

OFF_CB = 0
OFF_CC = OFF_CB + CONV_WIDTH
OFF_CH = OFF_CC + CONV_WIDTH
OFF_RQ = OFF_CH + CONV_WIDTH
OFF_RK = OFF_RQ + RET_QK_WIDTH
OFF_RV = OFF_RK + RET_QK_WIDTH
OFF_RG = OFF_RV + RET_V_WIDTH
OFF_MQ = OFF_RG + RET_V_WIDTH
OFF_GL = OFF_MQ + MEM_WIDTH
IN_WIDTH = OFF_GL + N_BRANCH * D_MODEL

V7X_VMEM_BYTES = 64 * 1024 * 1024
VMEM_LIMIT_BYTES = V7X_VMEM_BYTES - 6 * 1024 * 1024

PROMPT_TILE = 512
MIXER_CHUNK = 256
FFN_TILE = 512
FFN_CHUNK = 1024
SAMPLE_SEQS = 4
SAMPLE_POST_TILE = 512
SAMPLE_PROJ_COLS = 2048

BF16 = jnp.bfloat16
F32 = jnp.float32


def _dot(a, b):
    return jnp.dot(a, b, preferred_element_type=F32)


def _dot_nt(a, b):
    return lax.dot_general(a, b, (((1,), (1,)), ((), ())), preferred_element_type=F32)


def _dot_tn(a, b):
    return lax.dot_general(a, b, (((0,), (0,)), ((), ())), preferred_element_type=F32)


def _layer_norm(z, g, b):
    mu = jnp.mean(z, axis=-1, keepdims=True)
    zc = z - mu
    var = jnp.mean(zc * zc, axis=-1, keepdims=True)
    return zc * lax.rsqrt(var + LN_EPS) * g + b


def _group_norm(o):
    mu = jnp.mean(o, axis=-1, keepdims=True)
    oc = o - mu
    var = jnp.mean(oc * oc, axis=-1, keepdims=True)
    return oc * lax.rsqrt(var + GN_EPS)


def _rotary(xh, cos, sin_signed):
    return xh * cos + pltpu.roll(xh, RET_DK // 2, axis=1) * sin_signed


def _softmax_rows(s):
    m = jnp.max(s, axis=-1, keepdims=True)
    e = jnp.exp(s - m)
    return e / jnp.sum(e, axis=-1, keepdims=True)


def _merge_out_ln(x_ref, pre_refs, gate_logits, bg_ref, w_refs, wo_ref, ln_refs, y_ref):
    merged = None
    for i, (pre_ref, w_ref) in enumerate(zip(pre_refs, w_refs)):
        gate = jax.nn.sigmoid(gate_logits(i) + bg_ref[:, i * D_MODEL:(i + 1) * D_MODEL])
        term = gate * _dot(pre_ref[...], w_ref[...])
        merged = term if merged is None else merged + term
    merged = merged.astype(BF16)
    half = merged.shape[0] // 2
    for rows in (slice(0, half), slice(half, 2 * half)):
        z = DEEPNORM_ALPHA * x_ref[rows, :] + _dot(merged[rows], wo_ref[...])
        y_ref[rows, :] = z if ln_refs is None else _layer_norm(z, ln_refs[0][...], ln_refs[1][...])


def _ffn_ln(x, wup_ref, wdn_ref, g_ref, b_ref):
    xb = x.astype(BF16)
    acc = None
    for c in range(D_FF // FFN_CHUNK):
        cols = slice(c * FFN_CHUNK, (c + 1) * FFN_CHUNK)
        h = jnp.maximum(_dot(xb, wup_ref[:, cols]), 0.0)
        part = _dot((h * h).astype(BF16), wdn_ref[cols, :])
        acc = part if acc is None else acc + part
    return _layer_norm(DEEPNORM_ALPHA * x + acc, g_ref[...], b_ref[...])


MEM_KV_ROWS = 512
MEM_KV_OUTPUTS = 4


def _mem_kv_kernel(n_cast, mem_ref, w_ref, *refs):
    cast_src = refs[:n_cast]
    k_ref, v_ref, kb_ref, vb_ref = refs[n_cast:n_cast + MEM_KV_OUTPUTS]
    cast_dst = refs[n_cast + MEM_KV_OUTPUTS:n_cast + MEM_KV_OUTPUTS + n_cast]
    wb_ref = refs[-1]
    j = pl.program_id(1)

    for src_ref, dst_ref in zip(cast_src, cast_dst):
        dst_ref[...] = src_ref[...].astype(BF16)

    @pl.when(j == 0)
    def _():
        wb_ref[...] = w_ref[...].astype(BF16)

    rows = pl.ds(pl.multiple_of(j * MEM_KV_ROWS, MEM_KV_ROWS), MEM_KV_ROWS)
    kv = _dot(mem_ref[rows, :].astype(BF16), wb_ref[...])
    kb_ref[...] = kv[:, :MEM_WIDTH].astype(BF16)
    vb_ref[...] = kv[:, MEM_WIDTH:].astype(BF16)
    for h in range(MEM_HEADS):
        head_rows = pl.ds(h, MEM_KV_ROWS, stride=MEM_HEADS)
        k_ref[head_rows, :] = kv[:, h * MEM_HEAD_DIM:(h + 1) * MEM_HEAD_DIM]
        v_ref[head_rows, :] = kv[:, MEM_WIDTH + h * MEM_HEAD_DIM:MEM_WIDTH + (h + 1) * MEM_HEAD_DIM]


def _mem_kv(mem_prompt, w_mem_kv, to_cast):
    bp = mem_prompt.shape[0]
    n = bp * MEM_LEN
    nsub = n // MEM_KV_ROWS
    steps = DEPTH * nsub
    assert nsub * MEM_KV_ROWS == n
    cast_in_specs, cast_out_specs, cast_out_shapes = [], [], []
    for a in to_cast:
        _, rows, cols = a.shape
        slab = rows // steps
        assert slab * steps == rows and slab % 16 == 0
        cast_in_specs.append(pl.BlockSpec((None, slab, cols), lambda l, j: (0, l * nsub + j, 0)))
        cast_out_specs.append(pl.BlockSpec((slab, cols), lambda l, j: (l * nsub + j, 0)))
        cast_out_shapes.append(jax.ShapeDtypeStruct((rows, cols), BF16))
    out_f = jax.ShapeDtypeStruct((DEPTH, n * MEM_HEADS, MEM_HEAD_DIM), F32)
    out_b = jax.ShapeDtypeStruct((DEPTH, n, MEM_WIDTH), BF16)
    f_spec = pl.BlockSpec((None, MEM_KV_ROWS * MEM_HEADS, MEM_HEAD_DIM), lambda l, j: (l, j, 0))
    b_spec = pl.BlockSpec((None, MEM_KV_ROWS, MEM_WIDTH), lambda l, j: (l, j, 0))
    return pl.pallas_call(
        functools.partial(_mem_kv_kernel, len(to_cast)),
        grid=(DEPTH, nsub),
        in_specs=[
            _const((n, D_MODEL)),
            pl.BlockSpec((None, D_MODEL, 2 * MEM_WIDTH), lambda l, j: (l, 0, 0)),
        ] + cast_in_specs,
        out_specs=[f_spec, f_spec, b_spec, b_spec] + cast_out_specs,
        out_shape=[out_f, out_f, out_b, out_b] + cast_out_shapes,
        scratch_shapes=[pltpu.VMEM((D_MODEL, 2 * MEM_WIDTH), BF16)],
        compiler_params=pltpu.CompilerParams(
            dimension_semantics=("arbitrary", "arbitrary"), vmem_limit_bytes=VMEM_LIMIT_BYTES),
        name="mem_kv",
    )(mem_prompt.reshape(n, D_MODEL), w_mem_kv, *to_cast)


MIXER_INPUTS = 17
MIXER_OUTPUTS = 3


def _prompt_mixer_kernel(aliased, n_cast, *refs):
    if aliased:
        refs = refs[1:]
    (cd_ref, x_ref, cos_ref, sin_ref, dmask_ref, qdec_ref, kdec_ref, mk_ref, mv_ref,
     win_ref, bg_ref, cw_ref, gn_ref, wco_ref, wro_ref, wmo_ref, wo_ref) = refs[:MIXER_INPUTS]
    cast_src = refs[MIXER_INPUTS:MIXER_INPUTS + n_cast]
    outs = refs[MIXER_INPUTS + n_cast:]
    y_ref, ctail_ref, ret_ref = outs[:MIXER_OUTPUTS]
    cast_dst = outs[MIXER_OUTPUTS:MIXER_OUTPUTS + n_cast]
    (xb_ref, q_ref, k_ref, kd_ref, vb_ref, msk_ref, kv_ref,
     abuf_ref, bbuf_ref, mbuf_ref) = outs[MIXER_OUTPUTS + n_cast:]

    for src_ref, dst_ref in zip(cast_src, cast_dst):
        dst_ref[...] = src_ref[...].astype(BF16)

    t = pl.program_id(1)
    tq = x_ref.shape[0]
    chunk = dmask_ref.shape[1]
    chunks = [slice(j * chunk, (j + 1) * chunk) for j in range(tq // chunk)]
    scols = [slice(h * chunk, (h + 1) * chunk) for h in range(RET_HEADS)]
    kcols = [slice(h * RET_DK, (h + 1) * RET_DK) for h in range(RET_HEADS)]
    vcols = [slice(h * RET_DV, (h + 1) * RET_DV) for h in range(RET_HEADS)]
    mcols = [slice(h * MEM_HEAD_DIM, (h + 1) * MEM_HEAD_DIM) for h in range(MEM_HEADS)]

    @pl.when(t == 0)
    def _():
        ret_ref[...] = jnp.zeros_like(ret_ref)
        ctail_ref[...] = jnp.zeros_like(ctail_ref)

    xb_ref[...] = x_ref[...].astype(BF16)

    def proj(lo, width):
        return _dot(xb_ref[...], win_ref[:, lo:lo + width])

    pq = proj(OFF_RQ, RET_QK_WIDTH)
    pk = proj(OFF_RK, RET_QK_WIDTH)
    vb_ref[...] = proj(OFF_RV, RET_V_WIDTH).astype(BF16)
    cos = cos_ref[...]
    sin = sin_ref[...]
    for h in range(RET_HEADS):
        q_ref[:, kcols[h]] = _rotary(pq[:, kcols[h]], cos, sin).astype(BF16)
        kh = _rotary(pk[:, kcols[h]], cos, sin) * (RET_DK ** -0.5)
        k_ref[:, kcols[h]] = kh.astype(BF16)
        for r in chunks:
            kd_ref[r, kcols[h]] = (kh[r] * kdec_ref[h]).astype(BF16)

    pm = proj(OFF_MQ, MEM_WIDTH).astype(BF16)
    mem_s = [_dot_nt(pm[:, mcols[h]], mk_ref[:, mcols[h]]) * (MEM_HEAD_DIM ** -0.5) for h in range(MEM_HEADS)]

    cb = proj(OFF_CB, CONV_WIDTH)
    u = proj(OFF_CC, CONV_WIDTH) * proj(OFF_CH, CONV_WIDTH)
    row = lax.broadcasted_iota(jnp.int32, u.shape, 0)
    prev1 = ctail_ref[7:8, :]
    prev2 = ctail_ref[6:7, :]
    u1 = jnp.where(row == 0, prev1, pltpu.roll(u, 1, axis=0))
    u2 = jnp.where(row == 0, prev2, jnp.where(row == 1, prev1, pltpu.roll(u, 2, axis=0)))
    conv_y = u2 * cw_ref[0:1, :] + u1 * cw_ref[1:2, :] + u * cw_ref[2:3, :]
    ctail_ref[...] = u[tq - 8:, :]
    abuf_ref[...] = (cb * conv_y).astype(BF16)

    for h in range(MEM_HEADS):
        p = _softmax_rows(mem_s[h])
        mbuf_ref[:, mcols[h]] = _dot(p.astype(BF16), mv_ref[:, mcols[h]]).astype(BF16)

    for h in range(RET_HEADS):
        for r in chunks:
            msk_ref[r, scols[h]] = (_dot_nt(q_ref[r, kcols[h]], k_ref[r, kcols[h]]) * dmask_ref[h]).astype(BF16)

    for h in range(RET_HEADS):
        for j, r in enumerate(chunks):
            kv_ref[h, j] = _dot_tn(kd_ref[r, kcols[h]], vb_ref[r, vcols[h]])

    for h in range(RET_HEADS):
        pg = proj(OFF_RG + h * RET_DV, RET_DV)
        s_cur = ret_ref[h]
        for j, r in enumerate(chunks):
            intra = _dot(msk_ref[r, scols[h]], vb_ref[r, vcols[h]])
            o = intra + _dot(q_ref[r, kcols[h]], s_cur.astype(BF16)) * qdec_ref[h]
            gated = jax.nn.silu(pg[r]) * (_group_norm(o) * gn_ref[:, vcols[h]])
            bbuf_ref[r, vcols[h]] = gated.astype(BF16)
            s_cur = s_cur * cd_ref[h] + kv_ref[h, j]
        ret_ref[h] = s_cur

    _merge_out_ln(x_ref, (abuf_ref, bbuf_ref, mbuf_ref), lambda i: proj(OFF_GL + i * D_MODEL, D_MODEL), bg_ref,
                  (wco_ref, wro_ref, wmo_ref), wo_ref, None, y_ref)


def _const(shape):
    nd = len(shape)
    return pl.BlockSpec(tuple(shape), lambda *_: (0,) * nd, pipeline_mode=pl.Buffered(1))


def _layer_const(shape, layer):
    nd = len(shape)
    return pl.BlockSpec((None,) + tuple(shape), lambda *_: (layer,) + (0,) * nd, pipeline_mode=pl.Buffered(1))


def _prompt_mixer(layer, x, tabs, mk_b, mv_b, w, to_cast, ret_stack):
    bp, seq, _ = x.shape
    tq = PROMPT_TILE
    grid = (bp, seq // tq)
    steps = bp * (seq // tq)
    steps_per_seq = seq // tq
    cast_in_specs, cast_out_specs, cast_out_shapes = [], [], []
    for a, src_layer in to_cast:
        _, rows, cols = a.shape
        slab = rows // steps
        assert slab * steps == rows and slab % 16 == 0
        cast_in_specs.append(pl.BlockSpec(
            (None, slab, cols), lambda b, t, src_layer=src_layer: (src_layer, b * steps_per_seq + t, 0)))
        cast_out_specs.append(pl.BlockSpec((slab, cols), lambda b, t: (b * steps_per_seq + t, 0)))
        cast_out_shapes.append(jax.ShapeDtypeStruct((rows, cols), BF16))
    in_specs = [
        pl.BlockSpec(memory_space=pltpu.SMEM),
        pl.BlockSpec((None, tq, D_MODEL), lambda b, t: (b, t, 0)),
        pl.BlockSpec((tq, RET_DK), lambda b, t: (t, 0)),
        pl.BlockSpec((tq, RET_DK), lambda b, t: (t, 0)),
        _const((RET_HEADS, MIXER_CHUNK, MIXER_CHUNK)),
        _const((RET_HEADS, MIXER_CHUNK, RET_DV)),
        _const((RET_HEADS, MIXER_CHUNK, RET_DK)),
        pl.BlockSpec((None, None, MEM_LEN, MEM_WIDTH), lambda b, t: (layer, b, 0, 0)),
        pl.BlockSpec((None, None, MEM_LEN, MEM_WIDTH), lambda b, t: (layer, b, 0, 0)),
        _const((D_MODEL, IN_WIDTH)),
        _layer_const((1, N_BRANCH * D_MODEL), layer),
        _layer_const((CONV_K, CONV_WIDTH), layer),
        _layer_const((1, RET_V_WIDTH), layer),
        _const((CONV_WIDTH, D_MODEL)),
        _const((RET_V_WIDTH, D_MODEL)),
        _const((MEM_WIDTH, D_MODEL)),
        _const((D_MODEL, D_MODEL)),
    ]
    out_specs = [
        pl.BlockSpec((None, tq, D_MODEL), lambda b, t: (b, t, 0)),
        pl.BlockSpec((None, 8, CONV_WIDTH), lambda b, t: (b, 0, 0)),
        pl.BlockSpec((None, None, RET_HEADS, RET_DK, RET_DV), lambda b, t: (layer, b, 0, 0, 0)),
    ]
    out_shape = [
        jax.ShapeDtypeStruct((bp, seq, D_MODEL), F32),
        jax.ShapeDtypeStruct((bp, 8, CONV_WIDTH), F32),
        jax.ShapeDtypeStruct((DEPTH, bp, RET_HEADS, RET_DK, RET_DV), F32),
    ]
    assert len(in_specs) == MIXER_INPUTS and len(out_specs) == MIXER_OUTPUTS
    args = (tabs["cd"], x, tabs["cos"], tabs["sin"], tabs["dmask"], tabs["qdec"], tabs["kdec"], mk_b, mv_b,
            w["w_in"], w["b_gate"], w["conv_w"], w["ret_gn_g"], w["w_conv_out"], w["w_ret_out"], w["w_mem_out"],
            w["w_out"], *[a for a, _ in to_cast])
    in_specs = in_specs + cast_in_specs
    aliases = {}
    if ret_stack is not None:
        aliases = {0: MIXER_OUTPUTS - 1}
        in_specs = [pl.BlockSpec(memory_space=pl.ANY)] + in_specs
        args = (ret_stack,) + args
    return pl.pallas_call(
        functools.partial(_prompt_mixer_kernel, ret_stack is not None, len(to_cast)),
        grid=grid,
        in_specs=in_specs,
        out_specs=out_specs + cast_out_specs,
        out_shape=out_shape + cast_out_shapes,
        scratch_shapes=[
            pltpu.VMEM((tq, D_MODEL), BF16),
            pltpu.VMEM((tq, RET_QK_WIDTH), BF16),
            pltpu.VMEM((tq, RET_QK_WIDTH), BF16),
            pltpu.VMEM((tq, RET_QK_WIDTH), BF16),
            pltpu.VMEM((tq, RET_V_WIDTH), BF16),
            pltpu.VMEM((tq, RET_HEADS * MIXER_CHUNK), BF16),
            pltpu.VMEM((RET_HEADS, tq // MIXER_CHUNK, RET_DK, RET_DV), F32),
            pltpu.VMEM((tq, CONV_WIDTH), BF16),
            pltpu.VMEM((tq, RET_V_WIDTH), BF16),
            pltpu.VMEM((tq, MEM_WIDTH), BF16),
        ],
        input_output_aliases=aliases,
        compiler_params=pltpu.CompilerParams(
            dimension_semantics=("arbitrary", "arbitrary"), vmem_limit_bytes=VMEM_LIMIT_BYTES),
        name="prompt_mixer",
    )(*args)


FFN_INPUTS = 8
STATE_INPUTS = 13
STATE_OUTPUTS = 5


def _ffn_state_kernel(aliased, *refs):
    if aliased:
        refs = refs[1:]
    z_first_ref, z_next_ref, l1g_ref, l1b_ref, wup_ref, wdn_ref, g_ref, b_ref = refs[:FFN_INPUTS]
    state_in = refs[FFN_INPUTS:FFN_INPUTS + STATE_INPUTS]
    y_ref = refs[FFN_INPUTS + STATE_INPUTS]
    rest = refs[FFN_INPUTS + STATE_INPUTS + 1:]
    x1_scr, next_scr = rest[-2:]
    stages = _sample_state_stages(*state_in, *rest[:-2])
    n_chunks = D_FF // FFN_CHUNK
    assert len(stages) == 2 * n_chunks - 1

    @pl.when(pl.program_id(0) == 0)
    def _():
        x1_scr[...] = _layer_norm(z_first_ref[...], l1g_ref[...], l1b_ref[...])

    next_scr[...] = _layer_norm(z_next_ref[...], l1g_ref[...], l1b_ref[...])
    xb = x1_scr[...].astype(BF16)
    acc = None
    for c in range(n_chunks):
        cols = slice(c * FFN_CHUNK, (c + 1) * FFN_CHUNK)
        h = jnp.maximum(_dot(xb, wup_ref[:, cols]), 0.0)
        stages[2 * c]()
        part = _dot((h * h).astype(BF16), wdn_ref[cols, :])
        if c + 1 < n_chunks:
            stages[2 * c + 1]()
        acc = part if acc is None else acc + part
    y_ref[...] = _layer_norm(DEEPNORM_ALPHA * x1_scr[...] + acc, g_ref[...], b_ref[...])
    x1_scr[...] = next_scr[...]


def _ffn_and_sample_state(layer, z, w, proj, cst, tabs, state_ret, kc, vc, ret_stack):
    n = z.shape[0]
    tm = FFN_TILE
    ns = SAMPLE_SEQS
    n_tok = proj.shape[0]
    nseq = state_ret.shape[1]
    t_len = n_tok // nseq
    rows = ns * t_len
    n_tiles = n // tm
    assert n_tiles == nseq // ns
    tok_spec = lambda width: pl.BlockSpec((rows, width), lambda i: (i, 0))
    in_specs = [
        pl.BlockSpec((tm, D_MODEL), lambda i: (0, 0)),
        pl.BlockSpec((tm, D_MODEL), lambda i: (jnp.minimum(i + 1, n_tiles - 1), 0)),
        _layer_const((1, D_MODEL), layer),
        _layer_const((1, D_MODEL), layer),
        _const((D_MODEL, D_FF)),
        _const((D_FF, D_MODEL)),
        _layer_const((1, D_MODEL), layer),
        _layer_const((1, D_MODEL), layer),
        pl.BlockSpec(memory_space=pltpu.SMEM),
        tok_spec(OFF_GL),
        pl.BlockSpec((None, rows, CONV_WIDTH), lambda i: (layer, i, 0)),
        _const((rows, RET_DK)),
        _const((rows, RET_DK)),
        _const((RET_HEADS, rows, rows)),
        _const((RET_HEADS, t_len, RET_DV)),
        _const((RET_HEADS, rows, RET_DK)),
        pl.BlockSpec((None, ns, RET_HEADS, RET_DK, RET_DV), lambda i: (layer, i, 0, 0, 0)),
        pl.BlockSpec((None, ns, MEM_LEN * MEM_HEADS, MEM_HEAD_DIM), lambda i: (layer, i, 0, 0)),
        pl.BlockSpec((None, ns, MEM_LEN * MEM_HEADS, MEM_HEAD_DIM), lambda i: (layer, i, 0, 0)),
        _layer_const((CONV_K, CONV_WIDTH), layer),
        _layer_const((1, RET_V_WIDTH), layer),
    ]
    out_specs = [
        pl.BlockSpec((tm, D_MODEL), lambda i: (i, 0)),
        tok_spec(CONV_WIDTH),
        tok_spec(RET_V_WIDTH),
        tok_spec(MEM_WIDTH),
        tok_spec(CONV_WIDTH),
        pl.BlockSpec((None, ns, RET_HEADS, RET_DK, RET_DV), lambda i: (layer, i, 0, 0, 0)),
    ]
    out_shape = [
        jax.ShapeDtypeStruct((n, D_MODEL), F32),
        jax.ShapeDtypeStruct((n_tok, CONV_WIDTH), BF16),
        jax.ShapeDtypeStruct((n_tok, RET_V_WIDTH), BF16),
        jax.ShapeDtypeStruct((n_tok, MEM_WIDTH), BF16),
        jax.ShapeDtypeStruct((n_tok, CONV_WIDTH), F32),
        jax.ShapeDtypeStruct((DEPTH, nseq, RET_HEADS, RET_DK, RET_DV), F32),
    ]
    assert len(in_specs) == FFN_INPUTS + STATE_INPUTS and len(out_specs) == 1 + STATE_OUTPUTS
    args = (z, z, w["ln1_g"], w["ln1_b"], w["w_up"], w["w_down"], w["ln2_g"], w["ln2_b"],
            tabs["cd"], proj, cst, tabs["cos"], tabs["sin"], tabs["dmask"], tabs["qdec"], tabs["kdec"],
            state_ret, kc, vc, w["conv_w"], w["ret_gn_g"])
    aliases = {}
    if ret_stack is not None:
        aliases = {0: len(out_shape) - 1}
        in_specs = [pl.BlockSpec(memory_space=pl.ANY)] + in_specs
        args = (ret_stack,) + args
    scratch = [
        pltpu.VMEM((rows, RET_V_WIDTH), F32),
        pltpu.VMEM((rows, MEM_HEADS * MEM_LEN), F32),
        pltpu.VMEM((rows, MEM_WIDTH), F32),
        pltpu.VMEM((tm, D_MODEL), F32),
        pltpu.VMEM((tm, D_MODEL), F32),
    ]
    return pl.pallas_call(
        functools.partial(_ffn_state_kernel, ret_stack is not None),
        grid=(n // tm,),
        in_specs=in_specs,
        out_specs=out_specs,
        out_shape=out_shape,
        scratch_shapes=scratch,
        input_output_aliases=aliases,
        compiler_params=pltpu.CompilerParams(
            dimension_semantics=("arbitrary",), vmem_limit_bytes=VMEM_LIMIT_BYTES),
        name="ffn_state",
    )(*args)


def _sample_proj_kernel(x_ref, w_ref, o_ref):
    o_ref[...] = _dot(x_ref[...].astype(BF16), w_ref[...])


def _sample_proj(x, w_in_b):
    n = x.shape[0]
    tn = SAMPLE_PROJ_COLS
    return pl.pallas_call(
        _sample_proj_kernel,
        grid=(IN_WIDTH // tn,),
        in_specs=[
            pl.BlockSpec((n, D_MODEL), lambda j: (0, 0)),
            pl.BlockSpec((D_MODEL, tn), lambda j: (0, j)),
        ],
        out_specs=pl.BlockSpec((n, tn), lambda j: (0, j)),
        out_shape=jax.ShapeDtypeStruct((n, IN_WIDTH), F32),
        compiler_params=pltpu.CompilerParams(
            dimension_semantics=("arbitrary",), vmem_limit_bytes=VMEM_LIMIT_BYTES),
        name="sample_proj",
    )(x, w_in_b)


def _sample_state_stages(cd_ref, proj_ref, cst_ref, cos_ref, sin_ref, dmask_ref, qdec_ref, kdec_ref,
                         sret_ref, kc_ref, vc_ref, cw_ref, gn_ref,
                         a_ref, b_ref, m_ref, u_ref, nret_ref,
                         o_scr, p_scr, m_scr):
    ns = sret_ref.shape[0]
    t_len = proj_ref.shape[0] // ns
    rows = [slice(s * t_len, (s + 1) * t_len) for s in range(ns)]
    kcols = [slice(h * RET_DK, (h + 1) * RET_DK) for h in range(RET_HEADS)]
    vcols = [slice(h * RET_DV, (h + 1) * RET_DV) for h in range(RET_HEADS)]
    mcols = [slice(h * MEM_HEAD_DIM, (h + 1) * MEM_HEAD_DIM) for h in range(MEM_HEADS)]
    pcols = [slice(h * MEM_LEN, (h + 1) * MEM_LEN) for h in range(MEM_HEADS)]
    head_rows = [pl.ds(h, MEM_LEN, stride=MEM_HEADS) for h in range(MEM_HEADS)]

    q, kd, masked = [], [], []

    def conv_and_scores():
        cb = proj_ref[:, OFF_CB:OFF_CB + CONV_WIDTH]
        u = proj_ref[:, OFF_CC:OFF_CC + CONV_WIDTH] * proj_ref[:, OFF_CH:OFF_CH + CONV_WIDTH]
        tok = lax.broadcasted_iota(jnp.int32, u.shape, 0) & (t_len - 1)
        st2 = cst_ref[...]
        st1 = pltpu.roll(st2, st2.shape[0] - 1, axis=0)
        u1 = jnp.where(tok == 0, st1, pltpu.roll(u, 1, axis=0))
        u2 = jnp.where(tok < 2, st2, pltpu.roll(u, 2, axis=0))
        conv_y = u2 * cw_ref[0:1, :] + u1 * cw_ref[1:2, :] + u * cw_ref[2:3, :]
        u_ref[...] = u
        a_ref[...] = (cb * conv_y).astype(BF16)

        cos = cos_ref[...]
        sin = sin_ref[...]
        for h in range(RET_HEADS):
            qh = _rotary(proj_ref[:, OFF_RQ + h * RET_DK:OFF_RQ + (h + 1) * RET_DK], cos, sin)
            kh = _rotary(proj_ref[:, OFF_RK + h * RET_DK:OFF_RK + (h + 1) * RET_DK], cos, sin) * (RET_DK ** -0.5)
            q.append(qh)
            kd.append(kh * kdec_ref[h])
            masked.append((_dot_nt(qh.astype(BF16), kh.astype(BF16)) * dmask_ref[h]).astype(BF16))

    def intra_chunk():
        for h in range(RET_HEADS):
            vb = proj_ref[:, OFF_RV + h * RET_DV:OFF_RV + (h + 1) * RET_DV].astype(BF16)
            o_scr[:, vcols[h]] = _dot(masked[h], vb)

    def cross_chunk_and_state(seqs):
        for s in seqs:
            for h in range(RET_HEADS):
                s_prev = sret_ref[s, h]
                inter = _dot(q[h][rows[s]].astype(BF16), s_prev.astype(BF16))
                o_scr[rows[s], vcols[h]] += inter * qdec_ref[h]
                vb = proj_ref[rows[s], OFF_RV + h * RET_DV:OFF_RV + (h + 1) * RET_DV].astype(BF16)
                nret_ref[s, h] = s_prev * cd_ref[h] + _dot_tn(kd[h][rows[s]].astype(BF16), vb)

    def attention_scores():
        for s in range(ns):
            for h in range(MEM_HEADS):
                mq = proj_ref[rows[s], OFF_MQ + h * MEM_HEAD_DIM:OFF_MQ + (h + 1) * MEM_HEAD_DIM].astype(BF16)
                p_scr[rows[s], pcols[h]] = (_dot_nt(mq, kc_ref[s, head_rows[h], :].astype(BF16))
                                            * (MEM_HEAD_DIM ** -0.5))

    def attention_values():
        for h in range(MEM_HEADS):
            p_scr[:, pcols[h]] = _softmax_rows(p_scr[:, pcols[h]])
        for s in range(ns):
            for h in range(MEM_HEADS):
                p = p_scr[rows[s], pcols[h]].astype(BF16)
                m_scr[rows[s], mcols[h]] = _dot(p, vc_ref[s, head_rows[h], :].astype(BF16))
        m_ref[...] = m_scr[...].astype(BF16)

    def norm_and_gate():
        for h in range(RET_HEADS):
            rg = proj_ref[:, OFF_RG + h * RET_DV:OFF_RG + (h + 1) * RET_DV]
            gated = jax.nn.silu(rg) * (_group_norm(o_scr[:, vcols[h]]) * gn_ref[:, vcols[h]])
            b_ref[:, vcols[h]] = gated.astype(BF16)

    first, second = list(range(ns // 2)), list(range(ns // 2, ns))
    return [conv_and_scores, intra_chunk,
            functools.partial(cross_chunk_and_state, first), functools.partial(cross_chunk_and_state, second),
            attention_scores, attention_values, norm_and_gate]


def _sample_post_kernel(x_ref, a_ref, b_ref, m_ref, g0_ref, g1_ref, g2_ref,
                        bg_ref, wco_ref, wro_ref, wmo_ref, wo_ref, l1g_ref, l1b_ref,
                        wup_hbm, wdn_hbm, l2g_ref, l2b_ref, y_ref, x1_ref, wup_ref, wdn_ref, sem):
    first_step = pl.program_id(0) == 0

    def weight_copies():
        return (pltpu.make_async_copy(wup_hbm, wup_ref, sem.at[0]),
                pltpu.make_async_copy(wdn_hbm, wdn_ref, sem.at[1]))

    @pl.when(first_step)
    def _():
        for copy in weight_copies():
            copy.start()

    gate_refs = (g0_ref, g1_ref, g2_ref)
    _merge_out_ln(x_ref, (a_ref, b_ref, m_ref), lambda i: gate_refs[i][...], bg_ref,
                  (wco_ref, wro_ref, wmo_ref), wo_ref, (l1g_ref, l1b_ref), x1_ref)

    @pl.when(first_step)
    def _():
        for copy in weight_copies():
            copy.wait()

    y_ref[...] = _ffn_ln(x1_ref[...], wup_ref, wdn_ref, l2g_ref, l2b_ref)


def _sample_post(layer, x, a_pre, b_pre, m_pre, proj, w):
    n = x.shape[0]
    tm = SAMPLE_POST_TILE
    tok = lambda width: pl.BlockSpec((tm, width), lambda i: (i, 0))
    gate = lambda k: pl.BlockSpec((tm, D_MODEL), lambda i: (i, OFF_GL // D_MODEL + k))
    return pl.pallas_call(
        _sample_post_kernel,
        grid=(n // tm,),
        in_specs=[
            tok(D_MODEL), tok(CONV_WIDTH), tok(RET_V_WIDTH), tok(MEM_WIDTH), gate(0), gate(1), gate(2),
            _layer_const((1, N_BRANCH * D_MODEL), layer),
            _const((CONV_WIDTH, D_MODEL)),
            _const((RET_V_WIDTH, D_MODEL)),
            _const((MEM_WIDTH, D_MODEL)),
            _const((D_MODEL, D_MODEL)),
            _layer_const((1, D_MODEL), layer),
            _layer_const((1, D_MODEL), layer),
            pl.BlockSpec(memory_space=pl.ANY),
            pl.BlockSpec(memory_space=pl.ANY),
            _layer_const((1, D_MODEL), layer),
            _layer_const((1, D_MODEL), layer),
        ],
        out_specs=tok(D_MODEL),
        out_shape=jax.ShapeDtypeStruct((n, D_MODEL), F32),
        scratch_shapes=[
            pltpu.VMEM((tm, D_MODEL), F32),
            pltpu.VMEM((D_MODEL, D_FF), BF16),
            pltpu.VMEM((D_FF, D_MODEL), BF16),
            pltpu.SemaphoreType.DMA((2,)),
        ],
        compiler_params=pltpu.CompilerParams(
            dimension_semantics=("arbitrary",), vmem_limit_bytes=VMEM_LIMIT_BYTES),
        name="sample_post",
    )(x, a_pre, b_pre, m_pre, proj, proj, proj,
      w["b_gate"], w["w_conv_out"], w["w_ret_out"], w["w_mem_out"], w["w_out"], w["ln1_g"], w["ln1_b"],
      w["w_up"], w["w_down"], w["ln2_g"], w["ln2_b"])


def _rotary_tables(pos, reps):
    half = RET_DK // 2
    inv = ROPE_BASE ** (-jnp.arange(half, dtype=F32) / half)
    ang = pos.astype(F32)[:, None] * inv[None, :]
    cos = jnp.cos(ang)
    sin = jnp.sin(ang)
    cos_full = jnp.concatenate([cos, cos], axis=-1)
    sin_signed = jnp.concatenate([-sin, sin], axis=-1)
    return jnp.tile(cos_full, (reps, 1)), jnp.tile(sin_signed, (reps, 1))


def _decay_tables(c):
    h = RET_HEADS
    log_g = jnp.log1p(-jnp.exp2(-5.0 - jnp.arange(h, dtype=F32)))
    idx = jnp.arange(c, dtype=F32)
    rel = idx[:, None] - idx[None, :]
    causal = rel >= 0
    dmask = jnp.where(causal[None], jnp.exp(jnp.where(causal, rel, 0.0)[None] * log_g[:, None, None]), 0.0)
    q_decay = jnp.exp((idx + 1.0)[None, :] * log_g[:, None])
    k_decay = jnp.exp((c - 1.0 - idx)[None, :] * log_g[:, None])
    chunk_decay = jnp.exp(c * log_g)
    return {
        "dmask": dmask,
        "qdec": jnp.broadcast_to(q_decay[:, :, None], (h, c, RET_DV)),
        "kdec": jnp.broadcast_to(k_decay[:, :, None], (h, c, RET_DK)),
        "cd": chunk_decay,
    }


def kernel(x_prompt, x_sample, cache_mem_k, cache_mem_v, state_conv, state_ret, mem_prompt,
           w_in, b_gate, conv_w, ret_gn_g, w_conv_out, w_ret_out, w_mem_out, w_out, w_mem_kv,
           ln1_g, ln1_b, w_up, w_down, ln2_g, ln2_b):
    bp, seq, _ = x_prompt.shape
    nseq, t_len, _ = x_sample.shape
    assert seq % PROMPT_TILE == 0 and PROMPT_TILE % MIXER_CHUNK == 0
    assert t_len == 8 and nseq % SAMPLE_SEQS == 0 and t_len % RET_CHUNK != 0

    mixer_w = {"w_in": w_in, "w_conv_out": w_conv_out, "w_ret_out": w_ret_out, "w_mem_out": w_mem_out, "w_out": w_out}
    mlp_w = {"w_up": w_up, "w_down": w_down}
    small = {name: a.reshape(DEPTH, 1, a.shape[-1]) for name, a in
             {"b_gate": b_gate, "ret_gn_g": ret_gn_g, "ln1_g": ln1_g, "ln1_b": ln1_b,
              "ln2_g": ln2_g, "ln2_b": ln2_b}.items()}

    tabs_p = _decay_tables(MIXER_CHUNK)
    tabs_p["cos"], tabs_p["sin"] = _rotary_tables(jnp.arange(seq, dtype=jnp.int32), 1)
    tabs_s = _decay_tables(t_len)
    eye = jnp.eye(SAMPLE_SEQS, dtype=F32)
    tabs_s["dmask"] = (eye[None, :, None, :, None] * tabs_s["dmask"][:, None, :, None, :]).reshape(
        RET_HEADS, SAMPLE_SEQS * t_len, SAMPLE_SEQS * t_len)
    tabs_s["kdec"] = jnp.tile(tabs_s["kdec"], (1, SAMPLE_SEQS, 1))
    tabs_s["cos"], tabs_s["sin"] = _rotary_tables(PAST_LEN + jnp.arange(t_len, dtype=jnp.int32), SAMPLE_SEQS)

    mk_f, mv_f, mk_b, mv_b, *cast = _mem_kv(mem_prompt, w_mem_kv, list(mixer_w.values()))
    mixer_b = dict(zip(mixer_w, cast))
    mk_b = mk_b.reshape(DEPTH, bp, MEM_LEN, MEM_WIDTH)
    mv_b = mv_b.reshape(DEPTH, bp, MEM_LEN, MEM_WIDTH)

    kc = cache_mem_k.reshape(DEPTH, nseq, MEM_LEN * MEM_HEADS, MEM_HEAD_DIM)
    vc = cache_mem_v.reshape(DEPTH, nseq, MEM_LEN * MEM_HEADS, MEM_HEAD_DIM)
    cst = jnp.concatenate(
        [state_conv, jnp.zeros((DEPTH, nseq, t_len - (CONV_K - 1), CONV_WIDTH), state_conv.dtype)], axis=2
    ).reshape(DEPTH, nseq * t_len, CONV_WIDTH)

    xp = x_prompt
    xs = x_sample.reshape(nseq * t_len, D_MODEL)
    cp_list, cs_list = [], []
    ret_stack = ret_stack_p = None
    for l in range(DEPTH):
        w = dict(mixer_b, conv_w=conv_w, **small)
        to_cast = [(a, l) for a in mlp_w.values()]
        if l + 1 < DEPTH:
            to_cast += [(a, l + 1) for a in mixer_w.values()]
        zp, ctail_p, ret_stack_p, *cast = _prompt_mixer(l, xp, tabs_p, mk_b, mv_b, w, to_cast, ret_stack_p)
        w.update(zip(mlp_w, cast[:len(mlp_w)]))
        mixer_b = dict(zip(mixer_w, cast[len(mlp_w):]))
        cp_list.append(ctail_p)

        proj = _sample_proj(xs, w["w_in"])
        xp, a_pre, b_pre, m_pre, u_s, ret_stack = _ffn_and_sample_state(
            l, zp.reshape(bp * seq, D_MODEL), w, proj, cst, tabs_s, state_ret, kc, vc, ret_stack)
        xp = xp.reshape(bp, seq, D_MODEL)
        xs = _sample_post(l, xs, a_pre, b_pre, m_pre, proj, w)
        cs_list.append(u_s)

    new_conv_p = jnp.stack(cp_list)[:, :, 8 - (CONV_K - 1):, :]
    new_conv_s = jnp.stack(cs_list).reshape(DEPTH, nseq, t_len, CONV_WIDTH)[:, :, t_len - (CONV_K - 1):, :]
    mem_shape = (DEPTH, bp, MEM_LEN, MEM_HEADS, MEM_HEAD_DIM)
    return (xp, xs.reshape(nseq, t_len, D_MODEL), mk_f.reshape(mem_shape), mv_f.reshape(mem_shape),
            new_conv_p, ret_stack_p, new_conv_s, ret_stack)
```
